```python
import jax
import jax.numpy as jnp
from jax import lax
import numpy as np

D_MODEL = 1024
BATCH = 2
SEQ = 8192
DEPTH = 1

GRID_W = 64
CTX_LEN = 256
HEAD_DIM = 64
NA_HEADS = 8
RW_HEADS = 8
NA_WIDTH = NA_HEADS * HEAD_DIM
RW_WIDTH = RW_HEADS * HEAD_DIM
MIX_WIDTH = NA_WIDTH + RW_WIDTH
WIN_H = 8
WIN_W = 16
ROPE_BASE = 10000.0
N_DIR = 2
DECAY_LORA = 64
AAA_LORA = 64
GATE_LORA = 128
NA_COLS = 3 * NA_WIDTH
RW_COLS = 3 * RW_WIDTH + N_DIR * DECAY_LORA + N_DIR * AAA_LORA + GATE_LORA
IN_COLS = NA_COLS + RW_COLS
D_FF = 4 * D_MODEL
N_MOD = 6
LN_EPS = 1e-6
GN_EPS = 64e-5
ALPHA = (2 * DEPTH) ** 0.25
BETA = (8 * DEPTH) ** -0.25

kernel_name = 'hybrid_natten_rwkv7_dit_layer'


def _normalize(x, eps):
    xf = x.astype(jnp.float32)
    mu = jnp.mean(xf, axis=-1, keepdims=True)
    var = jnp.mean(jnp.square(xf - mu), axis=-1, keepdims=True)
    return (xf - mu) * lax.rsqrt(var + eps)


def layer_norm(x, gain, bias):
    return (_normalize(x, LN_EPS) * gain + bias).astype(x.dtype)


def modulate(x, shift, scale):
    return (_normalize(x, LN_EPS) * (1 + scale) + shift).astype(x.dtype)


def split_heads(z, n_heads):
    return z.reshape(z.shape[:-1] + (n_heads, HEAD_DIM))


def token_shift(p, mu_prev, mu_next):
    prev = jnp.pad(p, ((0, 0), (1, 0), (0, 0)))[:, :-1]
    nxt = jnp.pad(p, ((0, 0), (0, 1), (0, 0)))[:, 1:]
    return p + mu_prev * (prev - p) + mu_next * (nxt - p)


def rope_axis(x, pos):
    f = x.shape[-1] // 2
    inv = ROPE_BASE ** (-jnp.arange(f, dtype=jnp.float32) / f)
    ang = pos[:, None] * inv[None, :]
    cos = jnp.cos(ang)[:, None, :]
    sin = jnp.sin(ang)[:, None, :]
    x1, x2 = x[..., :f], x[..., f:]
    return jnp.concatenate([x1 * cos - x2 * sin, x1 * sin + x2 * cos], axis=-1).astype(x.dtype)


def axial_rope(x, row, col):
    h = x.shape[-1] // 2
    return jnp.concatenate([rope_axis(x[..., :h], row), rope_axis(x[..., h:], col)], axis=-1)


def neighbourhood_attention(q, k, v, kc, vc, rpb):
    B, T, H, Dh = q.shape
    rows = T // GRID_W
    kh = min(WIN_H, rows)
    kw = WIN_W
    t = jnp.arange(T)
    row = (t // GRID_W).astype(jnp.float32)
    col = (t % GRID_W).astype(jnp.float32)
    q_rot = axial_rope(q, row, col)
    k_rot = axial_rope(k, row, col)

    def to_grid(a):
        return a.reshape(B, rows, GRID_W, H, Dh).transpose(0, 3, 1, 2, 4)

    k_grid = to_grid(k_rot)
    v_grid = to_grid(v)
    q_rows = to_grid(q_rot).transpose(2, 0, 1, 3, 4)
    qp_rows = to_grid(q).transpose(2, 0, 1, 3, 4)
    kc_h = kc.transpose(0, 2, 1, 3)
    vc_h = vc.transpose(0, 2, 1, 3)
    cols = jnp.arange(GRID_W)
    col_idx = jnp.clip(cols - kw // 2, 0, GRID_W - kw)[:, None] + jnp.arange(kw)[None, :]
    col_off = col_idx - cols[:, None] + (WIN_W - 1)
    scale = Dh ** -0.5

    def one_row(args):
        i, q_i, qp_i = args
        r0 = jnp.clip(i - kh // 2, 0, rows - kh)
        row_off = r0 + jnp.arange(kh) - i + (WIN_H - 1)
        bias = rpb[:, row_off][:, :, col_off].transpose(0, 2, 1, 3)
        k_nb = lax.dynamic_slice_in_dim(k_grid, r0, kh, axis=2)[:, :, :, col_idx]
        v_nb = lax.dynamic_slice_in_dim(v_grid, r0, kh, axis=2)[:, :, :, col_idx]
        s_loc = jnp.einsum('bhqd,bhrqkd->bhqrk', q_i, k_nb) * scale + bias
        s_ctx = jnp.einsum('bhqd,bhld->bhql', qp_i, kc_h) * scale
        s = jnp.concatenate([s_loc.reshape(B, H, GRID_W, kh * kw), s_ctx], axis=-1).astype(jnp.float32)
        p = jax.nn.softmax(s, axis=-1).astype(v.dtype)
        p_loc = p[..., :kh * kw].reshape(B, H, GRID_W, kh, kw)
        p_ctx = p[..., kh * kw:]
        return (jnp.einsum('bhqrk,bhrqkd->bhqd', p_loc, v_nb)
                + jnp.einsum('bhql,bhld->bhqd', p_ctx, vc_h))

    out = lax.map(one_row, (jnp.arange(rows), q_rows, qp_rows))
    return out.transpose(1, 0, 3, 2, 4).reshape(B, T, H * Dh)


def context_attention(qc, kc, vc):
    B, L, H, Dh = qc.shape
    s = (jnp.einsum('bqhd,bkhd->bhqk', qc, kc) * Dh ** -0.5).astype(jnp.float32)
    p = jax.nn.softmax(s, axis=-1).astype(vc.dtype)
    return jnp.einsum('bhqk,bkhd->bqhd', p, vc).reshape(B, L, H * Dh)


def rwkv_terms(p, w0, w2, a0, a2, g2, k_k, k_a):
    B, T, _ = p.shape
    cuts = [RW_WIDTH, 2 * RW_WIDTH, 3 * RW_WIDTH,
            3 * RW_WIDTH + N_DIR * DECAY_LORA,
            3 * RW_WIDTH + N_DIR * (DECAY_LORA + AAA_LORA)]
    r, k, v, pw, pa, pg = jnp.split(p, cuts, axis=-1)
    pw = pw.reshape(B, T, N_DIR, DECAY_LORA)
    pa = pa.reshape(B, T, N_DIR, AAA_LORA)
    w = -jax.nn.softplus(-(w0 + jnp.einsum('btdr,drc->btdc', jnp.tanh(pw), w2))) - 0.5
    decay = jnp.exp(-jnp.exp(w))
    a = jax.nn.sigmoid(a0 + jnp.einsum('btdr,drc->btdc', pa, a2))
    g = jax.nn.sigmoid(pg) @ g2
    kk = split_heads(k * k_k, RW_HEADS)
    kk = kk / jnp.maximum(jnp.sqrt(jnp.sum(jnp.square(kk), axis=-1, keepdims=True)), 1e-12)
    kd = k[:, :, None] * (1 + (a - 1) * k_a)
    return (split_heads(r, RW_HEADS), split_heads(decay, RW_HEADS), split_heads(kd, RW_HEADS),
            split_heads(v, RW_HEADS), kk, split_heads(a, RW_HEADS), g)


def rwkv_scan(s0, terms, d, reverse):
    r, decay, kd, v, kk, a, _ = terms
    xs = tuple(z.astype(jnp.float32).transpose(1, 0, 2, 3)
               for z in (r, decay[:, :, d], kd[:, :, d], v, kk, a[:, :, d]))

    def step(S, inp):
        r_t, w_t, k_t, v_t, kk_t, a_t = inp
        s_kk = jnp.einsum('bhij,bhj->bhi', S, kk_t)
        S = (S * w_t[:, :, None, :] - s_kk[..., None] * (kk_t * a_t)[:, :, None, :]
             + v_t[..., None] * k_t[:, :, None, :])
        return S, jnp.einsum('bhij,bhj->bhi', S, r_t)

    S, y = lax.scan(step, s0, xs, reverse=reverse)
    return S, y.transpose(1, 0, 2, 3)


def rwkv_readout(y, terms, r_k, gn_g, gn_b):
    r, _, kd, v, _, _, g = terms
    B, T = r.shape[:2]
    yn = _normalize(y, GN_EPS).reshape(B, T, RW_WIDTH) * gn_g + gn_b
    bonus = jnp.sum(jnp.sum(r[:, :, None] * kd * r_k, axis=-1, keepdims=True) * v[:, :, None], axis=2)
    return ((yn + bonus.reshape(B, T, RW_WIDTH)) * g).astype(v.dtype)


def rwkv_mixer(p, pc, w0, w2, a0, a2, g2, k_k, k_a, r_k, gn_g, gn_b, update_ctx):
    lat = rwkv_terms(p, w0, w2, a0, a2, g2, k_k, k_a)
    ctx = rwkv_terms(pc, w0, w2, a0, a2, g2, k_k, k_a)
    B = p.shape[0]
    y_lat, y_ctx = [], []
    for d in range(N_DIR):
        s0 = jnp.zeros((B, RW_HEADS, HEAD_DIM, HEAD_DIM), jnp.float32)
        s_ctx, yc = rwkv_scan(s0, ctx, d, d == 1)
        _, yl = rwkv_scan(s_ctx, lat, d, d == 1)
        y_lat.append(yl)
        y_ctx.append(yc)
    out = rwkv_readout(y_lat[0] + y_lat[1], lat, r_k, gn_g, gn_b)
    out_c = rwkv_readout(y_ctx[0] + y_ctx[1], ctx, r_k, gn_g, gn_b) if update_ctx else None
    return out, out_c


def squared_relu_mlp(h, w1, w2):
    return jnp.square(jax.nn.relu(h @ w1)) @ w2


def trunk_layer(x, xc, c, c_ctx, w_mod, b_mod, w_in, w_out, ln1_g, ln1_b, mlp_w1, mlp_w2,
                ln2_g, ln2_b, na_rpb, rw_mu_prev, rw_mu_next, rw_w0, rw_w2, rw_a0, rw_a2, rw_g2,
                rw_k_k, rw_k_a, rw_r_k, rw_gn_g, rw_gn_b, update_ctx):
    mod = jax.nn.silu(c) @ w_mod + b_mod
    mod_c = jax.nn.silu(c_ctx) @ w_mod + b_mod
    sh1, sc1, g1, sh2, sc2, g2 = jnp.split(mod[:, None, :], N_MOD, axis=-1)
    sh1c, sc1c, g1c, sh2c, sc2c, g2c = jnp.split(mod_c, N_MOD, axis=-1)

    p = modulate(x, sh1, sc1) @ w_in
    pc = modulate(xc, sh1c, sc1c) @ w_in
    q, k, v = [split_heads(z, NA_HEADS) for z in jnp.split(p[..., :NA_COLS], 3, axis=-1)]
    qc, kc, vc = [split_heads(z, NA_HEADS) for z in jnp.split(pc[..., :NA_COLS], 3, axis=-1)]
    y_na = neighbourhood_attention(q, k, v, kc, vc, na_rpb)
    p_rw = token_shift(p[..., NA_COLS:], rw_mu_prev, rw_mu_next)
    pc_rw = token_shift(pc[..., NA_COLS:], rw_mu_prev, rw_mu_next)
    y_rw, yc_rw = rwkv_mixer(p_rw, pc_rw, rw_w0, rw_w2, rw_a0, rw_a2, rw_g2, rw_k_k, rw_k_a,
                             rw_r_k, rw_gn_g, rw_gn_b, update_ctx)

    y = jnp.concatenate([y_na, y_rw], axis=-1) @ w_out
    x = layer_norm(ALPHA * x + g1 * y, ln1_g, ln1_b)
    x = layer_norm(ALPHA * x + g2 * squared_relu_mlp(modulate(x, sh2, sc2), mlp_w1, mlp_w2), ln2_g, ln2_b)
    if update_ctx:
        yc = jnp.concatenate([context_attention(qc, kc, vc), yc_rw], axis=-1) @ w_out
        xc = layer_norm(ALPHA * xc + g1c * yc, ln1_g, ln1_b)
        xc = layer_norm(ALPHA * xc + g2c * squared_relu_mlp(modulate(xc, sh2c, sc2c), mlp_w1, mlp_w2),
                        ln2_g, ln2_b)
    return x, xc


def setup_inputs(seed: int = 0) -> dict:
    key = jax.random.key(seed)
    ks = jax.random.split(key, 27)
    L = DEPTH

    def nrm(k, shape, s):
        return jax.random.normal(k, shape, jnp.float32) * s

    def uni(k, shape, lo, hi):
        return jax.random.uniform(k, shape, jnp.float32, lo, hi)

    return {
        'x': nrm(ks[0], (BATCH, SEQ, D_MODEL), 1.0),
        'c': nrm(ks[1], (BATCH, D_MODEL), 1.0),
        'ctx': nrm(ks[2], (BATCH, CTX_LEN, D_MODEL), 1.0),
        'c_ctx': nrm(ks[3], (D_MODEL,), 1.0),
        'w_mod': nrm(ks[4], (L, D_MODEL, N_MOD * D_MODEL), 0.5 * D_MODEL ** -0.5),
        'b_mod': nrm(ks[5], (L, N_MOD * D_MODEL), 0.01),
        'w_in': nrm(ks[6], (L, D_MODEL, IN_COLS), D_MODEL ** -0.5),
        'w_out': nrm(ks[7], (L, MIX_WIDTH, D_MODEL), BETA * MIX_WIDTH ** -0.5),
        'ln1_g': 1.0 + nrm(ks[8], (L, D_MODEL), 0.01),
        'ln1_b': nrm(ks[9], (L, D_MODEL), 0.01),
        'mlp_w1': nrm(ks[10], (L, D_MODEL, D_FF), D_MODEL ** -0.5),
        'mlp_w2': nrm(ks[11], (L, D_FF, D_MODEL), BETA * D_FF ** -0.5),
        'ln2_g': 1.0 + nrm(ks[12], (L, D_MODEL), 0.01),
        'ln2_b': nrm(ks[13], (L, D_MODEL), 0.01),
        'na_rpb': nrm(ks[14], (L, NA_HEADS, 2 * WIN_H - 1, 2 * WIN_W - 1), 0.1),
        'rw_mu_prev': uni(ks[15], (L, RW_COLS), 0.0, 0.5),
        'rw_mu_next': uni(ks[16], (L, RW_COLS), 0.0, 0.5),
        'rw_w0': uni(ks[17], (L, N_DIR, RW_WIDTH), -3.0, 0.0),
        'rw_w2': nrm(ks[18], (L, N_DIR, DECAY_LORA, RW_WIDTH), 0.5 * DECAY_LORA ** -0.5),
        'rw_a0': nrm(ks[19], (L, N_DIR, RW_WIDTH), 0.1),
        'rw_a2': nrm(ks[20], (L, N_DIR, AAA_LORA, RW_WIDTH), 0.5 * AAA_LORA ** -0.5),
        'rw_g2': nrm(ks[21], (L, GATE_LORA, RW_WIDTH), GATE_LORA ** -0.5),
        'rw_k_k': 0.85 + nrm(ks[22], (L, RW_WIDTH), 0.02),
        'rw_k_a': 1.0 + nrm(ks[23], (L, RW_WIDTH), 0.02),
        'rw_r_k': nrm(ks[24], (L, RW_HEADS, HEAD_DIM), 0.1),
        'rw_gn_g': 1.0 + nrm(ks[25], (L, RW_WIDTH), 0.01),
        'rw_gn_b': nrm(ks[26], (L, RW_WIDTH), 0.01),
    }


def reference(x, c, ctx, c_ctx, w_mod, b_mod, w_in, w_out, ln1_g, ln1_b, mlp_w1, mlp_w2,
              ln2_g, ln2_b, na_rpb, rw_mu_prev, rw_mu_next, rw_w0, rw_w2, rw_a0, rw_a2, rw_g2,
              rw_k_k, rw_k_a, rw_r_k, rw_gn_g, rw_gn_b):
    xc = ctx
    for l in range(DEPTH):
        x, xc = trunk_layer(x, xc, c, c_ctx, w_mod[l], b_mod[l], w_in[l], w_out[l], ln1_g[l], ln1_b[l],
                            mlp_w1[l], mlp_w2[l], ln2_g[l], ln2_b[l], na_rpb[l], rw_mu_prev[l],
                            rw_mu_next[l], rw_w0[l], rw_w2[l], rw_a0[l], rw_a2[l], rw_g2[l],
                            rw_k_k[l], rw_k_a[l], rw_r_k[l], rw_gn_g[l], rw_gn_b[l],
                            update_ctx=l < DEPTH - 1)
    return x
```

```python
import functools
import math

import jax
import jax.numpy as jnp
from jax import lax
from jax.experimental import pallas as pl
from jax.experimental.pallas import tpu as pltpu

HEAD_DIM = 64
NA_HEADS = 8
RW_HEADS = 8
NA_WIDTH = NA_HEADS * HEAD_DIM
RW_WIDTH = RW_HEADS * HEAD_DIM
GRID_W = 64
WIN_H = 8
WIN_W = 16
ROPE_BASE = 10000.0
N_DIR = 2
DECAY_LORA = 64
AAA_LORA = 64
GATE_LORA = 128
LORA_W = N_DIR * DECAY_LORA
NA_COLS = 3 * NA_WIDTH
RW_COLS = 3 * RW_WIDTH + 3 * LORA_W
N_MOD = 6
LN_EPS = 1e-6
GN_EPS = 64e-5
NEG_BIAS = -1e30

CHUNK = 64
QUAD = 4 * HEAD_DIM
HALO = 8
Q_ROWS = 8
KV_GROUP = 4
VMEM_LIMIT = 56 * 1024 * 1024

F32 = jnp.float32
BF16 = jnp.bfloat16


def _cparams(*sem):
    return pltpu.CompilerParams(dimension_semantics=sem, vmem_limit_bytes=VMEM_LIMIT)


def _dot(a, b):
    return jnp.dot(a, b, preferred_element_type=F32)


def _dot_nt(a, b):
    return lax.dot_general(a, b, (((1,), (1,)), ((), ())), preferred_element_type=F32)


def _dot_tn(a, b):
    return lax.dot_general(a, b, (((0,), (0,)), ((), ())), preferred_element_type=F32)


def _split(x, pieces):
    out = []
    for _ in range(pieces):
        p = x.astype(BF16)
        out.append(p)
        x = x - p.astype(F32)
    return out


def _dot_lhs_split(x, w_bf16, pieces):
    acc = None
    for p in _split(x, pieces):
        t = _dot(p, w_bf16)
        acc = t if acc is None else acc + t
    return acc


def _dot_x3(a, b):
    a_hi, a_lo = _split(a, 2)
    b_hi, b_lo = _split(b, 2)
    return _dot(a_hi, b_hi) + (_dot(a_lo, b_hi) + _dot(a_hi, b_lo))


def _normalize(x, eps):
    mu = jnp.mean(x, axis=-1, keepdims=True)
    xc = x - mu
    var = jnp.mean(xc * xc, axis=-1, keepdims=True)
    return xc * lax.rsqrt(var + eps)


def _mod_kernel(c_ref, w_ref, b_ref, o_ref):
    c = c_ref[...]
    s = c * jax.nn.sigmoid(c)
    o_ref[...] = _dot_x3(s, w_ref[...]) + b_ref[...]


def _mod_call(cc, w_mod, b_mod):
    rows, d = cc.shape
    n = w_mod.shape[1]
    tn = 512
    return pl.pallas_call(
        _mod_kernel,
        grid=(n // tn,),
        in_specs=[pl.BlockSpec((rows, d), lambda j: (0, 0)),
                  pl.BlockSpec((d, tn), lambda j: (0, j)),
                  pl.BlockSpec((1, tn), lambda j: (0, j))],
        out_specs=pl.BlockSpec((rows, tn), lambda j: (0, j)),
        out_shape=jax.ShapeDtypeStruct((rows, n), F32),
        compiler_params=_cparams("arbitrary"),
        name="mod",
    )(cc, w_mod, b_mod.reshape(1, n))


def _inproj_kernel(d_model, x_ref, mod_ref, cos_ref, sa_ref, sb_ref, w_ref, *refs):
    q_ref, qr_ref, kr_ref, v_ref, rw_ref = refs[-5:]
    x = x_ref[...]
    mod = mod_ref[...]
    shift = mod[:, 0:d_model]
    scale = mod[:, d_model:2 * d_model]
    xm = (_normalize(x, LN_EPS) * (1.0 + scale) + shift).astype(BF16)

    reps = NA_WIDTH // cos_ref.shape[1]
    cos = jnp.concatenate([cos_ref[...]] * reps, axis=1)
    sa = jnp.concatenate([sa_ref[...]] * reps, axis=1)
    sb = jnp.concatenate([sb_ref[...]] * reps, axis=1)

    def rope(z):
        up = pltpu.roll(z, NA_WIDTH - HEAD_DIM // 4, 1)
        down = pltpu.roll(z, HEAD_DIM // 4, 1)
        return z * cos + up * sa + down * sb

    qk_scale = HEAD_DIM ** -0.5
    q = _dot(xm, w_ref[:, 0:NA_WIDTH]) * qk_scale
    q_ref[...] = q.astype(BF16)
    qr_ref[...] = rope(q).astype(BF16)
    k = _dot(xm, w_ref[:, NA_WIDTH:2 * NA_WIDTH])
    kr_ref[...] = rope(k).astype(BF16)
    v_ref[...] = _dot(xm, w_ref[:, 2 * NA_WIDTH:NA_COLS]).astype(BF16)
    rw_ref[...] = _dot(xm, w_ref[:, NA_COLS:NA_COLS + RW_COLS])


def _inproj_call(x2d, mod3, tables, w_in_bf16, *, tm, mod_index, table_index, rw_index, rw_rows, rw_prev=None):
    rows, d = x2d.shape
    cos, sa, sb = tables
    tw = cos.shape[1]
    ncols = w_in_bf16.shape[1]
    na_spec = pl.BlockSpec((tm, NA_WIDTH), lambda i: (i, 0))
    in_specs = [pl.BlockSpec((tm, d), lambda i: (i, 0)),
                pl.BlockSpec((None, 1, mod3.shape[2]), lambda i: (mod_index(i), 0, 0)),
                pl.BlockSpec((tm, tw), lambda i: (table_index(i), 0)),
                pl.BlockSpec((tm, tw), lambda i: (table_index(i), 0)),
                pl.BlockSpec((tm, tw), lambda i: (table_index(i), 0)),
                pl.BlockSpec((d, ncols), lambda i: (0, 0))]
    args = [x2d, mod3, cos, sa, sb, w_in_bf16]
    aliases = {}
    if rw_prev is not None:
        in_specs.append(pl.BlockSpec(memory_space=pl.ANY))
        args.append(rw_prev)
        aliases = {len(args) - 1: 4}
    na_shape = jax.ShapeDtypeStruct((rows, NA_WIDTH), BF16)
    return pl.pallas_call(
        functools.partial(_inproj_kernel, d),
        grid=(rows // tm,),
        in_specs=in_specs,
        out_specs=[na_spec, na_spec, na_spec, na_spec,
                   pl.BlockSpec((tm, RW_COLS), lambda i: (rw_index(i), 0))],
        out_shape=[na_shape, na_shape, na_shape, na_shape,
                   jax.ShapeDtypeStruct((rw_rows, RW_COLS), F32)],
        input_output_aliases=aliases,
        compiler_params=_cparams("parallel"),
        name="inproj_ctx" if rw_prev is not None else "inproj",
    )(*args)


def _na_kernel(grid_rows, q_ref, qr_ref, k0, k1, k2, k3, v0, v1, v2, v3, kc_ref, vc_ref, bias_ref,
               o_ref, kwin, vwin):
    i = pl.program_id(1)
    grp = KV_GROUP * GRID_W
    for g, (kr, vr) in enumerate(((k0, v0), (k1, v1), (k2, v2), (k3, v3))):
        kwin[g * grp:(g + 1) * grp, :] = kr[...]
        vwin[g * grp:(g + 1) * grp, :] = vr[...]
    n_groups = grid_rows // KV_GROUP
    win_row0 = KV_GROUP * jnp.clip(2 * i - 1, 0, n_groups - 4)
    win_keys = WIN_H * GRID_W
    lane = lax.broadcasted_iota(jnp.int32, (GRID_W, 2 * HEAD_DIM), 1)
    even = lane < HEAD_DIM

    def row_body(j, carry):
        irow = Q_ROWS * i + j
        r0 = jnp.clip(irow - WIN_H // 2, 0, grid_rows - WIN_H)
        koff = pl.multiple_of((r0 - win_row0) * GRID_W, GRID_W)
        brow = (WIN_H - 1) - (irow - r0)
        qoff = pl.multiple_of(j * GRID_W, GRID_W)
        for pair in range(NA_HEADS // 2):
            lanes = slice(pair * 2 * HEAD_DIM, (pair + 1) * 2 * HEAD_DIM)
            qr = qr_ref[pl.ds(qoff, GRID_W), lanes]
            qp = q_ref[pl.ds(qoff, GRID_W), lanes]
            zero = jnp.zeros_like(qr)
            qr2 = jnp.concatenate([jnp.where(even, qr, zero), jnp.where(even, zero, qr)], axis=0)
            qp2 = jnp.concatenate([jnp.where(even, qp, zero), jnp.where(even, zero, qp)], axis=0)
            kk = kwin[pl.ds(koff, win_keys), lanes]
            vv = vwin[pl.ds(koff, win_keys), lanes]
            bias = jnp.concatenate(
                [jnp.concatenate([bias_ref[brow + 2 * m, 2 * pair], bias_ref[brow + 2 * m, 2 * pair + 1]], axis=0)
                 for m in range(WIN_H // 2)], axis=1)
            s_loc = _dot_nt(qr2, kk) + bias
            s_ctx = _dot_nt(qp2, kc_ref[:, lanes])
            mx = jnp.maximum(jnp.max(s_loc, axis=-1, keepdims=True), jnp.max(s_ctx, axis=-1, keepdims=True))
            p_loc = jnp.exp(s_loc - mx)
            p_ctx = jnp.exp(s_ctx - mx)
            denom = jnp.sum(p_loc, axis=-1, keepdims=True) + jnp.sum(p_ctx, axis=-1, keepdims=True)
            o = (_dot(p_loc.astype(BF16), vv) + _dot(p_ctx.astype(BF16), vc_ref[:, lanes])) / denom
            o_ref[pl.ds(qoff, GRID_W), lanes] = jnp.where(even, o[0:GRID_W], o[GRID_W:2 * GRID_W]).astype(BF16)
        return carry

    lax.fori_loop(0, Q_ROWS, row_body, 0)


def _na_call(q, qr, kr, v, kc, vc, bias_tab, *, batch, seq, ctx_len):
    grid_rows = seq // GRID_W
    n_blocks = grid_rows // Q_ROWS
    n_groups = grid_rows // KV_GROUP
    blk = Q_ROWS * GRID_W
    grp = KV_GROUP * GRID_W

    def group_spec(g):
        return pl.BlockSpec((grp, NA_WIDTH),
                            lambda b, i: (b * n_groups + jnp.clip(2 * i - 1, 0, n_groups - 4) + g, 0))

    q_spec = pl.BlockSpec((blk, NA_WIDTH), lambda b, i: (b * n_blocks + i, 0))
    c_spec = pl.BlockSpec((ctx_len, NA_WIDTH), lambda b, i: (b, 0))
    return pl.pallas_call(
        functools.partial(_na_kernel, grid_rows),
        grid=(batch, n_blocks),
        in_specs=[q_spec, q_spec] + [group_spec(g) for g in range(4)] + [group_spec(g) for g in range(4)]
                 + [c_spec, c_spec, pl.BlockSpec(bias_tab.shape, lambda b, i: (0, 0, 0, 0))],
        out_specs=q_spec,
        out_shape=jax.ShapeDtypeStruct((batch * seq, NA_WIDTH), BF16),
        scratch_shapes=[pltpu.VMEM((4 * grp, NA_WIDTH), BF16), pltpu.VMEM((4 * grp, NA_WIDTH), BF16)],
        compiler_params=_cparams("parallel", "arbitrary"),
        name="natten",
    )(q, qr, kr, kr, kr, kr, v, v, v, v, kc, vc, bias_tab)


def _block_diag(x_cat, bd_mask):
    xb = x_cat.astype(BF16)
    tiled = jnp.concatenate([xb] * (QUAD // CHUNK), axis=0)
    return jnp.where(bd_mask, tiled, jnp.zeros_like(tiled))


def _fold(x_bd):
    out = x_bd[0:HEAD_DIM]
    for h in range(1, QUAD // HEAD_DIM):
        out = out + x_bd[h * HEAD_DIM:(h + 1) * HEAD_DIM]
    return out


def _rwkv_kernel(reverse, with_gate, n_ctx, n_chunks,
                 main_ref, prev_ref, next_ref, mup_ref, mun_ref, w0_ref, w2_ref, a0_ref, a2_ref, g2_ref,
                 kk_ref, ka_ref, rk_ref, ones_ref, *refs):
    if with_gate:
        y_ref, bonus_ref, gate_ref, h_ref = refs
    else:
        y_ref, bonus_ref, h_ref = refs
    s = pl.program_id(1)
    if reverse:
        n = jnp.where(s < n_ctx, n_ctx - 1 - s, n_chunks + n_ctx - 1 - s)
    else:
        n = s

    @pl.when(s == 0)
    def _():
        h_ref[...] = jnp.zeros_like(h_ref)

    p = main_ref[...]
    row = lax.broadcasted_iota(jnp.int32, (CHUNK, 1), 0)
    has_prev = jnp.logical_and(n != 0, n != n_ctx)
    has_next = jnp.logical_and(n != n_ctx - 1, n != n_chunks - 1)
    prow = jnp.where(has_prev, prev_ref[HALO - 1:HALO, :], 0.0)
    nrow = jnp.where(has_next, next_ref[0:1, :], 0.0)
    prev = jnp.where(row == 0, prow, pltpu.roll(p, 1, 0))
    nxt = jnp.where(row == CHUNK - 1, nrow, pltpu.roll(p, CHUNK - 1, 0))
    xs = p + mup_ref[...] * (prev - p) + mun_ref[...] * (nxt - p)

    w = RW_WIDTH
    r = xs[:, 0:w]
    k = xs[:, w:2 * w]
    v = xs[:, 2 * w:3 * w]
    pw = xs[:, 3 * w:3 * w + LORA_W]
    pa = xs[:, 3 * w + LORA_W:3 * w + 2 * LORA_W]
    ones = ones_ref[...]

    z = -(w0_ref[...] + _dot_x3(jnp.tanh(pw), w2_ref[...]))
    softplus = jnp.maximum(z, 0.0) + jnp.log1p(jnp.exp(-jnp.abs(z)))
    log_decay = -jnp.exp(-softplus - 0.5)
    a = jax.nn.sigmoid(a0_ref[...] + _dot_x3(pa, a2_ref[...]))
    kk_raw = k * kk_ref[...]
    kk = kk_raw / jnp.maximum(jnp.sqrt(_dot_lhs_split(kk_raw * kk_raw, ones, 2)), 1e-12)
    kd = k * (1.0 + (a - 1.0) * ka_ref[...])
    bb = a * kk
    bonus_ref[...] = _dot_lhs_split(r * kd * rk_ref[...], ones, 2) * v
    if with_gate:
        pg = xs[:, 3 * w + 2 * LORA_W:3 * w + 3 * LORA_W]
        gate_ref[...] = _dot_x3(jax.nn.sigmoid(pg), g2_ref[...])

    ti = lax.broadcasted_iota(jnp.int32, (CHUNK, CHUNK), 0)
    si = lax.broadcasted_iota(jnp.int32, (CHUNK, CHUNK), 1)
    tri = jnp.where((si >= ti) if reverse else (si <= ti), 1.0, 0.0).astype(BF16)
    lw_parts = _split(log_decay, 3)
    cum = _dot(tri, lw_parts[0]) + (_dot(tri, lw_parts[1]) + _dot(tri, lw_parts[2]))
    last = 0 if reverse else CHUNK - 1
    e_tot = jnp.exp(cum[last:last + 1, :])
    e_neg = jnp.exp(-cum)
    r_t = r * jnp.exp(cum)
    kk_t = kk * jnp.exp(cum - log_decay)
    k_h = kd * e_neg
    b_h = bb * e_neg
    k_e = k_h * e_tot
    b_e = b_h * e_tot

    trow = lax.broadcasted_iota(jnp.int32, (CHUNK, QUAD), 0)
    tcol = lax.broadcasted_iota(jnp.int32, (CHUNK, QUAD), 1) % CHUNK
    strict = (trow < tcol) if reverse else (trow > tcol)
    incl = (trow <= tcol) if reverse else (trow >= tcol)
    eye_cat = jnp.where(trow == tcol, 1.0, 0.0)
    brow = lax.broadcasted_iota(jnp.int32, (QUAD, QUAD), 0)
    bcol = lax.broadcasted_iota(jnp.int32, (QUAD, QUAD), 1)
    bd_mask = (brow // HEAD_DIM) == (bcol // HEAD_DIM)
    eye_bd = brow == bcol
    head_of_lane = lax.broadcasted_iota(jnp.int32, (CHUNK, QUAD), 1) // HEAD_DIM
    n_doublings = int(math.log2(CHUNK)) - 1

    for quad in range(RW_WIDTH // QUAD):
        sl = slice(quad * QUAD, (quad + 1) * QUAD)
        v_q = v[:, sl]
        kkt_q = kk_t[:, sl]
        rt_q = r_t[:, sl]
        kh_q = k_h[:, sl]
        bh_q = b_h[:, sl]
        heads = range(QUAD // HEAD_DIM)
        rhs = jnp.concatenate([jnp.where(head_of_lane == h, kh_q, 0.0) for h in heads]
                              + [jnp.where(head_of_lane == h, bh_q, 0.0) for h in heads], axis=0).astype(BF16)
        lhs = jnp.concatenate([kkt_q, rt_q], axis=0).astype(BF16)
        gram = _dot_nt(lhs, rhs)
        a_k = jnp.where(strict, gram[0:CHUNK, 0:QUAD], 0.0)
        a_b = jnp.where(strict, gram[0:CHUNK, QUAD:2 * QUAD], 0.0)
        b_k = jnp.where(incl, gram[CHUNK:2 * CHUNK, 0:QUAD], 0.0)
        b_b = jnp.where(incl, gram[CHUNK:2 * CHUNK, QUAD:2 * QUAD], 0.0)

        m = -a_b
        t_inv = eye_cat + m
        for _ in range(n_doublings):
            m = _dot(m.astype(BF16), _block_diag(m, bd_mask))
            t_inv = _dot(t_inv.astype(BF16), _block_diag(eye_cat + m, bd_mask))
        t_b = t_inv.astype(BF16)
        v_bd = _block_diag(v_q, bd_mask)
        akv = _dot(a_k.astype(BF16), v_bd)
        p1 = _dot(t_b, _block_diag(akv, bd_mask))
        tkk = _dot(t_b, _block_diag(kkt_q, bd_mask))
        bb_b = b_b.astype(BF16)
        y0 = _dot(b_k.astype(BF16), v_bd) - _dot(bb_b, _block_diag(p1, bd_mask))
        q_t = rt_q - _dot(bb_b, _block_diag(tkk, bd_mask))

        lhs_t = jnp.concatenate([k_e[:, sl], b_e[:, sl]], axis=0).astype(BF16)
        rhs_t = jnp.concatenate([jnp.concatenate([v_q, jnp.zeros_like(v_q)], axis=1),
                                 jnp.concatenate([-p1, -tkk], axis=1)], axis=0).astype(BF16)
        hg = _dot_tn(lhs_t, rhs_t)
        h0_cat = _fold(jnp.where(bd_mask, hg[:, 0:QUAD], 0.0))
        g_bd = jnp.where(bd_mask, hg[:, QUAD:2 * QUAD], 0.0) + jnp.where(eye_bd, e_tot[:, sl], 0.0)
        g_cat = _fold(g_bd)

        h = h_ref[quad]
        o2 = _dot(jnp.concatenate([g_cat, q_t], axis=0).astype(BF16), _block_diag(h, bd_mask))
        h_ref[quad] = o2[0:CHUNK] + h0_cat
        y_ref[:, sl] = y0 + o2[CHUNK:2 * CHUNK]


def _rwkv_call(rw_all, weights, *, reverse, with_gate, batch, seq, ctx_len):
    assert CHUNK == HEAD_DIM
    n_ctx = ctx_len // CHUNK
    n_lat = seq // CHUNK
    n_chunks = n_ctx + n_lat
    halo_per_chunk = CHUNK // HALO
    n_halo = batch * n_chunks * halo_per_chunk

    def chunk_of(s):
        if reverse:
            return jnp.where(s < n_ctx, n_ctx - 1 - s, n_chunks + n_ctx - 1 - s)
        return s

    def main_map(b, s):
        return (b * n_chunks + chunk_of(s), 0)

    def prev_map(b, s):
        return (jnp.maximum((b * n_chunks + chunk_of(s)) * halo_per_chunk - 1, 0), 0)

    def next_map(b, s):
        return (jnp.minimum((b * n_chunks + chunk_of(s) + 1) * halo_per_chunk, n_halo - 1), 0)

    def out_map(b, s):
        lat = jnp.maximum(s, n_ctx) - n_ctx
        return (b * n_lat + (n_lat - 1 - lat if reverse else lat), 0)

    def const_spec(a):
        return pl.BlockSpec(a.shape, lambda b, s: (0,) * a.ndim)

    out_spec = pl.BlockSpec((CHUNK, RW_WIDTH), out_map)
    out_shape = jax.ShapeDtypeStruct((batch * seq, RW_WIDTH), F32)
    n_out = 3 if with_gate else 2
    return pl.pallas_call(
        functools.partial(_rwkv_kernel, reverse, with_gate, n_ctx, n_chunks),
        grid=(batch, n_chunks),
        in_specs=[pl.BlockSpec((CHUNK, RW_COLS), main_map),
                  pl.BlockSpec((HALO, RW_COLS), prev_map),
                  pl.BlockSpec((HALO, RW_COLS), next_map)] + [const_spec(a) for a in weights],
        out_specs=[out_spec] * n_out,
        out_shape=[out_shape] * n_out,
        scratch_shapes=[pltpu.VMEM((RW_WIDTH // QUAD, CHUNK, QUAD), F32)],
        compiler_params=_cparams("parallel", "arbitrary"),
        name="rwkv_bwd" if reverse else "rwkv_fwd",
    )(rw_all, rw_all, rw_all, *weights)


def _outproj_kernel(d_model, alpha, yna_ref, y0_ref, y1_ref, b0_ref, b1_ref, gate_ref, x_ref, mod_ref,
                    gng_ref, gnb_ref, ones_ref, w_ref, lng_ref, lnb_ref, o_ref):
    ones = ones_ref[...]
    inv_n = 1.0 / HEAD_DIM
    y = y0_ref[...] + y1_ref[...]
    mu = _dot_lhs_split(y, ones, 2) * inv_n
    yc = y - mu
    var = _dot_lhs_split(yc * yc, ones, 2) * inv_n
    yn = yc * lax.rsqrt(var + GN_EPS) * gng_ref[...] + gnb_ref[...]
    y_rw = ((yn + (b0_ref[...] + b1_ref[...])) * gate_ref[...]).astype(BF16)
    proj = _dot(yna_ref[...], w_ref[0:NA_WIDTH, :]) + _dot(y_rw, w_ref[NA_WIDTH:NA_WIDTH + RW_WIDTH, :])
    g1 = mod_ref[...][:, 2 * d_model:3 * d_model]
    o_ref[...] = _normalize(alpha * x_ref[...] + g1 * proj, LN_EPS) * lng_ref[...] + lnb_ref[...]


def _outproj_call(y_na, y0, y1, b0, b1, gate, x2d, mod3, gn_g, gn_b, ones, w_out_bf16, ln_g, ln_b, *, tm, seq, alpha):
    rows, d = x2d.shape
    tiles_per_batch = seq // tm
    half = pl.BlockSpec((tm, RW_WIDTH), lambda i: (i, 0))
    full = pl.BlockSpec((tm, d), lambda i: (i, 0))

    def const_spec(a):
        return pl.BlockSpec(a.shape, lambda i: (0,) * a.ndim)

    return pl.pallas_call(
        functools.partial(_outproj_kernel, d, alpha),
        grid=(rows // tm,),
        in_specs=[half] * 6 + [full, pl.BlockSpec((None, 1, mod3.shape[2]), lambda i: (i // tiles_per_batch, 0, 0))]
                 + [const_spec(a) for a in (gn_g, gn_b, ones, w_out_bf16, ln_g, ln_b)],
        out_specs=full,
        out_shape=jax.ShapeDtypeStruct((rows, d), F32),
        compiler_params=_cparams("parallel"),
        name="outproj",
    )(y_na, y0, y1, b0, b1, gate, x2d, mod3, gn_g, gn_b, ones, w_out_bf16, ln_g, ln_b)


def _mlp_kernel(d_model, alpha, ff_chunk, x_ref, mod_ref, w1_ref, w2_ref, lng_ref, lnb_ref, o_ref):
    x = x_ref[...]
    mod = mod_ref[...]
    shift = mod[:, 3 * d_model:4 * d_model]
    scale = mod[:, 4 * d_model:5 * d_model]
    g2 = mod[:, 5 * d_model:6 * d_model]
    xm = (_normalize(x, LN_EPS) * (1.0 + scale) + shift).astype(BF16)
    acc = None
    for c0 in range(0, w1_ref.shape[1], ff_chunk):
        hid = jnp.maximum(_dot(xm, w1_ref[:, c0:c0 + ff_chunk]), 0.0)
        t = _dot((hid * hid).astype(BF16), w2_ref[c0:c0 + ff_chunk, :])
        acc = t if acc is None else acc + t
    o_ref[...] = _normalize(alpha * x + g2 * acc, LN_EPS) * lng_ref[...] + lnb_ref[...]


def _mlp_call(x2d, mod3, w1_bf16, w2_bf16, ln_g, ln_b, *, tm, seq, alpha):
    rows, d = x2d.shape
    tiles_per_batch = seq // tm
    full = pl.BlockSpec((tm, d), lambda i: (i, 0))

    def resident(a):
        return pl.BlockSpec(a.shape, lambda i: (0,) * a.ndim, pipeline_mode=pl.Buffered(1))

    return pl.pallas_call(
        functools.partial(_mlp_kernel, d, alpha, 1024),
        grid=(rows // tm,),
        in_specs=[full, pl.BlockSpec((None, 1, mod3.shape[2]), lambda i: (i // tiles_per_batch, 0, 0)),
                  resident(w1_bf16), resident(w2_bf16), resident(ln_g), resident(ln_b)],
        out_specs=full,
        out_shape=jax.ShapeDtypeStruct((rows, d), F32),
        compiler_params=_cparams("parallel"),
        name="mlp",
    )(x2d, mod3, w1_bf16, w2_bf16, ln_g, ln_b)


def _rope_tables(seq):
    f = HEAD_DIM // 4
    t = jnp.arange(seq)
    row = (t // GRID_W).astype(F32)
    col = (t % GRID_W).astype(F32)
    inv = ROPE_BASE ** (-jnp.arange(f, dtype=F32) / f)
    ang_r = row[:, None] * inv[None, :]
    ang_c = col[:, None] * inv[None, :]
    zero = jnp.zeros_like(ang_r)
    cos = jnp.concatenate([jnp.cos(ang_r), jnp.cos(ang_r), jnp.cos(ang_c), jnp.cos(ang_c)], axis=1)
    sa = jnp.concatenate([-jnp.sin(ang_r), zero, -jnp.sin(ang_c), zero], axis=1)
    sb = jnp.concatenate([zero, jnp.sin(ang_r), zero, jnp.sin(ang_c)], axis=1)
    return tuple(jnp.concatenate([z, z], axis=1) for z in (cos, sa, sb))


def _bias_table(rpb):
    qc = jnp.arange(GRID_W)[:, None]
    kc = jnp.arange(GRID_W)[None, :]
    c0 = jnp.clip(qc - WIN_W // 2, 0, GRID_W - WIN_W)
    inside = (kc >= c0) & (kc < c0 + WIN_W)
    off = jnp.clip(kc - qc + (WIN_W - 1), 0, 2 * WIN_W - 2)
    g = jnp.where(inside[None, None], rpb[:, :, off], NEG_BIAS)
    g = g.transpose(1, 0, 2, 3)
    return jnp.concatenate([g[:-1], g[1:]], axis=-1).astype(F32)


def _pad_lora(w, d):
    z = jnp.zeros_like(w[d])
    return jnp.concatenate([w[0] if d == 0 else z, w[1] if d == 1 else z], axis=0)


def kernel(x, c, ctx, c_ctx, w_mod, b_mod, w_in, w_out, ln1_g, ln1_b, mlp_w1, mlp_w2, ln2_g, ln2_b, na_rpb,
           rw_mu_prev, rw_mu_next, rw_w0, rw_w2, rw_a0, rw_a2, rw_g2, rw_k_k, rw_k_a, rw_r_k, rw_gn_g, rw_gn_b):
    depth = w_mod.shape[0]
    assert depth == 1, "single-layer trunk only (the context stream is never updated)"
    batch, seq, d = x.shape
    ctx_len = ctx.shape[1]
    alpha = (2 * depth) ** 0.25
    tm = 256 if ctx_len % 256 == 0 else 128
    assert seq % (Q_ROWS * GRID_W) == 0 and ctx_len % tm == 0 and ctx_len % CHUNK == 0
    assert batch + 1 <= 8

    cc = jnp.zeros((8, d), F32).at[:batch].set(c).at[batch].set(c_ctx)
    mod3 = _mod_call(cc, w_mod[0], b_mod[0]).reshape(8, 1, N_MOD * d)

    w_in_b = w_in[0].astype(BF16)
    tiles_lat = seq // tm
    tiles_ctx = ctx_len // tm
    tiles_all = tiles_lat + tiles_ctx
    rw_rows = batch * (seq + ctx_len)
    x2d = x.reshape(batch * seq, d)
    q, qr, kr, v, rw_all = _inproj_call(
        x2d, mod3, _rope_tables(seq), w_in_b, tm=tm,
        mod_index=lambda i: i // tiles_lat,
        table_index=lambda i: i % tiles_lat,
        rw_index=lambda i: (i // tiles_lat) * tiles_all + tiles_ctx + i % tiles_lat,
        rw_rows=rw_rows)
    ident = (jnp.ones((tm, 2 * HEAD_DIM), F32), jnp.zeros((tm, 2 * HEAD_DIM), F32), jnp.zeros((tm, 2 * HEAD_DIM), F32))
    _, _, kc, vc, rw_all = _inproj_call(
        ctx.reshape(batch * ctx_len, d), mod3, ident, w_in_b, tm=tm,
        mod_index=lambda i: batch,
        table_index=lambda i: 0,
        rw_index=lambda i: (i // tiles_ctx) * tiles_all + i % tiles_ctx,
        rw_rows=rw_rows, rw_prev=rw_all)

    y_na = _na_call(q, qr, kr, v, kc, vc, _bias_table(na_rpb[0]), batch=batch, seq=seq, ctx_len=ctx_len)

    lane_head = jnp.arange(RW_WIDTH) // HEAD_DIM
    ones = (lane_head[:, None] == lane_head[None, :]).astype(BF16)
    row = lambda a: a.reshape(1, -1).astype(F32)
    outs = []
    for dirn in range(N_DIR):
        weights = (row(rw_mu_prev[0]), row(rw_mu_next[0]), row(rw_w0[0, dirn]), _pad_lora(rw_w2[0], dirn),
                   row(rw_a0[0, dirn]), _pad_lora(rw_a2[0], dirn), rw_g2[0], row(rw_k_k[0]), row(rw_k_a[0]),
                   row(rw_r_k[0]), ones)
        outs.append(_rwkv_call(rw_all, weights, reverse=dirn == 1, with_gate=dirn == 0,
                               batch=batch, seq=seq, ctx_len=ctx_len))
    (y0, b0, gate), (y1, b1) = outs

    x1 = _outproj_call(y_na, y0, y1, b0, b1, gate, x2d, mod3, row(rw_gn_g[0]), row(rw_gn_b[0]), ones,
                       w_out[0].astype(BF16), row(ln1_g[0]), row(ln1_b[0]), tm=tm, seq=seq, alpha=alpha)
    out = _mlp_call(x1, mod3, mlp_w1[0].astype(BF16), mlp_w2[0].astype(BF16), row(ln2_g[0]), row(ln2_b[0]),
                    tm=tm, seq=seq, alpha=alpha)
    return out.reshape(batch, seq, d)
```

```python
import functools
import math

import jax
import jax.numpy as jnp
import numpy as np
from jax import lax
from jax.experimental import pallas as pl
from jax.experimental.pallas import tpu as pltpu

HEAD_DIM = 64
NA_HEADS = 8
RW_HEADS = 8
NA_WIDTH = NA_HEADS * HEAD_DIM
RW_WIDTH = RW_HEADS * HEAD_DIM
GRID_W = 64
WIN_H = 8
WIN_W = 16
ROPE_BASE = 10000.0
N_DIR = 2
DECAY_LORA = 64
AAA_LORA = 64
GATE_LORA = 128
LORA_W = N_DIR * DECAY_LORA
NA_COLS = 3 * NA_WIDTH
RW_COLS = 3 * RW_WIDTH + 3 * LORA_W
N_MOD = 6
LN_EPS = 1e-6
GN_EPS = 64e-5
NEG_BIAS = -1e30

CHUNK = 64
RW_BLOCK_CHUNKS = 4
QUAD = 4 * HEAD_DIM
HALO = 8
Q_ROWS = 8
KV_GROUP = 4
VMEM_LIMIT = 56 * 1024 * 1024

F32 = jnp.float32
BF16 = jnp.bfloat16


def _cparams(*sem):
    return pltpu.CompilerParams(dimension_semantics=sem, vmem_limit_bytes=VMEM_LIMIT)


def _dot(a, b):
    return jnp.dot(a, b, preferred_element_type=F32)


def _dot_nt(a, b):
    return lax.dot_general(a, b, (((1,), (1,)), ((), ())), preferred_element_type=F32)


def _dot_tn(a, b):
    return lax.dot_general(a, b, (((0,), (0,)), ((), ())), preferred_element_type=F32)


def _split(x, pieces):
    out = []
    for _ in range(pieces):
        p = x.astype(BF16)
        out.append(p)
        x = x - p.astype(F32)
    return out


def _dot_lhs_split(x, w_bf16, pieces):
    acc = None
    for p in _split(x, pieces):
        t = _dot(p, w_bf16)
        acc = t if acc is None else acc + t
    return acc


def _dot_x3(a, b):
    a_hi, a_lo = _split(a, 2)
    b_hi, b_lo = _split(b, 2)
    return _dot(a_hi, b_hi) + (_dot(a_lo, b_hi) + _dot(a_hi, b_lo))


def _normalize(x, eps):
    mu = jnp.mean(x, axis=-1, keepdims=True)
    xc = x - mu
    var = jnp.mean(xc * xc, axis=-1, keepdims=True)
    return xc * lax.rsqrt(var + eps)


def _mod_kernel(c_ref, w_ref, b_ref, o_ref):
    c = c_ref[...]
    s = c * jax.nn.sigmoid(c)
    o_ref[...] = _dot_x3(s, w_ref[...]) + b_ref[...]


def _mod_call(cc, w_mod, b_mod):
    rows, d = cc.shape
    n = w_mod.shape[1]
    tn = 512
    return pl.pallas_call(
        _mod_kernel,
        grid=(n // tn,),
        in_specs=[pl.BlockSpec((rows, d), lambda j: (0, 0)),
                  pl.BlockSpec((d, tn), lambda j: (0, j)),
                  pl.BlockSpec((1, tn), lambda j: (0, j))],
        out_specs=pl.BlockSpec((rows, tn), lambda j: (0, j)),
        out_shape=jax.ShapeDtypeStruct((rows, n), F32),
        compiler_params=_cparams("arbitrary"),
        name="mod",
    )(cc, w_mod, b_mod.reshape(1, n))


def _inproj_kernel(d_model, x_ref, mod_ref, cos_ref, sa_ref, sb_ref, w_ref, *refs):
    q_ref, qr_ref, kr_ref, v_ref, rw_ref = refs[-5:]
    x = x_ref[...]
    mod = mod_ref[...]
    shift = mod[:, 0:d_model]
    scale = mod[:, d_model:2 * d_model]
    xm = (_normalize(x, LN_EPS) * (1.0 + scale) + shift).astype(BF16)

    reps = NA_WIDTH // cos_ref.shape[1]
    cos = jnp.concatenate([cos_ref[...]] * reps, axis=1)
    sa = jnp.concatenate([sa_ref[...]] * reps, axis=1)
    sb = jnp.concatenate([sb_ref[...]] * reps, axis=1)

    def rope(z):
        up = pltpu.roll(z, NA_WIDTH - HEAD_DIM // 4, 1)
        down = pltpu.roll(z, HEAD_DIM // 4, 1)
        return z * cos + up * sa + down * sb

    qk_scale = HEAD_DIM ** -0.5
    q = _dot(xm, w_ref[:, 0:NA_WIDTH]) * qk_scale
    q_ref[...] = q.astype(BF16)
    qr_ref[...] = rope(q).astype(BF16)
    k = _dot(xm, w_ref[:, NA_WIDTH:2 * NA_WIDTH])
    kr_ref[...] = rope(k).astype(BF16)
    v_ref[...] = _dot(xm, w_ref[:, 2 * NA_WIDTH:NA_COLS]).astype(BF16)
    rw_ref[...] = _dot(xm, w_ref[:, NA_COLS:NA_COLS + RW_COLS])


def _inproj_call(x2d, mod3, tables, w_in_bf16, *, tm, mod_index, table_index, rw_index, rw_rows, rw_prev=None):
    rows, d = x2d.shape
    cos, sa, sb = tables
    tw = cos.shape[1]
    ncols = w_in_bf16.shape[1]
    na_spec = pl.BlockSpec((tm, NA_WIDTH), lambda i: (i, 0))
    in_specs = [pl.BlockSpec((tm, d), lambda i: (i, 0)),
                pl.BlockSpec((None, 1, mod3.shape[2]), lambda i: (mod_index(i), 0, 0)),
                pl.BlockSpec((tm, tw), lambda i: (table_index(i), 0)),
                pl.BlockSpec((tm, tw), lambda i: (table_index(i), 0)),
                pl.BlockSpec((tm, tw), lambda i: (table_index(i), 0)),
                pl.BlockSpec((d, ncols), lambda i: (0, 0))]
    args = [x2d, mod3, cos, sa, sb, w_in_bf16]
    aliases = {}
    if rw_prev is not None:
        in_specs.append(pl.BlockSpec(memory_space=pl.ANY))
        args.append(rw_prev)
        aliases = {len(args) - 1: 4}
    na_shape = jax.ShapeDtypeStruct((rows, NA_WIDTH), BF16)
    return pl.pallas_call(
        functools.partial(_inproj_kernel, d),
        grid=(rows // tm,),
        in_specs=in_specs,
        out_specs=[na_spec, na_spec, na_spec, na_spec,
                   pl.BlockSpec((tm, RW_COLS), lambda i: (rw_index(i), 0))],
        out_shape=[na_shape, na_shape, na_shape, na_shape,
                   jax.ShapeDtypeStruct((rw_rows, RW_COLS), F32)],
        input_output_aliases=aliases,
        compiler_params=_cparams("parallel"),
        name="inproj_ctx" if rw_prev is not None else "inproj",
    )(*args)


def _na_kernel(grid_rows, q_ref, qr_ref, k0, k1, k2, k3, v0, v1, v2, v3, kc_ref, vc_ref, bias_ref,
               o_ref, kwin, vwin):
    i = pl.program_id(1)
    grp = KV_GROUP * GRID_W
    for g, (kr, vr) in enumerate(((k0, v0), (k1, v1), (k2, v2), (k3, v3))):
        kwin[g * grp:(g + 1) * grp, :] = kr[...]
        vwin[g * grp:(g + 1) * grp, :] = vr[...]
    n_groups = grid_rows // KV_GROUP
    win_row0 = KV_GROUP * jnp.clip(2 * i - 1, 0, n_groups - 4)
    win_keys = WIN_H * GRID_W
    lane = lax.broadcasted_iota(jnp.int32, (GRID_W, 2 * HEAD_DIM), 1)
    even = lane < HEAD_DIM

    def row_body(j, carry):
        irow = Q_ROWS * i + j
        r0 = jnp.clip(irow - WIN_H // 2, 0, grid_rows - WIN_H)
        koff = pl.multiple_of((r0 - win_row0) * GRID_W, GRID_W)
        brow = (WIN_H - 1) - (irow - r0)
        qoff = pl.multiple_of(j * GRID_W, GRID_W)
        pairs = range(NA_HEADS // 2)
        lanes = [slice(pair * 2 * HEAD_DIM, (pair + 1) * 2 * HEAD_DIM) for pair in pairs]
        s_loc, s_ctx = [], []
        for pair in pairs:
            qr = qr_ref[pl.ds(qoff, GRID_W), lanes[pair]]
            qp = q_ref[pl.ds(qoff, GRID_W), lanes[pair]]
            zero = jnp.zeros_like(qr)
            qr2 = jnp.concatenate([jnp.where(even, qr, zero), jnp.where(even, zero, qr)], axis=0)
            qp2 = jnp.concatenate([jnp.where(even, qp, zero), jnp.where(even, zero, qp)], axis=0)
            bias = jnp.concatenate(
                [jnp.concatenate([bias_ref[brow + 2 * m, 2 * pair], bias_ref[brow + 2 * m, 2 * pair + 1]], axis=0)
                 for m in range(WIN_H // 2)], axis=1)
            s_loc.append(_dot_nt(qr2, kwin[pl.ds(koff, win_keys), lanes[pair]]) + bias)
            s_ctx.append(_dot_nt(qp2, kc_ref[:, lanes[pair]]))
        p_loc, p_ctx, denom = [], [], []
        for pair in pairs:
            mx = jnp.maximum(jnp.max(s_loc[pair], axis=-1, keepdims=True),
                             jnp.max(s_ctx[pair], axis=-1, keepdims=True))
            pl_ = jnp.exp(s_loc[pair] - mx)
            pc_ = jnp.exp(s_ctx[pair] - mx)
            denom.append(jnp.sum(pl_, axis=-1, keepdims=True) + jnp.sum(pc_, axis=-1, keepdims=True))
            p_loc.append(pl_.astype(BF16))
            p_ctx.append(pc_.astype(BF16))
        for pair in pairs:
            o = (_dot(p_loc[pair], vwin[pl.ds(koff, win_keys), lanes[pair]])
                 + _dot(p_ctx[pair], vc_ref[:, lanes[pair]])) / denom[pair]
            o_ref[pl.ds(qoff, GRID_W), lanes[pair]] = jnp.where(
                even, o[0:GRID_W], o[GRID_W:2 * GRID_W]).astype(BF16)
        return carry

    lax.fori_loop(0, Q_ROWS, row_body, 0)


def _na_call(q, qr, kr, v, kc, vc, bias_tab, *, batch, seq, ctx_len):
    grid_rows = seq // GRID_W
    n_blocks = grid_rows // Q_ROWS
    n_groups = grid_rows // KV_GROUP
    blk = Q_ROWS * GRID_W
    grp = KV_GROUP * GRID_W

    def group_spec(g):
        return pl.BlockSpec((grp, NA_WIDTH),
                            lambda b, i: (b * n_groups + jnp.clip(2 * i - 1, 0, n_groups - 4) + g, 0))

    q_spec = pl.BlockSpec((blk, NA_WIDTH), lambda b, i: (b * n_blocks + i, 0))
    c_spec = pl.BlockSpec((ctx_len, NA_WIDTH), lambda b, i: (b, 0))
    return pl.pallas_call(
        functools.partial(_na_kernel, grid_rows),
        grid=(batch, n_blocks),
        in_specs=[q_spec, q_spec] + [group_spec(g) for g in range(4)] + [group_spec(g) for g in range(4)]
                 + [c_spec, c_spec, pl.BlockSpec(bias_tab.shape, lambda b, i: (0, 0, 0, 0))],
        out_specs=q_spec,
        out_shape=jax.ShapeDtypeStruct((batch * seq, NA_WIDTH), BF16),
        scratch_shapes=[pltpu.VMEM((4 * grp, NA_WIDTH), BF16), pltpu.VMEM((4 * grp, NA_WIDTH), BF16)],
        compiler_params=_cparams("parallel", "arbitrary"),
        name="natten",
    )(q, qr, kr, kr, kr, kr, v, v, v, v, kc, vc, bias_tab)


def _block_diag(x_cat, bd_mask):
    xb = x_cat.astype(BF16)
    tiled = jnp.concatenate([xb] * (QUAD // CHUNK), axis=0)
    return jnp.where(bd_mask, tiled, jnp.zeros_like(tiled))


def _diag_blocks(x, head_of_lane):
    n = QUAD // HEAD_DIM
    out = x[(n - 1) * HEAD_DIM:n * HEAD_DIM]
    for h in range(n - 2, -1, -1):
        out = jnp.where(head_of_lane == h, x[h * HEAD_DIM:(h + 1) * HEAD_DIM], out)
    return out


def _rwkv_kernel(reverse, with_gate, n_ctx, n_blocks,
                 main_ref, prev_ref, next_ref, mup_ref, mun_ref, w0_ref, w2_ref, a0_ref, a2_ref, g2_ref,
                 kk_ref, ka_ref, rk_ref, ones_ref, *refs):
    if with_gate:
        y_ref, bonus_ref, gate_ref, h_ref = refs
    else:
        y_ref, bonus_ref, h_ref = refs
    s = pl.program_id(1)
    if reverse:
        n = jnp.where(s < n_ctx, n_ctx - 1 - s, n_blocks + n_ctx - 1 - s)
    else:
        n = s

    @pl.when(s == 0)
    def _():
        h_ref[...] = jnp.zeros_like(h_ref)

    p = main_ref[...]
    rows = p.shape[0]
    row = lax.broadcasted_iota(jnp.int32, (rows, 1), 0)
    has_prev = jnp.logical_and(n != 0, n != n_ctx)
    has_next = jnp.logical_and(n != n_ctx - 1, n != n_blocks - 1)
    prow = jnp.where(has_prev, prev_ref[HALO - 1:HALO, :], 0.0)
    nrow = jnp.where(has_next, next_ref[0:1, :], 0.0)
    prev = jnp.where(row == 0, prow, pltpu.roll(p, 1, 0))
    nxt = jnp.where(row == rows - 1, nrow, pltpu.roll(p, rows - 1, 0))
    xs = p + mup_ref[...] * (prev - p) + mun_ref[...] * (nxt - p)

    w = RW_WIDTH
    r = xs[:, 0:w]
    k = xs[:, w:2 * w]
    v = xs[:, 2 * w:3 * w]
    pw = xs[:, 3 * w:3 * w + LORA_W]
    pa = xs[:, 3 * w + LORA_W:3 * w + 2 * LORA_W]
    ones = ones_ref[...]

    z = -(w0_ref[...] + _dot_x3(jnp.tanh(pw), w2_ref[...]))
    softplus = jnp.maximum(z, 0.0) + jnp.log1p(jnp.exp(-jnp.abs(z)))
    log_decay = -jnp.exp(-softplus - 0.5)
    a = jax.nn.sigmoid(a0_ref[...] + _dot_x3(pa, a2_ref[...]))
    kk_raw = k * kk_ref[...]
    kk = kk_raw / jnp.maximum(jnp.sqrt(_dot_lhs_split(kk_raw * kk_raw, ones, 2)), 1e-12)
    kd = k * (1.0 + (a - 1.0) * ka_ref[...])
    bb = a * kk
    bonus_ref[...] = _dot_lhs_split(r * kd * rk_ref[...], ones, 2) * v
    if with_gate:
        pg = xs[:, 3 * w + 2 * LORA_W:3 * w + 3 * LORA_W]
        gate_ref[...] = _dot_x3(jax.nn.sigmoid(pg), g2_ref[...])

    ti = lax.broadcasted_iota(jnp.int32, (rows, rows), 0)
    si = lax.broadcasted_iota(jnp.int32, (rows, rows), 1)
    same_chunk = (ti // CHUNK) == (si // CHUNK)
    tri = jnp.where(jnp.logical_and(same_chunk, (si >= ti) if reverse else (si <= ti)), 1.0, 0.0).astype(BF16)
    lw_parts = _split(log_decay, 3)
    cum = _dot(tri, lw_parts[0]) + (_dot(tri, lw_parts[1]) + _dot(tri, lw_parts[2]))
    e_neg = jnp.exp(-cum)
    r_t = r * jnp.exp(cum)
    kk_t = kk * jnp.exp(cum - log_decay)
    k_h = kd * e_neg
    b_h = bb * e_neg

    trow = lax.broadcasted_iota(jnp.int32, (CHUNK, QUAD), 0)
    tcol = lax.broadcasted_iota(jnp.int32, (CHUNK, QUAD), 1) % CHUNK
    strict = (trow < tcol) if reverse else (trow > tcol)
    incl = (trow <= tcol) if reverse else (trow >= tcol)
    on_diag = trow == tcol
    eye_cat = jnp.where(on_diag, 1.0, 0.0)
    brow = lax.broadcasted_iota(jnp.int32, (QUAD, QUAD), 0)
    bcol = lax.broadcasted_iota(jnp.int32, (QUAD, QUAD), 1)
    bd_mask = (brow // HEAD_DIM) == (bcol // HEAD_DIM)
    head_of_lane = lax.broadcasted_iota(jnp.int32, (CHUNK, QUAD), 1) // HEAD_DIM
    n_doublings = int(math.log2(CHUNK)) - 1
    heads = range(QUAD // HEAD_DIM)
    quads = range(RW_WIDTH // QUAD)
    n_ch = rows // CHUNK
    order = list(range(n_ch - 1, -1, -1) if reverse else range(n_ch))
    last = 0 if reverse else CHUNK - 1

    chains = [(c, qd) for c in order for qd in quads]

    def rsl(c):
        return slice(c * CHUNK, (c + 1) * CHUNK)

    def lsl(qd):
        return slice(qd * QUAD, (qd + 1) * QUAD)

    e_tot = {c: jnp.exp(cum[c * CHUNK + last:c * CHUNK + last + 1, :]) for c in order}
    v_q = {ch: v[rsl(ch[0]), lsl(ch[1])] for ch in chains}
    kkt_q = {ch: kk_t[rsl(ch[0]), lsl(ch[1])] for ch in chains}
    rt_q = {ch: r_t[rsl(ch[0]), lsl(ch[1])] for ch in chains}

    a_k, a_b, b_k, b_b = {}, {}, {}, {}
    for ch in chains:
        kh = k_h[rsl(ch[0]), lsl(ch[1])].astype(BF16)
        bh = b_h[rsl(ch[0]), lsl(ch[1])].astype(BF16)
        zero = jnp.zeros_like(kh)
        rhs = jnp.concatenate([jnp.where(head_of_lane == h, kh, zero) for h in heads]
                              + [jnp.where(head_of_lane == h, bh, zero) for h in heads], axis=0)
        lhs = jnp.concatenate([kkt_q[ch], rt_q[ch]], axis=0).astype(BF16)
        gram = _dot_nt(lhs, rhs)
        a_k[ch] = jnp.where(strict, gram[0:CHUNK, 0:QUAD], 0.0).astype(BF16)
        a_b[ch] = jnp.where(strict, gram[0:CHUNK, QUAD:2 * QUAD], 0.0)
        b_k[ch] = jnp.where(incl, gram[CHUNK:2 * CHUNK, 0:QUAD], 0.0).astype(BF16)
        b_b[ch] = jnp.where(incl, gram[CHUNK:2 * CHUNK, QUAD:2 * QUAD], 0.0).astype(BF16)

    v_bd = {ch: _block_diag(v_q[ch], bd_mask) for ch in chains}
    akv = {ch: _dot(a_k[ch], v_bd[ch]) for ch in chains}

    m = {ch: -a_b[ch] for ch in chains}
    t_inv = {ch: eye_cat + m[ch] for ch in chains}
    for _ in range(n_doublings):
        for ch in chains:
            m[ch] = _dot(m[ch].astype(BF16), _block_diag(m[ch], bd_mask))
        for ch in chains:
            t_inv[ch] = _dot(t_inv[ch].astype(BF16), _block_diag(eye_cat + m[ch], bd_mask))

    p1, tkk = {}, {}
    for ch in chains:
        t_b = t_inv[ch].astype(BF16)
        p1[ch] = _dot(t_b, _block_diag(akv[ch], bd_mask))
        tkk[ch] = _dot(t_b, _block_diag(kkt_q[ch], bd_mask))

    y0, q_t, h0_cat, g_cat = {}, {}, {}, {}
    for ch in chains:
        c, qd = ch
        y0[ch] = _dot(b_k[ch], v_bd[ch]) - _dot(b_b[ch], _block_diag(p1[ch], bd_mask))
        q_t[ch] = rt_q[ch] - _dot(b_b[ch], _block_diag(tkk[ch], bd_mask))
        e_q = e_tot[c][:, lsl(qd)]
        lhs_t = jnp.concatenate([k_h[rsl(c), lsl(qd)] * e_q, b_h[rsl(c), lsl(qd)] * e_q], axis=0).astype(BF16)
        rhs_t = jnp.concatenate([jnp.concatenate([v_q[ch], jnp.zeros_like(v_q[ch])], axis=1),
                                 jnp.concatenate([-p1[ch], -tkk[ch]], axis=1)], axis=0).astype(BF16)
        hg = _dot_tn(lhs_t, rhs_t)
        h0_cat[ch] = _diag_blocks(hg[:, 0:QUAD], head_of_lane)
        g_cat[ch] = _diag_blocks(hg[:, QUAD:2 * QUAD], head_of_lane) + jnp.where(on_diag, e_q, 0.0)

    h = {qd: h_ref[qd] for qd in quads}
    for c in order:
        for qd in quads:
            ch = (c, qd)
            o2 = _dot(jnp.concatenate([g_cat[ch], q_t[ch]], axis=0).astype(BF16), _block_diag(h[qd], bd_mask))
            h[qd] = o2[0:CHUNK] + h0_cat[ch]
            y_ref[rsl(c), lsl(qd)] = y0[ch] + o2[CHUNK:2 * CHUNK]
    for qd in quads:
        h_ref[qd] = h[qd]


def _rwkv_call(rw_all, weights, *, reverse, with_gate, batch, seq, ctx_len):
    assert CHUNK == HEAD_DIM
    blk_rows = RW_BLOCK_CHUNKS * CHUNK
    assert ctx_len % blk_rows == 0 and seq % blk_rows == 0
    n_ctx = ctx_len // blk_rows
    n_lat = seq // blk_rows
    n_chunks = n_ctx + n_lat
    halo_per_chunk = blk_rows // HALO
    n_halo = batch * n_chunks * halo_per_chunk

    def chunk_of(s):
        if reverse:
            return jnp.where(s < n_ctx, n_ctx - 1 - s, n_chunks + n_ctx - 1 - s)
        return s

    def main_map(b, s):
        return (b * n_chunks + chunk_of(s), 0)

    def prev_map(b, s):
        return (jnp.maximum((b * n_chunks + chunk_of(s)) * halo_per_chunk - 1, 0), 0)

    def next_map(b, s):
        return (jnp.minimum((b * n_chunks + chunk_of(s) + 1) * halo_per_chunk, n_halo - 1), 0)

    def out_map(b, s):
        lat = jnp.maximum(s, n_ctx) - n_ctx
        return (b * n_lat + (n_lat - 1 - lat if reverse else lat), 0)

    def const_spec(a):
        return pl.BlockSpec(a.shape, lambda b, s: (0,) * a.ndim)

    out_spec = pl.BlockSpec((blk_rows, RW_WIDTH), out_map)
    out_shape = jax.ShapeDtypeStruct((batch * seq, RW_WIDTH), F32)
    n_out = 3 if with_gate else 2
    return pl.pallas_call(
        functools.partial(_rwkv_kernel, reverse, with_gate, n_ctx, n_chunks),
        grid=(batch, n_chunks),
        in_specs=[pl.BlockSpec((blk_rows, RW_COLS), main_map),
                  pl.BlockSpec((HALO, RW_COLS), prev_map),
                  pl.BlockSpec((HALO, RW_COLS), next_map)] + [const_spec(a) for a in weights],
        out_specs=[out_spec] * n_out,
        out_shape=[out_shape] * n_out,
        scratch_shapes=[pltpu.VMEM((RW_WIDTH // QUAD, CHUNK, QUAD), F32)],
        compiler_params=_cparams("parallel", "arbitrary"),
        name="rwkv_bwd" if reverse else "rwkv_fwd",
    )(rw_all, rw_all, rw_all, *weights)


def _outproj_kernel(d_model, alpha, yna_ref, y0_ref, y1_ref, b0_ref, b1_ref, gate_ref, x_ref, mod_ref,
                    gng_ref, gnb_ref, ones_ref, w_ref, lng_ref, lnb_ref, o_ref):
    ones = ones_ref[...]
    inv_n = 1.0 / HEAD_DIM
    y = y0_ref[...] + y1_ref[...]
    mu = _dot_lhs_split(y, ones, 2) * inv_n
    yc = y - mu
    var = _dot_lhs_split(yc * yc, ones, 2) * inv_n
    yn = yc * lax.rsqrt(var + GN_EPS) * gng_ref[...] + gnb_ref[...]
    y_rw = ((yn + (b0_ref[...] + b1_ref[...])) * gate_ref[...]).astype(BF16)
    proj = _dot(yna_ref[...], w_ref[0:NA_WIDTH, :]) + _dot(y_rw, w_ref[NA_WIDTH:NA_WIDTH + RW_WIDTH, :])
    g1 = mod_ref[...][:, 2 * d_model:3 * d_model]
    o_ref[...] = _normalize(alpha * x_ref[...] + g1 * proj, LN_EPS) * lng_ref[...] + lnb_ref[...]


def _outproj_call(y_na, y0, y1, b0, b1, gate, x2d, mod3, gn_g, gn_b, ones, w_out_bf16, ln_g, ln_b, *, tm, seq, alpha):
    rows, d = x2d.shape
    tiles_per_batch = seq // tm
    half = pl.BlockSpec((tm, RW_WIDTH), lambda i: (i, 0))
    full = pl.BlockSpec((tm, d), lambda i: (i, 0))

    def const_spec(a):
        return pl.BlockSpec(a.shape, lambda i: (0,) * a.ndim)

    return pl.pallas_call(
        functools.partial(_outproj_kernel, d, alpha),
        grid=(rows // tm,),
        in_specs=[half] * 6 + [full, pl.BlockSpec((None, 1, mod3.shape[2]), lambda i: (i // tiles_per_batch, 0, 0))]
                 + [const_spec(a) for a in (gn_g, gn_b, ones, w_out_bf16, ln_g, ln_b)],
        out_specs=full,
        out_shape=jax.ShapeDtypeStruct((rows, d), F32),
        compiler_params=_cparams("parallel"),
        name="outproj",
    )(y_na, y0, y1, b0, b1, gate, x2d, mod3, gn_g, gn_b, ones, w_out_bf16, ln_g, ln_b)


def _mlp_kernel(d_model, alpha, ff_chunk, x_ref, mod_ref, w1_ref, w2_ref, lng_ref, lnb_ref, o_ref):
    x = x_ref[...]
    mod = mod_ref[...]
    shift = mod[:, 3 * d_model:4 * d_model]
    scale = mod[:, 4 * d_model:5 * d_model]
    g2 = mod[:, 5 * d_model:6 * d_model]
    xm = (_normalize(x, LN_EPS) * (1.0 + scale) + shift).astype(BF16)
    acc = None
    for c0 in range(0, w1_ref.shape[1], ff_chunk):
        hid = jnp.maximum(_dot(xm, w1_ref[:, c0:c0 + ff_chunk]), 0.0)
        t = _dot((hid * hid).astype(BF16), w2_ref[c0:c0 + ff_chunk, :])
        acc = t if acc is None else acc + t
    o_ref[...] = _normalize(alpha * x + g2 * acc, LN_EPS) * lng_ref[...] + lnb_ref[...]


def _mlp_call(x2d, mod3, w1_bf16, w2_bf16, ln_g, ln_b, *, tm, seq, alpha):
    rows, d = x2d.shape
    tiles_per_batch = seq // tm
    full = pl.BlockSpec((tm, d), lambda i: (i, 0))

    def resident(a):
        return pl.BlockSpec(a.shape, lambda i: (0,) * a.ndim, pipeline_mode=pl.Buffered(1))

    return pl.pallas_call(
        functools.partial(_mlp_kernel, d, alpha, 1024),
        grid=(rows // tm,),
        in_specs=[full, pl.BlockSpec((None, 1, mod3.shape[2]), lambda i: (i // tiles_per_batch, 0, 0)),
                  resident(w1_bf16), resident(w2_bf16), resident(ln_g), resident(ln_b)],
        out_specs=full,
        out_shape=jax.ShapeDtypeStruct((rows, d), F32),
        compiler_params=_cparams("parallel"),
        name="mlp",
    )(x2d, mod3, w1_bf16, w2_bf16, ln_g, ln_b)


def _rope_tables(seq):
    f = HEAD_DIM // 4
    t = np.arange(seq)
    row = (t // GRID_W).astype(np.float32)
    col = (t % GRID_W).astype(np.float32)
    inv = (ROPE_BASE ** (-np.arange(f, dtype=np.float32) / f)).astype(np.float32)
    ang_r = row[:, None] * inv[None, :]
    ang_c = col[:, None] * inv[None, :]
    zero = np.zeros_like(ang_r)
    cos = np.concatenate([np.cos(ang_r), np.cos(ang_r), np.cos(ang_c), np.cos(ang_c)], axis=1)
    sa = np.concatenate([-np.sin(ang_r), zero, -np.sin(ang_c), zero], axis=1)
    sb = np.concatenate([zero, np.sin(ang_r), zero, np.sin(ang_c)], axis=1)
    return tuple(jnp.asarray(np.concatenate([z, z], axis=1), F32) for z in (cos, sa, sb))


def _bias_table(rpb):
    qc = jnp.arange(GRID_W)[:, None]
    kc = jnp.arange(GRID_W)[None, :]
    c0 = jnp.clip(qc - WIN_W // 2, 0, GRID_W - WIN_W)
    inside = (kc >= c0) & (kc < c0 + WIN_W)
    off = jnp.clip(kc - qc + (WIN_W - 1), 0, 2 * WIN_W - 2)
    g = jnp.where(inside[None, None], rpb[:, :, off], NEG_BIAS)
    g = g.transpose(1, 0, 2, 3)
    return jnp.concatenate([g[:-1], g[1:]], axis=-1).astype(F32)


def _pad_lora(w, d):
    z = jnp.zeros_like(w[d])
    return jnp.concatenate([w[0] if d == 0 else z, w[1] if d == 1 else z], axis=0)


def kernel(x, c, ctx, c_ctx, w_mod, b_mod, w_in, w_out, ln1_g, ln1_b, mlp_w1, mlp_w2, ln2_g, ln2_b, na_rpb,
           rw_mu_prev, rw_mu_next, rw_w0, rw_w2, rw_a0, rw_a2, rw_g2, rw_k_k, rw_k_a, rw_r_k, rw_gn_g, rw_gn_b):
    depth = w_mod.shape[0]
    assert depth == 1, "single-layer trunk only (the context stream is never updated)"
    batch, seq, d = x.shape
    ctx_len = ctx.shape[1]
    alpha = (2 * depth) ** 0.25
    tm = 256 if ctx_len % 256 == 0 else 128
    assert seq % (Q_ROWS * GRID_W) == 0 and ctx_len % tm == 0 and ctx_len % CHUNK == 0
    assert batch + 1 <= 8

    cc = jnp.zeros((8, d), F32).at[:batch].set(c).at[batch].set(c_ctx)
    mod3 = _mod_call(cc, w_mod[0], b_mod[0]).reshape(8, 1, N_MOD * d)

    w_in_b = w_in[0].astype(BF16)
    tiles_lat = seq // tm
    tiles_ctx = ctx_len // tm
    tiles_all = tiles_lat + tiles_ctx
    rw_rows = batch * (seq + ctx_len)
    x2d = x.reshape(batch * seq, d)
    q, qr, kr, v, rw_all = _inproj_call(
        x2d, mod3, _rope_tables(seq), w_in_b, tm=tm,
        mod_index=lambda i: i // tiles_lat,
        table_index=lambda i: i % tiles_lat,
        rw_index=lambda i: (i // tiles_lat) * tiles_all + tiles_ctx + i % tiles_lat,
        rw_rows=rw_rows)
    ident = (jnp.ones((tm, 2 * HEAD_DIM), F32), jnp.zeros((tm, 2 * HEAD_DIM), F32), jnp.zeros((tm, 2 * HEAD_DIM), F32))
    _, _, kc, vc, rw_all = _inproj_call(
        ctx.reshape(batch * ctx_len, d), mod3, ident, w_in_b, tm=tm,
        mod_index=lambda i: batch,
        table_index=lambda i: 0,
        rw_index=lambda i: (i // tiles_ctx) * tiles_all + i % tiles_ctx,
        rw_rows=rw_rows, rw_prev=rw_all)

    y_na = _na_call(q, qr, kr, v, kc, vc, _bias_table(na_rpb[0]), batch=batch, seq=seq, ctx_len=ctx_len)

    lane_head = jnp.arange(RW_WIDTH) // HEAD_DIM
    ones = (lane_head[:, None] == lane_head[None, :]).astype(BF16)
    row = lambda a: a.reshape(1, -1).astype(F32)
    outs = []
    for dirn in range(N_DIR):
        weights = (row(rw_mu_prev[0]), row(rw_mu_next[0]), row(rw_w0[0, dirn]), _pad_lora(rw_w2[0], dirn),
                   row(rw_a0[0, dirn]), _pad_lora(rw_a2[0], dirn), rw_g2[0], row(rw_k_k[0]), row(rw_k_a[0]),
                   row(rw_r_k[0]), ones)
        outs.append(_rwkv_call(rw_all, weights, reverse=dirn == 1, with_gate=dirn == 0,
                               batch=batch, seq=seq, ctx_len=ctx_len))
    (y0, b0, gate), (y1, b1) = outs

    x1 = _outproj_call(y_na, y0, y1, b0, b1, gate, x2d, mod3, row(rw_gn_g[0]), row(rw_gn_b[0]), ones,
                       w_out[0].astype(BF16), row(ln1_g[0]), row(ln1_b[0]), tm=tm, seq=seq, alpha=alpha)
    out = _mlp_call(x1, mod3, mlp_w1[0].astype(BF16), mlp_w2[0].astype(BF16), row(ln2_g[0]), row(ln2_b[0]),
                    tm=tm, seq=seq, alpha=alpha)
    return out.reshape(batch, seq, d)
```

```python
import functools
import math

import jax
import jax.numpy as jnp
import numpy as np
from jax import lax
from jax.experimental import pallas as pl
from jax.experimental.pallas import tpu as pltpu

HEAD_DIM = 64
NA_HEADS = 8
RW_HEADS = 8
NA_WIDTH = NA_HEADS * HEAD_DIM
RW_WIDTH = RW_HEADS * HEAD_DIM
GRID_W = 64
WIN_H = 8
WIN_W = 16
ROPE_BASE = 10000.0
N_DIR = 2
DECAY_LORA = 64
AAA_LORA = 64
GATE_LORA = 128
LORA_W = N_DIR * DECAY_LORA
NA_COLS = 3 * NA_WIDTH
RW_COLS = 3 * RW_WIDTH + 3 * LORA_W
N_MOD = 6
LN_EPS = 1e-6
GN_EPS = 64e-5
NEG_BIAS = -1e30

CHUNK = 64
RW_BLOCK_CHUNKS = 4
QUAD = 2 * HEAD_DIM
HALO = 8
Q_ROWS = 8
KV_GROUP = 4
ROW_TILE = 512
VMEM_LIMIT = 56 * 1024 * 1024

F32 = jnp.float32
BF16 = jnp.bfloat16


def _cparams(*sem):
    return pltpu.CompilerParams(dimension_semantics=sem, vmem_limit_bytes=VMEM_LIMIT)


def _dot(a, b):
    return jnp.dot(a, b, preferred_element_type=F32)


def _dot_nt(a, b):
    return lax.dot_general(a, b, (((1,), (1,)), ((), ())), preferred_element_type=F32)


def _dot_tn(a, b):
    return lax.dot_general(a, b, (((0,), (0,)), ((), ())), preferred_element_type=F32)


def _split(x, pieces):
    out = []
    for _ in range(pieces):
        p = x.astype(BF16)
        out.append(p)
        x = x - p.astype(F32)
    return out


def _dot_lhs_split(x, w_bf16, pieces):
    acc = None
    for p in _split(x, pieces):
        t = _dot(p, w_bf16)
        acc = t if acc is None else acc + t
    return acc


def _dot_x3(a, b):
    a_hi, a_lo = _split(a, 2)
    b_hi, b_lo = _split(b, 2)
    return _dot(a_hi, b_hi) + (_dot(a_lo, b_hi) + _dot(a_hi, b_lo))


def _normalize(x, eps):
    mu = jnp.mean(x, axis=-1, keepdims=True)
    xc = x - mu
    var = jnp.mean(xc * xc, axis=-1, keepdims=True)
    return xc * lax.rsqrt(var + eps)


def _mod_kernel(c_ref, w_ref, b_ref, o_ref):
    c = c_ref[...]
    s = c * jax.nn.sigmoid(c)
    o_ref[...] = _dot_x3(s, w_ref[...]) + b_ref[...]


def _mod_call(cc, w_mod, b_mod):
    rows, d = cc.shape
    n = w_mod.shape[1]
    tn = 512
    return pl.pallas_call(
        _mod_kernel,
        grid=(n // tn,),
        in_specs=[pl.BlockSpec((rows, d), lambda j: (0, 0)),
                  pl.BlockSpec((d, tn), lambda j: (0, j)),
                  pl.BlockSpec((1, tn), lambda j: (0, j))],
        out_specs=pl.BlockSpec((rows, tn), lambda j: (0, j)),
        out_shape=jax.ShapeDtypeStruct((rows, n), F32),
        compiler_params=_cparams("arbitrary"),
        name="mod",
    )(cc, w_mod, b_mod.reshape(1, n))


def _inproj_kernel(d_model, x_ref, mod_ref, cos_ref, sa_ref, sb_ref, w_ref, q_ref, qr_ref, kr_ref, v_ref, rw_ref):
    x = x_ref[...]
    mod = mod_ref[...]
    shift = mod[:, 0:d_model]
    scale = mod[:, d_model:2 * d_model]
    xm = (_normalize(x, LN_EPS) * (1.0 + scale) + shift).astype(BF16)

    reps = NA_WIDTH // cos_ref.shape[1]
    cos = jnp.concatenate([cos_ref[...]] * reps, axis=1)
    sa = jnp.concatenate([sa_ref[...]] * reps, axis=1)
    sb = jnp.concatenate([sb_ref[...]] * reps, axis=1)

    def rope(z):
        up = pltpu.roll(z, NA_WIDTH - HEAD_DIM // 4, 1)
        down = pltpu.roll(z, HEAD_DIM // 4, 1)
        return z * cos + up * sa + down * sb

    qk_scale = HEAD_DIM ** -0.5
    q = _dot(xm, w_ref[:, 0:NA_WIDTH]) * qk_scale
    q_ref[...] = q.astype(BF16)
    qr_ref[...] = rope(q).astype(BF16)
    k = _dot(xm, w_ref[:, NA_WIDTH:2 * NA_WIDTH])
    kr_ref[...] = rope(k).astype(BF16)
    v_ref[...] = _dot(xm, w_ref[:, 2 * NA_WIDTH:NA_COLS]).astype(BF16)
    rw_ref[...] = _dot(xm, w_ref[:, NA_COLS:NA_COLS + RW_COLS])


def _inproj_call(x2d, mod3, tables, w_in_bf16, *, tm, mod_index, table_index, name):
    rows, d = x2d.shape
    cos, sa, sb = tables
    tw = cos.shape[1]
    ncols = w_in_bf16.shape[1]
    na_spec = pl.BlockSpec((tm, NA_WIDTH), lambda i: (i, 0))
    table_spec = pl.BlockSpec((tm, tw), lambda i: (table_index(i), 0))
    na_shape = jax.ShapeDtypeStruct((rows, NA_WIDTH), BF16)
    return pl.pallas_call(
        functools.partial(_inproj_kernel, d),
        grid=(rows // tm,),
        in_specs=[pl.BlockSpec((tm, d), lambda i: (i, 0)),
                  pl.BlockSpec((None, 1, mod3.shape[2]), lambda i: (mod_index(i), 0, 0)),
                  table_spec, table_spec, table_spec,
                  pl.BlockSpec((d, ncols), lambda i: (0, 0))],
        out_specs=[na_spec, na_spec, na_spec, na_spec, pl.BlockSpec((tm, RW_COLS), lambda i: (i, 0))],
        out_shape=[na_shape, na_shape, na_shape, na_shape, jax.ShapeDtypeStruct((rows, RW_COLS), F32)],
        compiler_params=_cparams("parallel"),
        name=name,
    )(x2d, mod3, cos, sa, sb, w_in_bf16)


def _na_kernel(grid_rows, q_ref, qr_ref, k0, k1, k2, k3, v0, v1, v2, v3, kc_ref, vc_ref, bias_ref,
               o_ref, kwin, vwin):
    i = pl.program_id(1)
    grp = KV_GROUP * GRID_W
    for g, (kr, vr) in enumerate(((k0, v0), (k1, v1), (k2, v2), (k3, v3))):
        kwin[g * grp:(g + 1) * grp, :] = kr[...]
        vwin[g * grp:(g + 1) * grp, :] = vr[...]
    n_groups = grid_rows // KV_GROUP
    win_row0 = KV_GROUP * jnp.clip(2 * i - 1, 0, n_groups - 4)
    win_keys = WIN_H * GRID_W
    lane = lax.broadcasted_iota(jnp.int32, (GRID_W, 2 * HEAD_DIM), 1)
    even = lane < HEAD_DIM

    def row_body(j, carry):
        irow = Q_ROWS * i + j
        r0 = jnp.clip(irow - WIN_H // 2, 0, grid_rows - WIN_H)
        koff = pl.multiple_of((r0 - win_row0) * GRID_W, GRID_W)
        brow = (WIN_H - 1) - (irow - r0)
        qoff = pl.multiple_of(j * GRID_W, GRID_W)
        pairs = range(NA_HEADS // 2)
        lanes = [slice(pair * 2 * HEAD_DIM, (pair + 1) * 2 * HEAD_DIM) for pair in pairs]
        s_loc, s_ctx = [], []
        for pair in pairs:
            qr = qr_ref[pl.ds(qoff, GRID_W), lanes[pair]]
            qp = q_ref[pl.ds(qoff, GRID_W), lanes[pair]]
            zero = jnp.zeros_like(qr)
            qr2 = jnp.concatenate([jnp.where(even, qr, zero), jnp.where(even, zero, qr)], axis=0)
            qp2 = jnp.concatenate([jnp.where(even, qp, zero), jnp.where(even, zero, qp)], axis=0)
            bias = jnp.concatenate(
                [jnp.concatenate([bias_ref[brow + 2 * m, 2 * pair], bias_ref[brow + 2 * m, 2 * pair + 1]], axis=0)
                 for m in range(WIN_H // 2)], axis=1)
            s_loc.append(_dot_nt(qr2, kwin[pl.ds(koff, win_keys), lanes[pair]]) + bias)
            s_ctx.append(_dot_nt(qp2, kc_ref[:, lanes[pair]]))
        p_loc, p_ctx, denom = [], [], []
        for pair in pairs:
            mx = jnp.maximum(jnp.max(s_loc[pair], axis=-1, keepdims=True),
                             jnp.max(s_ctx[pair], axis=-1, keepdims=True))
            pl_ = jnp.exp(s_loc[pair] - mx)
            pc_ = jnp.exp(s_ctx[pair] - mx)
            denom.append(jnp.sum(pl_, axis=-1, keepdims=True) + jnp.sum(pc_, axis=-1, keepdims=True))
            p_loc.append(pl_.astype(BF16))
            p_ctx.append(pc_.astype(BF16))
        for pair in pairs:
            o = (_dot(p_loc[pair], vwin[pl.ds(koff, win_keys), lanes[pair]])
                 + _dot(p_ctx[pair], vc_ref[:, lanes[pair]])) / denom[pair]
            o_ref[pl.ds(qoff, GRID_W), lanes[pair]] = jnp.where(
                even, o[0:GRID_W], o[GRID_W:2 * GRID_W]).astype(BF16)
        return carry

    lax.fori_loop(0, Q_ROWS, row_body, 0)


def _na_call(q, qr, kr, v, kc, vc, bias_tab, *, batch, seq, ctx_len):
    grid_rows = seq // GRID_W
    n_blocks = grid_rows // Q_ROWS
    n_groups = grid_rows // KV_GROUP
    blk = Q_ROWS * GRID_W
    grp = KV_GROUP * GRID_W

    def group_spec(g):
        return pl.BlockSpec((grp, NA_WIDTH),
                            lambda b, i: (b * n_groups + jnp.clip(2 * i - 1, 0, n_groups - 4) + g, 0))

    q_spec = pl.BlockSpec((blk, NA_WIDTH), lambda b, i: (b * n_blocks + i, 0))
    c_spec = pl.BlockSpec((ctx_len, NA_WIDTH), lambda b, i: (b, 0))
    return pl.pallas_call(
        functools.partial(_na_kernel, grid_rows),
        grid=(batch, n_blocks),
        in_specs=[q_spec, q_spec] + [group_spec(g) for g in range(4)] + [group_spec(g) for g in range(4)]
                 + [c_spec, c_spec, pl.BlockSpec(bias_tab.shape, lambda b, i: (0, 0, 0, 0))],
        out_specs=q_spec,
        out_shape=jax.ShapeDtypeStruct((batch * seq, NA_WIDTH), BF16),
        scratch_shapes=[pltpu.VMEM((4 * grp, NA_WIDTH), BF16), pltpu.VMEM((4 * grp, NA_WIDTH), BF16)],
        compiler_params=_cparams("parallel", "arbitrary"),
        name="natten",
    )(q, qr, kr, kr, kr, kr, v, v, v, v, kc, vc, bias_tab)


def _block_diag(x_cat, bd_mask):
    xb = x_cat.astype(BF16)
    tiled = jnp.concatenate([xb] * (QUAD // CHUNK), axis=0)
    return jnp.where(bd_mask, tiled, jnp.zeros_like(tiled))


def _diag_blocks(x, head_of_lane):
    n = QUAD // HEAD_DIM
    out = x[(n - 1) * HEAD_DIM:n * HEAD_DIM]
    for h in range(n - 2, -1, -1):
        out = jnp.where(head_of_lane == h, x[h * HEAD_DIM:(h + 1) * HEAD_DIM], out)
    return out


def _rwkv_kernel(reverse, with_gate, n_ctx, n_blocks,
                 lat_ref, lat_prev_ref, lat_next_ref, ctx_ref, ctx_prev_ref, ctx_next_ref,
                 mup_ref, mun_ref, w0_ref, w2_ref, a0_ref, a2_ref, g2_ref,
                 kk_ref, ka_ref, rk_ref, ones_ref, *refs):
    n_out = 3 if with_gate else 2
    y_ref, bonus_ref = refs[0:2]
    gate_ref = refs[2] if with_gate else None
    h_ref = refs[n_out]
    sets = (refs[n_out + 1:n_out + 7], refs[n_out + 7:n_out + 13])
    s = pl.program_id(1)

    @pl.when(s == 0)
    def _():
        h_ref[...] = jnp.zeros_like(h_ref)
        for ref in sets[1]:
            ref[...] = jnp.zeros_like(ref)

    @pl.when(s % 2 == 0)
    def _():
        _rwkv_step(reverse, n_ctx, n_blocks, s, sets[0], sets[1], lat_ref, lat_prev_ref, lat_next_ref,
                   ctx_ref, ctx_prev_ref, ctx_next_ref, mup_ref, mun_ref, w0_ref, w2_ref, a0_ref, a2_ref,
                   g2_ref, kk_ref, ka_ref, rk_ref, ones_ref, y_ref, bonus_ref, gate_ref, h_ref)

    @pl.when(s % 2 == 1)
    def _():
        _rwkv_step(reverse, n_ctx, n_blocks, s, sets[1], sets[0], lat_ref, lat_prev_ref, lat_next_ref,
                   ctx_ref, ctx_prev_ref, ctx_next_ref, mup_ref, mun_ref, w0_ref, w2_ref, a0_ref, a2_ref,
                   g2_ref, kk_ref, ka_ref, rk_ref, ones_ref, y_ref, bonus_ref, gate_ref, h_ref)


def _rwkv_step(reverse, n_ctx, n_blocks, s, wset, rset, lat_ref, lat_prev_ref, lat_next_ref,
               ctx_ref, ctx_prev_ref, ctx_next_ref, mup_ref, mun_ref, w0_ref, w2_ref, a0_ref, a2_ref,
               g2_ref, kk_ref, ka_ref, rk_ref, ones_ref, y_ref, bonus_ref, gate_ref, h_ref):
    sp = jnp.minimum(s, n_blocks - 1)
    if reverse:
        n = jnp.where(sp < n_ctx, n_ctx - 1 - sp, n_blocks + n_ctx - 1 - sp)
    else:
        n = sp

    rows = lat_ref.shape[0]
    n_ch = rows // CHUNK
    w = RW_WIDTH
    t = {}

    is_ctx = n < n_ctx
    has_prev = jnp.logical_and(n != 0, n != n_ctx)
    has_next = jnp.logical_and(n != n_ctx - 1, n != n_blocks - 1)
    row = lax.broadcasted_iota(jnp.int32, (rows, 1), 0)

    def shifted(c0, c1):
        p = jnp.where(is_ctx, ctx_ref[:, c0:c1], lat_ref[:, c0:c1])
        prow = jnp.where(is_ctx, ctx_prev_ref[HALO - 1:HALO, c0:c1], lat_prev_ref[HALO - 1:HALO, c0:c1])
        nrow = jnp.where(is_ctx, ctx_next_ref[0:1, c0:c1], lat_next_ref[0:1, c0:c1])
        prow = jnp.where(has_prev, prow, 0.0)
        nrow = jnp.where(has_next, nrow, 0.0)
        prev = jnp.where(row == 0, prow, pltpu.roll(p, 1, 0))
        nxt = jnp.where(row == rows - 1, nrow, pltpu.roll(p, rows - 1, 0))
        return p + mup_ref[:, c0:c1] * (prev - p) + mun_ref[:, c0:c1] * (nxt - p)

    def prep_lora():
        lora = shifted(3 * w, 3 * w + 3 * LORA_W)
        pw = lora[:, 0:LORA_W]
        pa = lora[:, LORA_W:2 * LORA_W]
        t["log_decay"] = -math.exp(-0.5) * jax.nn.sigmoid(w0_ref[...] + _dot_x3(jnp.tanh(pw), w2_ref[...]))
        t["a"] = jax.nn.sigmoid(a0_ref[...] + _dot_x3(pa, a2_ref[...]))
        if gate_ref is not None:
            gate_ref[...] = _dot_x3(jax.nn.sigmoid(lora[:, 2 * LORA_W:3 * LORA_W]), g2_ref[...])

    def prep_keys():
        k = shifted(w, 2 * w)
        ones = ones_ref[...]
        kk_raw = k * kk_ref[...]
        kk = kk_raw * lax.rsqrt(jnp.maximum(_dot_lhs_split(kk_raw * kk_raw, ones, 2), 1e-24))
        t.update(kk=kk, kd=k * (1.0 + (t["a"] - 1.0) * ka_ref[...]), bb=t["a"] * kk)

    def prep_bonus():
        r = shifted(0, w)
        v = shifted(2 * w, 3 * w)
        bonus_ref[...] = _dot_lhs_split(r * t["kd"] * rk_ref[...], ones_ref[...], 2) * v
        t.update(r=r, v=v)

    def finish_prep():
        r, v, kk, kd, bb, log_decay = (t[name] for name in ("r", "v", "kk", "kd", "bb", "log_decay"))
        ti = lax.broadcasted_iota(jnp.int32, (rows, rows), 0)
        si = lax.broadcasted_iota(jnp.int32, (rows, rows), 1)
        same_chunk = (ti // CHUNK) == (si // CHUNK)
        tri = jnp.where(jnp.logical_and(same_chunk, (si >= ti) if reverse else (si <= ti)), 1.0, 0.0).astype(BF16)
        lw_parts = _split(log_decay, 3)
        cum = _dot(tri, lw_parts[0]) + (_dot(tri, lw_parts[1]) + _dot(tri, lw_parts[2]))
        e_neg = jnp.exp(-cum)
        last = 0 if reverse else CHUNK - 1
        w_v, w_kkt, w_rt, w_kh, w_bh, w_et = wset
        w_v[...] = v
        w_kkt[...] = kk * jnp.exp(cum - log_decay)
        w_rt[...] = r * jnp.exp(cum)
        w_kh[...] = kd * e_neg
        w_bh[...] = bb * e_neg
        for c in range(n_ch):
            w_et[c:c + 1, :] = jnp.exp(cum[c * CHUNK + last:c * CHUNK + last + 1, :])

    r_v, r_kkt, r_rt, r_kh, r_bh, r_et = rset
    trow = lax.broadcasted_iota(jnp.int32, (CHUNK, QUAD), 0)
    tcol = lax.broadcasted_iota(jnp.int32, (CHUNK, QUAD), 1) % CHUNK
    strict = (trow < tcol) if reverse else (trow > tcol)
    incl = (trow <= tcol) if reverse else (trow >= tcol)
    on_diag = trow == tcol
    eye_cat = jnp.where(on_diag, 1.0, 0.0)
    brow = lax.broadcasted_iota(jnp.int32, (QUAD, QUAD), 0)
    bcol = lax.broadcasted_iota(jnp.int32, (QUAD, QUAD), 1)
    bd_mask = (brow // HEAD_DIM) == (bcol // HEAD_DIM)
    head_of_lane = lax.broadcasted_iota(jnp.int32, (CHUNK, QUAD), 1) // HEAD_DIM
    n_doublings = int(math.log2(CHUNK)) - 1
    heads = range(QUAD // HEAD_DIM)
    quads = range(RW_WIDTH // QUAD)
    order = list(range(n_ch - 1, -1, -1) if reverse else range(n_ch))

    chains = [(c, qd) for c in order for qd in quads]

    def rsl(c):
        return slice(c * CHUNK, (c + 1) * CHUNK)

    def lsl(qd):
        return slice(qd * QUAD, (qd + 1) * QUAD)

    v_q = {ch: r_v[rsl(ch[0]), lsl(ch[1])] for ch in chains}
    kkt_q = {ch: r_kkt[rsl(ch[0]), lsl(ch[1])] for ch in chains}
    rt_q = {ch: r_rt[rsl(ch[0]), lsl(ch[1])] for ch in chains}

    a_k, a_b, b_k, b_b = {}, {}, {}, {}
    for ch in chains:
        kh = r_kh[rsl(ch[0]), lsl(ch[1])].astype(BF16)
        bh = r_bh[rsl(ch[0]), lsl(ch[1])].astype(BF16)
        zero = jnp.zeros_like(kh)
        rhs = jnp.concatenate([jnp.where(head_of_lane == h, kh, zero) for h in heads]
                              + [jnp.where(head_of_lane == h, bh, zero) for h in heads], axis=0)
        lhs = jnp.concatenate([kkt_q[ch], rt_q[ch]], axis=0).astype(BF16)
        gram = _dot_nt(lhs, rhs)
        a_k[ch] = jnp.where(strict, gram[0:CHUNK, 0:QUAD], 0.0)
        a_b[ch] = jnp.where(strict, gram[0:CHUNK, QUAD:2 * QUAD], 0.0)
        b_k[ch] = jnp.where(incl, gram[CHUNK:2 * CHUNK, 0:QUAD], 0.0)
        b_b[ch] = jnp.where(incl, gram[CHUNK:2 * CHUNK, QUAD:2 * QUAD], 0.0).astype(BF16)

    akv, bkv = {}, {}
    for ch in chains:
        both = _dot(jnp.concatenate([a_k[ch], b_k[ch]], axis=0).astype(BF16), _block_diag(v_q[ch], bd_mask))
        akv[ch] = both[0:CHUNK]
        bkv[ch] = both[CHUNK:2 * CHUNK]

    m = {ch: -a_b[ch] for ch in chains}
    t_inv = {ch: eye_cat + m[ch] for ch in chains}
    for ch in chains:
        m[ch] = _dot(m[ch].astype(BF16), _block_diag(m[ch], bd_mask))
    prep_at = {0: prep_lora, 1: prep_keys, 2: prep_bonus, 4: finish_prep}
    for step in range(n_doublings):
        final = step == n_doublings - 1
        if step in prep_at:
            prep_at[step]()
        for ch in chains:
            m_bd = _block_diag(m[ch], bd_mask)
            if final:
                t_inv[ch] = t_inv[ch] + _dot(t_inv[ch].astype(BF16), m_bd)
            else:
                both = _dot(jnp.concatenate([m[ch], t_inv[ch]], axis=0).astype(BF16), m_bd)
                m[ch] = both[0:CHUNK]
                t_inv[ch] = t_inv[ch] + both[CHUNK:2 * CHUNK]

    p1, tkk = {}, {}
    for ch in chains:
        t_b = t_inv[ch].astype(BF16)
        p1[ch] = _dot(t_b, _block_diag(akv[ch], bd_mask))
        tkk[ch] = _dot(t_b, _block_diag(kkt_q[ch], bd_mask))

    y0, q_t, h0_cat, g_cat = {}, {}, {}, {}
    for ch in chains:
        c, qd = ch
        y0[ch] = bkv[ch] - _dot(b_b[ch], _block_diag(p1[ch], bd_mask))
        q_t[ch] = rt_q[ch] - _dot(b_b[ch], _block_diag(tkk[ch], bd_mask))
        e_q = r_et[c:c + 1, lsl(qd)]
        lhs_t = jnp.concatenate([r_kh[rsl(c), lsl(qd)] * e_q, r_bh[rsl(c), lsl(qd)] * e_q], axis=0).astype(BF16)
        rhs_t = jnp.concatenate([jnp.concatenate([v_q[ch], jnp.zeros_like(v_q[ch])], axis=1),
                                 jnp.concatenate([-p1[ch], -tkk[ch]], axis=1)], axis=0).astype(BF16)
        hg = _dot_tn(lhs_t, rhs_t)
        h0_cat[ch] = _diag_blocks(hg[:, 0:QUAD], head_of_lane)
        g_cat[ch] = _diag_blocks(hg[:, QUAD:2 * QUAD], head_of_lane) + jnp.where(on_diag, e_q, 0.0)

    h = {qd: h_ref[qd] for qd in quads}
    for c in order:
        for qd in quads:
            ch = (c, qd)
            o2 = _dot(jnp.concatenate([g_cat[ch], q_t[ch]], axis=0).astype(BF16), _block_diag(h[qd], bd_mask))
            h[qd] = o2[0:CHUNK] + h0_cat[ch]
            y_ref[rsl(c), lsl(qd)] = y0[ch] + o2[CHUNK:2 * CHUNK]
    for qd in quads:
        h_ref[qd] = h[qd]


def _rwkv_call(rw_lat, rw_ctx, weights, *, reverse, with_gate, batch, seq, ctx_len):
    assert CHUNK == HEAD_DIM
    blk_rows = RW_BLOCK_CHUNKS * CHUNK
    assert ctx_len % blk_rows == 0 and seq % blk_rows == 0
    n_ctx = ctx_len // blk_rows
    n_lat = seq // blk_rows
    n_chunks = n_ctx + n_lat
    halo_per_blk = blk_rows // HALO

    def block_of(s):
        if reverse:
            return jnp.where(s < n_ctx, n_ctx - 1 - s, n_chunks + n_ctx - 1 - s)
        return s

    def stream_specs(first, count):
        n_halo = batch * count * halo_per_blk

        def blk(b, s):
            return b * count + jnp.clip(block_of(jnp.minimum(s, n_chunks - 1)) - first, 0, count - 1)

        return [pl.BlockSpec((blk_rows, RW_COLS), lambda b, s: (blk(b, s), 0)),
                pl.BlockSpec((HALO, RW_COLS), lambda b, s: (jnp.maximum(blk(b, s) * halo_per_blk - 1, 0), 0)),
                pl.BlockSpec((HALO, RW_COLS),
                             lambda b, s: (jnp.minimum((blk(b, s) + 1) * halo_per_blk, n_halo - 1), 0))]

    def out_block(b, step):
        lat = jnp.maximum(step, n_ctx) - n_ctx
        return (b * n_lat + (n_lat - 1 - lat if reverse else lat), 0)

    def const_spec(a):
        return pl.BlockSpec(a.shape, lambda b, s: (0,) * a.ndim)

    prep_spec = pl.BlockSpec((blk_rows, RW_WIDTH), lambda b, s: out_block(b, jnp.minimum(s, n_chunks - 1)))
    y_spec = pl.BlockSpec((blk_rows, RW_WIDTH), lambda b, s: out_block(b, jnp.maximum(s - 1, 0)))
    out_shape = jax.ShapeDtypeStruct((batch * seq, RW_WIDTH), F32)
    n_out = 3 if with_gate else 2
    term_set = [pltpu.VMEM((blk_rows, RW_WIDTH), F32)] * 5 + [pltpu.VMEM((8, RW_WIDTH), F32)]
    return pl.pallas_call(
        functools.partial(_rwkv_kernel, reverse, with_gate, n_ctx, n_chunks),
        grid=(batch, n_chunks + 1),
        in_specs=stream_specs(n_ctx, n_lat) + stream_specs(0, n_ctx) + [const_spec(a) for a in weights],
        out_specs=[y_spec] + [prep_spec] * (n_out - 1),
        out_shape=[out_shape] * n_out,
        scratch_shapes=[pltpu.VMEM((RW_WIDTH // QUAD, CHUNK, QUAD), F32)] + term_set + term_set,
        compiler_params=_cparams("parallel", "arbitrary"),
        name="rwkv_bwd" if reverse else "rwkv_fwd",
    )(rw_lat, rw_lat, rw_lat, rw_ctx, rw_ctx, rw_ctx, *weights)


def _outproj_kernel(d_model, alpha, yna_ref, y0_ref, y1_ref, b0_ref, b1_ref, gate_ref, x_ref, mod_ref,
                    gng_ref, gnb_ref, ones_ref, w_ref, lng_ref, lnb_ref, o_ref):
    ones = ones_ref[...]
    inv_n = 1.0 / HEAD_DIM
    y = y0_ref[...] + y1_ref[...]
    mu = _dot_lhs_split(y, ones, 2) * inv_n
    yc = y - mu
    var = _dot_lhs_split(yc * yc, ones, 2) * inv_n
    yn = yc * lax.rsqrt(var + GN_EPS) * gng_ref[...] + gnb_ref[...]
    y_rw = ((yn + (b0_ref[...] + b1_ref[...])) * gate_ref[...]).astype(BF16)
    proj = _dot(yna_ref[...], w_ref[0:NA_WIDTH, :]) + _dot(y_rw, w_ref[NA_WIDTH:NA_WIDTH + RW_WIDTH, :])
    g1 = mod_ref[...][:, 2 * d_model:3 * d_model]
    o_ref[...] = _normalize(alpha * x_ref[...] + g1 * proj, LN_EPS) * lng_ref[...] + lnb_ref[...]


def _outproj_call(y_na, y0, y1, b0, b1, gate, x2d, mod3, gn_g, gn_b, ones, w_out_bf16, ln_g, ln_b, *, tm, seq, alpha):
    rows, d = x2d.shape
    tiles_per_batch = seq // tm
    half = pl.BlockSpec((tm, RW_WIDTH), lambda i: (i, 0))
    full = pl.BlockSpec((tm, d), lambda i: (i, 0))

    def const_spec(a):
        return pl.BlockSpec(a.shape, lambda i: (0,) * a.ndim)

    return pl.pallas_call(
        functools.partial(_outproj_kernel, d, alpha),
        grid=(rows // tm,),
        in_specs=[half] * 6 + [full, pl.BlockSpec((None, 1, mod3.shape[2]), lambda i: (i // tiles_per_batch, 0, 0))]
                 + [const_spec(a) for a in (gn_g, gn_b, ones, w_out_bf16, ln_g, ln_b)],
        out_specs=full,
        out_shape=jax.ShapeDtypeStruct((rows, d), F32),
        compiler_params=_cparams("parallel"),
        name="outproj",
    )(y_na, y0, y1, b0, b1, gate, x2d, mod3, gn_g, gn_b, ones, w_out_bf16, ln_g, ln_b)


def _mlp_kernel(d_model, alpha, ff_chunk, x_ref, mod_ref, w1_ref, w2_ref, lng_ref, lnb_ref, o_ref):
    x = x_ref[...]
    mod = mod_ref[...]
    shift = mod[:, 3 * d_model:4 * d_model]
    scale = mod[:, 4 * d_model:5 * d_model]
    g2 = mod[:, 5 * d_model:6 * d_model]
    xm = (_normalize(x, LN_EPS) * (1.0 + scale) + shift).astype(BF16)
    acc = None
    for c0 in range(0, w1_ref.shape[1], ff_chunk):
        hid = jnp.maximum(_dot(xm, w1_ref[:, c0:c0 + ff_chunk]), 0.0)
        t = _dot((hid * hid).astype(BF16), w2_ref[c0:c0 + ff_chunk, :])
        acc = t if acc is None else acc + t
    o_ref[...] = _normalize(alpha * x + g2 * acc, LN_EPS) * lng_ref[...] + lnb_ref[...]


def _mlp_call(x2d, mod3, w1_bf16, w2_bf16, ln_g, ln_b, *, tm, seq, alpha):
    rows, d = x2d.shape
    tiles_per_batch = seq // tm
    full = pl.BlockSpec((tm, d), lambda i: (i, 0))

    def resident(a):
        return pl.BlockSpec(a.shape, lambda i: (0,) * a.ndim, pipeline_mode=pl.Buffered(1))

    return pl.pallas_call(
        functools.partial(_mlp_kernel, d, alpha, 1024),
        grid=(rows // tm,),
        in_specs=[full, pl.BlockSpec((None, 1, mod3.shape[2]), lambda i: (i // tiles_per_batch, 0, 0)),
                  resident(w1_bf16), resident(w2_bf16), resident(ln_g), resident(ln_b)],
        out_specs=full,
        out_shape=jax.ShapeDtypeStruct((rows, d), F32),
        compiler_params=_cparams("parallel"),
        name="mlp",
    )(x2d, mod3, w1_bf16, w2_bf16, ln_g, ln_b)


def _rope_tables(seq):
    f = HEAD_DIM // 4
    t = np.arange(seq)
    row = (t // GRID_W).astype(np.float32)
    col = (t % GRID_W).astype(np.float32)
    inv = (ROPE_BASE ** (-np.arange(f, dtype=np.float32) / f)).astype(np.float32)
    ang_r = row[:, None] * inv[None, :]
    ang_c = col[:, None] * inv[None, :]
    zero = np.zeros_like(ang_r)
    cos = np.concatenate([np.cos(ang_r), np.cos(ang_r), np.cos(ang_c), np.cos(ang_c)], axis=1)
    sa = np.concatenate([-np.sin(ang_r), zero, -np.sin(ang_c), zero], axis=1)
    sb = np.concatenate([zero, np.sin(ang_r), zero, np.sin(ang_c)], axis=1)
    return tuple(jnp.asarray(np.concatenate([z, z], axis=1), F32) for z in (cos, sa, sb))


def _bias_table(rpb):
    qc = jnp.arange(GRID_W)[:, None]
    kc = jnp.arange(GRID_W)[None, :]
    c0 = jnp.clip(qc - WIN_W // 2, 0, GRID_W - WIN_W)
    inside = (kc >= c0) & (kc < c0 + WIN_W)
    off = jnp.clip(kc - qc + (WIN_W - 1), 0, 2 * WIN_W - 2)
    g = jnp.where(inside[None, None], rpb[:, :, off], NEG_BIAS)
    g = g.transpose(1, 0, 2, 3)
    return jnp.concatenate([g[:-1], g[1:]], axis=-1).astype(F32)


def _pad_lora(w, d):
    z = jnp.zeros_like(w[d])
    return jnp.concatenate([w[0] if d == 0 else z, w[1] if d == 1 else z], axis=0)


def kernel(x, c, ctx, c_ctx, w_mod, b_mod, w_in, w_out, ln1_g, ln1_b, mlp_w1, mlp_w2, ln2_g, ln2_b, na_rpb,
           rw_mu_prev, rw_mu_next, rw_w0, rw_w2, rw_a0, rw_a2, rw_g2, rw_k_k, rw_k_a, rw_r_k, rw_gn_g, rw_gn_b):
    depth = w_mod.shape[0]
    assert depth == 1, "single-layer trunk only (the context stream is never updated)"
    batch, seq, d = x.shape
    ctx_len = ctx.shape[1]
    alpha = (2 * depth) ** 0.25
    tm = ROW_TILE
    tm_ctx = math.gcd(ctx_len, ROW_TILE)
    assert seq % (Q_ROWS * GRID_W) == 0 and seq % tm == 0 and tm_ctx % 8 == 0
    assert batch + 1 <= 8

    cc = jnp.zeros((8, d), F32).at[:batch].set(c).at[batch].set(c_ctx)
    mod3 = _mod_call(cc, w_mod[0], b_mod[0]).reshape(8, 1, N_MOD * d)

    w_in_b = w_in[0].astype(BF16)
    tiles_lat = seq // tm
    x2d = x.reshape(batch * seq, d)
    q, qr, kr, v, rw_lat = _inproj_call(
        x2d, mod3, _rope_tables(seq), w_in_b, tm=tm,
        mod_index=lambda i: i // tiles_lat, table_index=lambda i: i % tiles_lat, name="inproj")
    ident = tuple(jnp.full((tm_ctx, 2 * HEAD_DIM), val, F32) for val in (1.0, 0.0, 0.0))
    _, _, kc, vc, rw_ctx = _inproj_call(
        ctx.reshape(batch * ctx_len, d), mod3, ident, w_in_b, tm=tm_ctx,
        mod_index=lambda i: batch, table_index=lambda i: 0, name="inproj_ctx")

    y_na = _na_call(q, qr, kr, v, kc, vc, _bias_table(na_rpb[0]), batch=batch, seq=seq, ctx_len=ctx_len)

    lane_head = jnp.arange(RW_WIDTH) // HEAD_DIM
    ones = (lane_head[:, None] == lane_head[None, :]).astype(BF16)
    row = lambda a: a.reshape(1, -1).astype(F32)
    outs = []
    for dirn in range(N_DIR):
        weights = (row(rw_mu_prev[0]), row(rw_mu_next[0]), row(rw_w0[0, dirn]), _pad_lora(rw_w2[0], dirn),
                   row(rw_a0[0, dirn]), _pad_lora(rw_a2[0], dirn), rw_g2[0], row(rw_k_k[0]), row(rw_k_a[0]),
                   row(rw_r_k[0]), ones)
        outs.append(_rwkv_call(rw_lat, rw_ctx, weights, reverse=dirn == 1, with_gate=dirn == 0,
                               batch=batch, seq=seq, ctx_len=ctx_len))
    (y0, b0, gate), (y1, b1) = outs

    x1 = _outproj_call(y_na, y0, y1, b0, b1, gate, x2d, mod3, row(rw_gn_g[0]), row(rw_gn_b[0]), ones,
                       w_out[0].astype(BF16), row(ln1_g[0]), row(ln1_b[0]), tm=tm, seq=seq, alpha=alpha)
    out = _mlp_call(x1, mod3, mlp_w1[0].astype(BF16), mlp_w2[0].astype(BF16), row(ln2_g[0]), row(ln2_b[0]),
                    tm=tm, seq=seq, alpha=alpha)
    return out.reshape(batch, seq, d)
```

```python
import functools
import math

import jax
import jax.numpy as jnp
import numpy as np
from jax import lax
from jax.experimental import pallas as pl
from jax.experimental.pallas import tpu as pltpu

HEAD_DIM = 64
NA_HEADS = 8
RW_HEADS = 8
NA_WIDTH = NA_HEADS * HEAD_DIM
RW_WIDTH = RW_HEADS * HEAD_DIM
GRID_W = 64
WIN_H = 8
WIN_W = 16
ROPE_BASE = 10000.0
N_DIR = 2
DECAY_LORA = 64
AAA_LORA = 64
GATE_LORA = 128
LORA_W = N_DIR * DECAY_LORA
NA_COLS = 3 * NA_WIDTH
RW_COLS = 3 * RW_WIDTH + 3 * LORA_W
N_MOD = 6
LN_EPS = 1e-6
GN_EPS = 64e-5
NEG_BIAS = -1e30

CHUNK = 64
RW_BLOCK_CHUNKS = 4
QUAD = 2 * HEAD_DIM
HALO = 8
LOG2_E = math.log2(math.e)
Q_ROWS = 8
NA_UNROLL = 2
KV_GROUP = 4
ROW_TILE = 512
VMEM_LIMIT = 56 * 1024 * 1024

F32 = jnp.float32
BF16 = jnp.bfloat16


def _cparams(*sem):
    return pltpu.CompilerParams(dimension_semantics=sem, vmem_limit_bytes=VMEM_LIMIT)


def _dot(a, b):
    return jnp.dot(a, b, preferred_element_type=F32)


def _dot_nt(a, b):
    return lax.dot_general(a, b, (((1,), (1,)), ((), ())), preferred_element_type=F32)


def _dot_tn(a, b):
    return lax.dot_general(a, b, (((0,), (0,)), ((), ())), preferred_element_type=F32)


def _split(x, pieces):
    out = []
    for _ in range(pieces):
        p = x.astype(BF16)
        out.append(p)
        x = x - p.astype(F32)
    return out


def _dot_lhs_split(x, w_bf16, pieces):
    acc = None
    for p in _split(x, pieces):
        t = _dot(p, w_bf16)
        acc = t if acc is None else acc + t
    return acc


def _dot_x3(a, b):
    a_hi, a_lo = _split(a, 2)
    b_hi, b_lo = _split(b, 2)
    return _dot(a_hi, b_hi) + (_dot(a_lo, b_hi) + _dot(a_hi, b_lo))


def _normalize(x, eps):
    mu = jnp.mean(x, axis=-1, keepdims=True)
    xc = x - mu
    var = jnp.mean(xc * xc, axis=-1, keepdims=True)
    return xc * lax.rsqrt(var + eps)


def _mod_kernel(c_ref, w_ref, b_ref, o_ref):
    c = c_ref[...]
    s = c * jax.nn.sigmoid(c)
    o_ref[...] = _dot_x3(s, w_ref[...]) + b_ref[...]


def _mod_call(cc, w_mod, b_mod):
    rows, d = cc.shape
    n = w_mod.shape[1]
    tn = 512
    return pl.pallas_call(
        _mod_kernel,
        grid=(n // tn,),
        in_specs=[pl.BlockSpec((rows, d), lambda j: (0, 0)),
                  pl.BlockSpec((d, tn), lambda j: (0, j)),
                  pl.BlockSpec((1, tn), lambda j: (0, j))],
        out_specs=pl.BlockSpec((rows, tn), lambda j: (0, j)),
        out_shape=jax.ShapeDtypeStruct((rows, n), F32),
        compiler_params=_cparams("arbitrary"),
        name="mod",
    )(cc, w_mod, b_mod.reshape(1, n))


def _inproj_kernel(d_model, x_ref, mod_ref, cos_ref, sa_ref, sb_ref, w_ref, q_ref, qr_ref, kr_ref, v_ref, rw_ref):
    x = x_ref[...]
    mod = mod_ref[...]
    shift = mod[:, 0:d_model]
    scale = mod[:, d_model:2 * d_model]
    xm = (_normalize(x, LN_EPS) * (1.0 + scale) + shift).astype(BF16)

    reps = NA_WIDTH // cos_ref.shape[1]
    cos = jnp.concatenate([cos_ref[...]] * reps, axis=1)
    sa = jnp.concatenate([sa_ref[...]] * reps, axis=1)
    sb = jnp.concatenate([sb_ref[...]] * reps, axis=1)

    def rope(z):
        up = pltpu.roll(z, NA_WIDTH - HEAD_DIM // 4, 1)
        down = pltpu.roll(z, HEAD_DIM // 4, 1)
        return z * cos + up * sa + down * sb

    qk_scale = HEAD_DIM ** -0.5 * LOG2_E
    q = _dot(xm, w_ref[:, 0:NA_WIDTH]) * qk_scale
    q_ref[...] = q.astype(BF16)
    qr_ref[...] = rope(q).astype(BF16)
    k = _dot(xm, w_ref[:, NA_WIDTH:2 * NA_WIDTH])
    kr_ref[...] = rope(k).astype(BF16)
    v_ref[...] = _dot(xm, w_ref[:, 2 * NA_WIDTH:NA_COLS]).astype(BF16)
    rw_ref[...] = _dot(xm, w_ref[:, NA_COLS:NA_COLS + RW_COLS])


def _inproj_call(x2d, mod3, tables, w_in_bf16, *, tm, mod_index, table_index, name):
    rows, d = x2d.shape
    cos, sa, sb = tables
    tw = cos.shape[1]
    ncols = w_in_bf16.shape[1]
    na_spec = pl.BlockSpec((tm, NA_WIDTH), lambda i: (i, 0))
    table_spec = pl.BlockSpec((tm, tw), lambda i: (table_index(i), 0))
    na_shape = jax.ShapeDtypeStruct((rows, NA_WIDTH), BF16)
    return pl.pallas_call(
        functools.partial(_inproj_kernel, d),
        grid=(rows // tm,),
        in_specs=[pl.BlockSpec((tm, d), lambda i: (i, 0)),
                  pl.BlockSpec((None, 1, mod3.shape[2]), lambda i: (mod_index(i), 0, 0)),
                  table_spec, table_spec, table_spec,
                  pl.BlockSpec((d, ncols), lambda i: (0, 0))],
        out_specs=[na_spec, na_spec, na_spec, na_spec, pl.BlockSpec((tm, RW_COLS), lambda i: (i, 0))],
        out_shape=[na_shape, na_shape, na_shape, na_shape, jax.ShapeDtypeStruct((rows, RW_COLS), F32)],
        compiler_params=_cparams("parallel"),
        name=name,
    )(x2d, mod3, cos, sa, sb, w_in_bf16)


def _na_kernel(grid_rows, q_ref, qr_ref, k0, k1, k2, k3, v0, v1, v2, v3, kc_ref, vc_ref, bias_ref,
               o_ref, kwin, vwin):
    i = pl.program_id(1)
    grp = KV_GROUP * GRID_W
    for g, (kr, vr) in enumerate(((k0, v0), (k1, v1), (k2, v2), (k3, v3))):
        kwin[g * grp:(g + 1) * grp, :] = kr[...]
        vwin[g * grp:(g + 1) * grp, :] = vr[...]
    n_groups = grid_rows // KV_GROUP
    win_row0 = KV_GROUP * jnp.clip(2 * i - 1, 0, n_groups - 4)
    win_keys = WIN_H * GRID_W
    lane = lax.broadcasted_iota(jnp.int32, (GRID_W, 2 * HEAD_DIM), 1)
    even = lane < HEAD_DIM

    pairs = range(NA_HEADS // 2)
    lanes = [slice(pair * 2 * HEAD_DIM, (pair + 1) * 2 * HEAD_DIM) for pair in pairs]

    def rows_body(jj, carry):
        units = []
        for u in range(NA_UNROLL):
            j = NA_UNROLL * jj + u
            irow = Q_ROWS * i + j
            r0 = jnp.clip(irow - WIN_H // 2, 0, grid_rows - WIN_H)
            koff = pl.multiple_of((r0 - win_row0) * GRID_W, GRID_W)
            brow = (WIN_H - 1) - (irow - r0)
            qoff = pl.multiple_of(j * GRID_W, GRID_W)
            units += [(koff, brow, qoff, pair) for pair in pairs]
        s_loc, s_ctx = [], []
        for koff, brow, qoff, pair in units:
            qr = qr_ref[pl.ds(qoff, GRID_W), lanes[pair]]
            qp = q_ref[pl.ds(qoff, GRID_W), lanes[pair]]
            zero = jnp.zeros_like(qr)
            qr2 = jnp.concatenate([jnp.where(even, qr, zero), jnp.where(even, zero, qr)], axis=0)
            qp2 = jnp.concatenate([jnp.where(even, qp, zero), jnp.where(even, zero, qp)], axis=0)
            bias = jnp.concatenate(
                [jnp.concatenate([bias_ref[brow + 2 * m, 2 * pair], bias_ref[brow + 2 * m, 2 * pair + 1]], axis=0)
                 for m in range(WIN_H // 2)], axis=1)
            s_loc.append(_dot_nt(qr2, kwin[pl.ds(koff, win_keys), lanes[pair]]) + bias)
            s_ctx.append(_dot_nt(qp2, kc_ref[:, lanes[pair]]))
        p_loc, p_ctx, denom = [], [], []
        for n in range(len(units)):
            mx = jnp.maximum(jnp.max(s_loc[n], axis=-1, keepdims=True), jnp.max(s_ctx[n], axis=-1, keepdims=True))
            pl_ = jnp.exp2(s_loc[n] - mx)
            pc_ = jnp.exp2(s_ctx[n] - mx)
            denom.append(jnp.sum(pl_, axis=-1, keepdims=True) + jnp.sum(pc_, axis=-1, keepdims=True))
            p_loc.append(pl_.astype(BF16))
            p_ctx.append(pc_.astype(BF16))
        for n, (koff, brow, qoff, pair) in enumerate(units):
            o = (_dot(p_loc[n], vwin[pl.ds(koff, win_keys), lanes[pair]])
                 + _dot(p_ctx[n], vc_ref[:, lanes[pair]])) / denom[n]
            o_ref[pl.ds(qoff, GRID_W), lanes[pair]] = jnp.where(
                even, o[0:GRID_W], o[GRID_W:2 * GRID_W]).astype(BF16)
        return carry

    lax.fori_loop(0, Q_ROWS // NA_UNROLL, rows_body, 0)


def _na_call(q, qr, kr, v, kc, vc, bias_tab, *, batch, seq, ctx_len):
    grid_rows = seq // GRID_W
    n_blocks = grid_rows // Q_ROWS
    n_groups = grid_rows // KV_GROUP
    blk = Q_ROWS * GRID_W
    grp = KV_GROUP * GRID_W

    def group_spec(g):
        return pl.BlockSpec((grp, NA_WIDTH),
                            lambda b, i: (b * n_groups + jnp.clip(2 * i - 1, 0, n_groups - 4) + g, 0))

    q_spec = pl.BlockSpec((blk, NA_WIDTH), lambda b, i: (b * n_blocks + i, 0))
    c_spec = pl.BlockSpec((ctx_len, NA_WIDTH), lambda b, i: (b, 0))
    return pl.pallas_call(
        functools.partial(_na_kernel, grid_rows),
        grid=(batch, n_blocks),
        in_specs=[q_spec, q_spec] + [group_spec(g) for g in range(4)] + [group_spec(g) for g in range(4)]
                 + [c_spec, c_spec, pl.BlockSpec(bias_tab.shape, lambda b, i: (0, 0, 0, 0))],
        out_specs=q_spec,
        out_shape=jax.ShapeDtypeStruct((batch * seq, NA_WIDTH), BF16),
        scratch_shapes=[pltpu.VMEM((4 * grp, NA_WIDTH), BF16), pltpu.VMEM((4 * grp, NA_WIDTH), BF16)],
        compiler_params=_cparams("parallel", "arbitrary"),
        name="natten",
    )(q, qr, kr, kr, kr, kr, v, v, v, v, kc, vc, bias_tab)


def _block_diag(x_cat, bd_mask):
    xb = x_cat.astype(BF16)
    tiled = jnp.concatenate([xb] * (QUAD // CHUNK), axis=0)
    return jnp.where(bd_mask, tiled, jnp.zeros_like(tiled))


def _diag_blocks(x, head_of_lane):
    n = QUAD // HEAD_DIM
    out = x[(n - 1) * HEAD_DIM:n * HEAD_DIM]
    for h in range(n - 2, -1, -1):
        out = jnp.where(head_of_lane == h, x[h * HEAD_DIM:(h + 1) * HEAD_DIM], out)
    return out


def _rwkv_kernel(reverse, with_gate, n_ctx, n_blocks,
                 lat_ref, lat_prev_ref, lat_next_ref, ctx_ref, ctx_prev_ref, ctx_next_ref,
                 mup_ref, mun_ref, w0_ref, w2_ref, a0_ref, a2_ref, g2_ref,
                 kk_ref, ka_ref, rk_ref, ones_ref, *refs):
    n_out = 3 if with_gate else 2
    y_ref, bonus_ref = refs[0:2]
    gate_ref = refs[2] if with_gate else None
    h_ref = refs[n_out]
    sets = (refs[n_out + 1:n_out + 7], refs[n_out + 7:n_out + 13])
    s = pl.program_id(1)

    @pl.when(s == 0)
    def _():
        h_ref[...] = jnp.zeros_like(h_ref)
        for ref in sets[1]:
            ref[...] = jnp.zeros_like(ref)

    @pl.when(s % 2 == 0)
    def _():
        _rwkv_step(reverse, n_ctx, n_blocks, s, sets[0], sets[1], lat_ref, lat_prev_ref, lat_next_ref,
                   ctx_ref, ctx_prev_ref, ctx_next_ref, mup_ref, mun_ref, w0_ref, w2_ref, a0_ref, a2_ref,
                   g2_ref, kk_ref, ka_ref, rk_ref, ones_ref, y_ref, bonus_ref, gate_ref, h_ref)

    @pl.when(s % 2 == 1)
    def _():
        _rwkv_step(reverse, n_ctx, n_blocks, s, sets[1], sets[0], lat_ref, lat_prev_ref, lat_next_ref,
                   ctx_ref, ctx_prev_ref, ctx_next_ref, mup_ref, mun_ref, w0_ref, w2_ref, a0_ref, a2_ref,
                   g2_ref, kk_ref, ka_ref, rk_ref, ones_ref, y_ref, bonus_ref, gate_ref, h_ref)


def _rwkv_step(reverse, n_ctx, n_blocks, s, wset, rset, lat_ref, lat_prev_ref, lat_next_ref,
               ctx_ref, ctx_prev_ref, ctx_next_ref, mup_ref, mun_ref, w0_ref, w2_ref, a0_ref, a2_ref,
               g2_ref, kk_ref, ka_ref, rk_ref, ones_ref, y_ref, bonus_ref, gate_ref, h_ref):
    sp = jnp.minimum(s, n_blocks - 1)
    if reverse:
        n = jnp.where(sp < n_ctx, n_ctx - 1 - sp, n_blocks + n_ctx - 1 - sp)
    else:
        n = sp

    rows = lat_ref.shape[0]
    n_ch = rows // CHUNK
    w = RW_WIDTH
    t = {}

    is_ctx = n < n_ctx
    has_prev = jnp.logical_and(n != 0, n != n_ctx)
    has_next = jnp.logical_and(n != n_ctx - 1, n != n_blocks - 1)
    row = lax.broadcasted_iota(jnp.int32, (rows, 1), 0)

    def shifted(c0, c1):
        p = jnp.where(is_ctx, ctx_ref[:, c0:c1], lat_ref[:, c0:c1])
        prow = jnp.where(is_ctx, ctx_prev_ref[HALO - 1:HALO, c0:c1], lat_prev_ref[HALO - 1:HALO, c0:c1])
        nrow = jnp.where(is_ctx, ctx_next_ref[0:1, c0:c1], lat_next_ref[0:1, c0:c1])
        prow = jnp.where(has_prev, prow, 0.0)
        nrow = jnp.where(has_next, nrow, 0.0)
        prev = jnp.where(row == 0, prow, pltpu.roll(p, 1, 0))
        nxt = jnp.where(row == rows - 1, nrow, pltpu.roll(p, rows - 1, 0))
        return p + mup_ref[:, c0:c1] * (prev - p) + mun_ref[:, c0:c1] * (nxt - p)

    def prep_lora():
        lora = shifted(3 * w, 3 * w + 3 * LORA_W)
        pw = lora[:, 0:LORA_W]
        pa = lora[:, LORA_W:2 * LORA_W]
        t["log_decay"] = -math.exp(-0.5) * jax.nn.sigmoid(
            w0_ref[...] + _dot(jnp.tanh(pw).astype(BF16), w2_ref[...]))
        t["a"] = jax.nn.sigmoid(a0_ref[...] + _dot(pa.astype(BF16), a2_ref[...]))
        if gate_ref is not None:
            gate_ref[...] = _dot(jax.nn.sigmoid(lora[:, 2 * LORA_W:3 * LORA_W]).astype(BF16), g2_ref[...])

    def prep_keys():
        k = shifted(w, 2 * w)
        ones = ones_ref[...]
        kk_raw = k * kk_ref[...]
        kk = kk_raw * lax.rsqrt(jnp.maximum(_dot_lhs_split(kk_raw * kk_raw, ones, 1), 1e-24))
        t.update(kk=kk, kd=k * (1.0 + (t["a"] - 1.0) * ka_ref[...]), bb=t["a"] * kk)

    def prep_bonus():
        r = shifted(0, w)
        v = shifted(2 * w, 3 * w)
        bonus_ref[...] = _dot_lhs_split(r * t["kd"] * rk_ref[...], ones_ref[...], 2) * v
        t.update(r=r, v=v)

    def finish_prep():
        r, v, kk, kd, bb, log_decay = (t[name] for name in ("r", "v", "kk", "kd", "bb", "log_decay"))
        ti = lax.broadcasted_iota(jnp.int32, (rows, rows), 0)
        si = lax.broadcasted_iota(jnp.int32, (rows, rows), 1)
        same_chunk = (ti // CHUNK) == (si // CHUNK)
        tri = jnp.where(jnp.logical_and(same_chunk, (si >= ti) if reverse else (si <= ti)), 1.0, 0.0).astype(BF16)
        lw_hi, lw_lo = _split(log_decay, 2)
        cum = _dot(tri, lw_hi) + _dot(tri, lw_lo)
        e_neg = jnp.exp(-cum)
        last = 0 if reverse else CHUNK - 1
        w_v, w_kkt, w_rt, w_kh, w_bh, w_et = wset
        w_v[...] = v
        w_kkt[...] = kk * jnp.exp(cum - log_decay)
        w_rt[...] = r * jnp.exp(cum)
        w_kh[...] = kd * e_neg
        w_bh[...] = bb * e_neg
        for c in range(n_ch):
            w_et[c:c + 1, :] = jnp.exp(cum[c * CHUNK + last:c * CHUNK + last + 1, :])

    r_v, r_kkt, r_rt, r_kh, r_bh, r_et = rset
    trow = lax.broadcasted_iota(jnp.int32, (CHUNK, QUAD), 0)
    tcol = lax.broadcasted_iota(jnp.int32, (CHUNK, QUAD), 1) % CHUNK
    strict = (trow < tcol) if reverse else (trow > tcol)
    incl = (trow <= tcol) if reverse else (trow >= tcol)
    on_diag = trow == tcol
    eye_cat = jnp.where(on_diag, 1.0, 0.0)
    brow = lax.broadcasted_iota(jnp.int32, (QUAD, QUAD), 0)
    bcol = lax.broadcasted_iota(jnp.int32, (QUAD, QUAD), 1)
    bd_mask = (brow // HEAD_DIM) == (bcol // HEAD_DIM)
    head_of_lane = lax.broadcasted_iota(jnp.int32, (CHUNK, QUAD), 1) // HEAD_DIM
    n_doublings = int(math.log2(CHUNK)) - 1
    heads = range(QUAD // HEAD_DIM)
    quads = range(RW_WIDTH // QUAD)
    order = list(range(n_ch - 1, -1, -1) if reverse else range(n_ch))

    chains = [(c, qd) for c in order for qd in quads]

    def rsl(c):
        return slice(c * CHUNK, (c + 1) * CHUNK)

    def lsl(qd):
        return slice(qd * QUAD, (qd + 1) * QUAD)

    v_q = {ch: r_v[rsl(ch[0]), lsl(ch[1])] for ch in chains}
    kkt_q = {ch: r_kkt[rsl(ch[0]), lsl(ch[1])] for ch in chains}
    rt_q = {ch: r_rt[rsl(ch[0]), lsl(ch[1])] for ch in chains}

    a_k, a_b, b_k, b_b = {}, {}, {}, {}
    for ch in chains:
        kh = r_kh[rsl(ch[0]), lsl(ch[1])].astype(BF16)
        bh = r_bh[rsl(ch[0]), lsl(ch[1])].astype(BF16)
        zero = jnp.zeros_like(kh)
        rhs = jnp.concatenate([jnp.where(head_of_lane == h, kh, zero) for h in heads]
                              + [jnp.where(head_of_lane == h, bh, zero) for h in heads], axis=0)
        lhs = jnp.concatenate([kkt_q[ch], rt_q[ch]], axis=0).astype(BF16)
        gram = _dot_nt(lhs, rhs)
        a_k[ch] = jnp.where(strict, gram[0:CHUNK, 0:QUAD], 0.0)
        a_b[ch] = jnp.where(strict, gram[0:CHUNK, QUAD:2 * QUAD], 0.0)
        b_k[ch] = jnp.where(incl, gram[CHUNK:2 * CHUNK, 0:QUAD], 0.0)
        b_b[ch] = jnp.where(incl, gram[CHUNK:2 * CHUNK, QUAD:2 * QUAD], 0.0).astype(BF16)

    akv, bkv = {}, {}
    for ch in chains:
        both = _dot(jnp.concatenate([a_k[ch], b_k[ch]], axis=0).astype(BF16), _block_diag(v_q[ch], bd_mask))
        akv[ch] = both[0:CHUNK]
        bkv[ch] = both[CHUNK:2 * CHUNK]

    m = {ch: -a_b[ch] for ch in chains}
    t_inv = {ch: eye_cat + m[ch] for ch in chains}
    for ch in chains:
        m[ch] = _dot(m[ch].astype(BF16), _block_diag(m[ch], bd_mask))
    prep_at = {0: prep_lora, 1: prep_keys, 2: prep_bonus, 4: finish_prep}
    for step in range(n_doublings):
        final = step == n_doublings - 1
        if step in prep_at:
            prep_at[step]()
        for ch in chains:
            m_bd = _block_diag(m[ch], bd_mask)
            if final:
                t_inv[ch] = t_inv[ch] + _dot(t_inv[ch].astype(BF16), m_bd)
            else:
                both = _dot(jnp.concatenate([m[ch], t_inv[ch]], axis=0).astype(BF16), m_bd)
                m[ch] = both[0:CHUNK]
                t_inv[ch] = t_inv[ch] + both[CHUNK:2 * CHUNK]

    p1, tkk = {}, {}
    for ch in chains:
        t_b = t_inv[ch].astype(BF16)
        p1[ch] = _dot(t_b, _block_diag(akv[ch], bd_mask))
        tkk[ch] = _dot(t_b, _block_diag(kkt_q[ch], bd_mask))

    y0, q_t, h0_cat, g_cat = {}, {}, {}, {}
    for ch in chains:
        c, qd = ch
        y0[ch] = bkv[ch] - _dot(b_b[ch], _block_diag(p1[ch], bd_mask))
        q_t[ch] = rt_q[ch] - _dot(b_b[ch], _block_diag(tkk[ch], bd_mask))
        e_q = r_et[c:c + 1, lsl(qd)]
        lhs_t = jnp.concatenate([r_kh[rsl(c), lsl(qd)] * e_q, r_bh[rsl(c), lsl(qd)] * e_q], axis=0).astype(BF16)
        rhs_t = jnp.concatenate([jnp.concatenate([v_q[ch], jnp.zeros_like(v_q[ch])], axis=1),
                                 jnp.concatenate([-p1[ch], -tkk[ch]], axis=1)], axis=0).astype(BF16)
        hg = _dot_tn(lhs_t, rhs_t)
        h0_cat[ch] = _diag_blocks(hg[:, 0:QUAD], head_of_lane)
        g_cat[ch] = _diag_blocks(hg[:, QUAD:2 * QUAD], head_of_lane) + jnp.where(on_diag, e_q, 0.0)

    h = {qd: h_ref[qd] for qd in quads}
    for c in order:
        for qd in quads:
            ch = (c, qd)
            o2 = _dot(jnp.concatenate([g_cat[ch], q_t[ch]], axis=0).astype(BF16), _block_diag(h[qd], bd_mask))
            h[qd] = o2[0:CHUNK] + h0_cat[ch]
            y_ref[rsl(c), lsl(qd)] = y0[ch] + o2[CHUNK:2 * CHUNK]
    for qd in quads:
        h_ref[qd] = h[qd]


def _rwkv_call(rw_lat, rw_ctx, weights, *, reverse, with_gate, batch, seq, ctx_len):
    assert CHUNK == HEAD_DIM
    blk_rows = RW_BLOCK_CHUNKS * CHUNK
    assert ctx_len % blk_rows == 0 and seq % blk_rows == 0
    n_ctx = ctx_len // blk_rows
    n_lat = seq // blk_rows
    n_chunks = n_ctx + n_lat
    halo_per_blk = blk_rows // HALO

    def block_of(s):
        if reverse:
            return jnp.where(s < n_ctx, n_ctx - 1 - s, n_chunks + n_ctx - 1 - s)
        return s

    def stream_specs(first, count):
        n_halo = batch * count * halo_per_blk

        def blk(b, s):
            return b * count + jnp.clip(block_of(jnp.minimum(s, n_chunks - 1)) - first, 0, count - 1)

        return [pl.BlockSpec((blk_rows, RW_COLS), lambda b, s: (blk(b, s), 0)),
                pl.BlockSpec((HALO, RW_COLS), lambda b, s: (jnp.maximum(blk(b, s) * halo_per_blk - 1, 0), 0)),
                pl.BlockSpec((HALO, RW_COLS),
                             lambda b, s: (jnp.minimum((blk(b, s) + 1) * halo_per_blk, n_halo - 1), 0))]

    def out_block(b, step):
        lat = jnp.maximum(step, n_ctx) - n_ctx
        return (b * n_lat + (n_lat - 1 - lat if reverse else lat), 0)

    def const_spec(a):
        return pl.BlockSpec(a.shape, lambda b, s: (0,) * a.ndim)

    prep_spec = pl.BlockSpec((blk_rows, RW_WIDTH), lambda b, s: out_block(b, jnp.minimum(s, n_chunks - 1)))
    y_spec = pl.BlockSpec((blk_rows, RW_WIDTH), lambda b, s: out_block(b, jnp.maximum(s - 1, 0)))
    out_shape = jax.ShapeDtypeStruct((batch * seq, RW_WIDTH), F32)
    n_out = 3 if with_gate else 2
    term_set = [pltpu.VMEM((blk_rows, RW_WIDTH), F32)] * 5 + [pltpu.VMEM((8, RW_WIDTH), F32)]
    return pl.pallas_call(
        functools.partial(_rwkv_kernel, reverse, with_gate, n_ctx, n_chunks),
        grid=(batch, n_chunks + 1),
        in_specs=stream_specs(n_ctx, n_lat) + stream_specs(0, n_ctx) + [const_spec(a) for a in weights],
        out_specs=[y_spec] + [prep_spec] * (n_out - 1),
        out_shape=[out_shape] * n_out,
        scratch_shapes=[pltpu.VMEM((RW_WIDTH // QUAD, CHUNK, QUAD), F32)] + term_set + term_set,
        compiler_params=_cparams("parallel", "arbitrary"),
        name="rwkv_bwd" if reverse else "rwkv_fwd",
    )(rw_lat, rw_lat, rw_lat, rw_ctx, rw_ctx, rw_ctx, *weights)


def _outproj_kernel(d_model, alpha, yna_ref, y0_ref, y1_ref, b0_ref, b1_ref, gate_ref, x_ref, mod_ref,
                    gng_ref, gnb_ref, ones_ref, w_ref, lng_ref, lnb_ref, o_ref):
    ones = ones_ref[...]
    inv_n = 1.0 / HEAD_DIM
    y = y0_ref[...] + y1_ref[...]
    mu = _dot_lhs_split(y, ones, 2) * inv_n
    yc = y - mu
    var = _dot_lhs_split(yc * yc, ones, 2) * inv_n
    yn = yc * lax.rsqrt(var + GN_EPS) * gng_ref[...] + gnb_ref[...]
    y_rw = ((yn + (b0_ref[...] + b1_ref[...])) * gate_ref[...]).astype(BF16)
    proj = _dot(yna_ref[...], w_ref[0:NA_WIDTH, :]) + _dot(y_rw, w_ref[NA_WIDTH:NA_WIDTH + RW_WIDTH, :])
    g1 = mod_ref[...][:, 2 * d_model:3 * d_model]
    o_ref[...] = _normalize(alpha * x_ref[...] + g1 * proj, LN_EPS) * lng_ref[...] + lnb_ref[...]


def _outproj_call(y_na, y0, y1, b0, b1, gate, x2d, mod3, gn_g, gn_b, ones, w_out_bf16, ln_g, ln_b, *, tm, seq, alpha):
    rows, d = x2d.shape
    tiles_per_batch = seq // tm
    half = pl.BlockSpec((tm, RW_WIDTH), lambda i: (i, 0))
    full = pl.BlockSpec((tm, d), lambda i: (i, 0))

    def const_spec(a):
        return pl.BlockSpec(a.shape, lambda i: (0,) * a.ndim)

    return pl.pallas_call(
        functools.partial(_outproj_kernel, d, alpha),
        grid=(rows // tm,),
        in_specs=[half] * 6 + [full, pl.BlockSpec((None, 1, mod3.shape[2]), lambda i: (i // tiles_per_batch, 0, 0))]
                 + [const_spec(a) for a in (gn_g, gn_b, ones, w_out_bf16, ln_g, ln_b)],
        out_specs=full,
        out_shape=jax.ShapeDtypeStruct((rows, d), F32),
        compiler_params=_cparams("parallel"),
        name="outproj",
    )(y_na, y0, y1, b0, b1, gate, x2d, mod3, gn_g, gn_b, ones, w_out_bf16, ln_g, ln_b)


def _mlp_kernel(d_model, alpha, ff_chunk, x_ref, mod_ref, w1_ref, w2_ref, lng_ref, lnb_ref, o_ref):
    x = x_ref[...]
    mod = mod_ref[...]
    shift = mod[:, 3 * d_model:4 * d_model]
    scale = mod[:, 4 * d_model:5 * d_model]
    g2 = mod[:, 5 * d_model:6 * d_model]
    xm = (_normalize(x, LN_EPS) * (1.0 + scale) + shift).astype(BF16)
    acc = None
    for c0 in range(0, w1_ref.shape[1], ff_chunk):
        hid = jnp.maximum(_dot(xm, w1_ref[:, c0:c0 + ff_chunk]), 0.0)
        t = _dot((hid * hid).astype(BF16), w2_ref[c0:c0 + ff_chunk, :])
        acc = t if acc is None else acc + t
    o_ref[...] = _normalize(alpha * x + g2 * acc, LN_EPS) * lng_ref[...] + lnb_ref[...]


def _mlp_call(x2d, mod3, w1_bf16, w2_bf16, ln_g, ln_b, *, tm, seq, alpha):
    rows, d = x2d.shape
    tiles_per_batch = seq // tm
    full = pl.BlockSpec((tm, d), lambda i: (i, 0))

    def resident(a):
        return pl.BlockSpec(a.shape, lambda i: (0,) * a.ndim, pipeline_mode=pl.Buffered(1))

    return pl.pallas_call(
        functools.partial(_mlp_kernel, d, alpha, 1024),
        grid=(rows // tm,),
        in_specs=[full, pl.BlockSpec((None, 1, mod3.shape[2]), lambda i: (i // tiles_per_batch, 0, 0)),
                  resident(w1_bf16), resident(w2_bf16), resident(ln_g), resident(ln_b)],
        out_specs=full,
        out_shape=jax.ShapeDtypeStruct((rows, d), F32),
        compiler_params=_cparams("parallel"),
        name="mlp",
    )(x2d, mod3, w1_bf16, w2_bf16, ln_g, ln_b)


def _rope_tables(seq):
    f = HEAD_DIM // 4
    t = np.arange(seq)
    row = (t // GRID_W).astype(np.float32)
    col = (t % GRID_W).astype(np.float32)
    inv = (ROPE_BASE ** (-np.arange(f, dtype=np.float32) / f)).astype(np.float32)
    ang_r = row[:, None] * inv[None, :]
    ang_c = col[:, None] * inv[None, :]
    zero = np.zeros_like(ang_r)
    cos = np.concatenate([np.cos(ang_r), np.cos(ang_r), np.cos(ang_c), np.cos(ang_c)], axis=1)
    sa = np.concatenate([-np.sin(ang_r), zero, -np.sin(ang_c), zero], axis=1)
    sb = np.concatenate([zero, np.sin(ang_r), zero, np.sin(ang_c)], axis=1)
    return tuple(jnp.asarray(np.concatenate([z, z], axis=1), F32) for z in (cos, sa, sb))


def _bias_table(rpb):
    qc = np.arange(GRID_W)[:, None]
    kc = np.arange(GRID_W)[None, :]
    c0 = np.clip(qc - WIN_W // 2, 0, GRID_W - WIN_W)
    inside = (kc >= c0) & (kc < c0 + WIN_W)
    pad = GRID_W - WIN_W
    padded = jnp.pad(rpb.astype(F32) * LOG2_E, ((0, 0), (0, 0), (pad, pad)))
    g = jnp.stack([padded[:, :, GRID_W - 1 - q:2 * GRID_W - 1 - q] for q in range(GRID_W)], axis=2)
    g = jnp.where(inside[None, None], g, NEG_BIAS)
    g = g.transpose(1, 0, 2, 3)
    return jnp.concatenate([g[:-1], g[1:]], axis=-1)


def _pad_lora(w, d):
    z = jnp.zeros_like(w[d])
    return jnp.concatenate([w[0] if d == 0 else z, w[1] if d == 1 else z], axis=0)


def kernel(x, c, ctx, c_ctx, w_mod, b_mod, w_in, w_out, ln1_g, ln1_b, mlp_w1, mlp_w2, ln2_g, ln2_b, na_rpb,
           rw_mu_prev, rw_mu_next, rw_w0, rw_w2, rw_a0, rw_a2, rw_g2, rw_k_k, rw_k_a, rw_r_k, rw_gn_g, rw_gn_b):
    depth = w_mod.shape[0]
    assert depth == 1, "single-layer trunk only (the context stream is never updated)"
    batch, seq, d = x.shape
    ctx_len = ctx.shape[1]
    alpha = (2 * depth) ** 0.25
    tm = ROW_TILE
    tm_ctx = math.gcd(ctx_len, ROW_TILE)
    assert seq % (Q_ROWS * GRID_W) == 0 and seq % tm == 0 and tm_ctx % 8 == 0
    assert batch + 1 <= 8

    cc = jnp.zeros((8, d), F32).at[:batch].set(c).at[batch].set(c_ctx)
    mod3 = _mod_call(cc, w_mod[0], b_mod[0]).reshape(8, 1, N_MOD * d)

    w_in_b = w_in[0].astype(BF16)
    tiles_lat = seq // tm
    x2d = x.reshape(batch * seq, d)
    q, qr, kr, v, rw_lat = _inproj_call(
        x2d, mod3, _rope_tables(seq), w_in_b, tm=tm,
        mod_index=lambda i: i // tiles_lat, table_index=lambda i: i % tiles_lat, name="inproj")
    ident = tuple(jnp.full((tm_ctx, 2 * HEAD_DIM), val, F32) for val in (1.0, 0.0, 0.0))
    _, _, kc, vc, rw_ctx = _inproj_call(
        ctx.reshape(batch * ctx_len, d), mod3, ident, w_in_b, tm=tm_ctx,
        mod_index=lambda i: batch, table_index=lambda i: 0, name="inproj_ctx")

    y_na = _na_call(q, qr, kr, v, kc, vc, _bias_table(na_rpb[0]), batch=batch, seq=seq, ctx_len=ctx_len)

    lane_head = jnp.arange(RW_WIDTH) // HEAD_DIM
    ones = (lane_head[:, None] == lane_head[None, :]).astype(BF16)
    row = lambda a: a.reshape(1, -1).astype(F32)
    outs = []
    for dirn in range(N_DIR):
        weights = (row(rw_mu_prev[0]), row(rw_mu_next[0]), row(rw_w0[0, dirn]),
                   _pad_lora(rw_w2[0], dirn).astype(BF16), row(rw_a0[0, dirn]),
                   _pad_lora(rw_a2[0], dirn).astype(BF16), rw_g2[0].astype(BF16),
                   row(rw_k_k[0]), row(rw_k_a[0]), row(rw_r_k[0]), ones)
        outs.append(_rwkv_call(rw_lat, rw_ctx, weights, reverse=dirn == 1, with_gate=dirn == 0,
                               batch=batch, seq=seq, ctx_len=ctx_len))
    (y0, b0, gate), (y1, b1) = outs

    x1 = _outproj_call(y_na, y0, y1, b0, b1, gate, x2d, mod3, row(rw_gn_g[0]), row(rw_gn_b[0]), ones,
                       w_out[0].astype(BF16), row(ln1_g[0]), row(ln1_b[0]), tm=tm, seq=seq, alpha=alpha)
    out = _mlp_call(x1, mod3, mlp_w1[0].astype(BF16), mlp_w2[0].astype(BF16), row(ln2_g[0]), row(ln2_b[0]),
                    tm=tm, seq=seq, alpha=alpha)
    return out.reshape(batch, seq, d)
```

```python
import functools
import math

import jax
import jax.numpy as jnp
import numpy as np
from jax import lax
from jax.experimental import pallas as pl
from jax.experimental.pallas import tpu as pltpu

HEAD_DIM = 64
NA_HEADS = 8
RW_HEADS = 8
NA_WIDTH = NA_HEADS * HEAD_DIM
RW_WIDTH = RW_HEADS * HEAD_DIM
GRID_W = 64
WIN_H = 8
WIN_W = 16
ROPE_BASE = 10000.0
N_DIR = 2
DECAY_LORA = 64
AAA_LORA = 64
GATE_LORA = 128
LORA_W = N_DIR * DECAY_LORA
NA_COLS = 3 * NA_WIDTH
RW_COLS = 3 * RW_WIDTH + 3 * LORA_W
N_MOD = 6
LN_EPS = 1e-6
GN_EPS = 64e-5
NEG_BIAS = -1e30

CHUNK = 64
RW_BLOCK_CHUNKS = 4
QUAD = 2 * HEAD_DIM
HALO = 8
LOG2_E = math.log2(math.e)
Q_ROWS = 8
NA_UNROLL = 2
KV_GROUP = 4
ROW_TILE = 512
VMEM_LIMIT = 56 * 1024 * 1024

F32 = jnp.float32
BF16 = jnp.bfloat16


def _cparams(*sem):
    return pltpu.CompilerParams(dimension_semantics=sem, vmem_limit_bytes=VMEM_LIMIT)


def _dot(a, b):
    return jnp.dot(a, b, preferred_element_type=F32)


def _dot_nt(a, b):
    return lax.dot_general(a, b, (((1,), (1,)), ((), ())), preferred_element_type=F32)


def _dot_tn(a, b):
    return lax.dot_general(a, b, (((0,), (0,)), ((), ())), preferred_element_type=F32)


def _split(x, pieces):
    out = []
    for _ in range(pieces):
        p = x.astype(BF16)
        out.append(p)
        x = x - p.astype(F32)
    return out


def _dot_lhs_split(x, w_bf16, pieces):
    acc = None
    for p in _split(x, pieces):
        t = _dot(p, w_bf16)
        acc = t if acc is None else acc + t
    return acc


def _dot_x3(a, b):
    a_hi, a_lo = _split(a, 2)
    b_hi, b_lo = _split(b, 2)
    return _dot(a_hi, b_hi) + (_dot(a_lo, b_hi) + _dot(a_hi, b_lo))


def _normalize(x, eps):
    mu = jnp.mean(x, axis=-1, keepdims=True)
    xc = x - mu
    var = jnp.mean(xc * xc, axis=-1, keepdims=True)
    return xc * lax.rsqrt(var + eps)


def _mod_kernel(c_ref, w_ref, b_ref, o_ref):
    c = c_ref[...]
    s = c * jax.nn.sigmoid(c)
    o_ref[...] = _dot(s.astype(BF16), w_ref[...].astype(BF16)) + b_ref[...]


def _mod_call(cc, w_mod, b_mod):
    rows, d = cc.shape
    n = w_mod.shape[1]
    tn = 512
    return pl.pallas_call(
        _mod_kernel,
        grid=(n // tn,),
        in_specs=[pl.BlockSpec((rows, d), lambda j: (0, 0)),
                  pl.BlockSpec((d, tn), lambda j: (0, j)),
                  pl.BlockSpec((1, tn), lambda j: (0, j))],
        out_specs=pl.BlockSpec((rows, tn), lambda j: (0, j)),
        out_shape=jax.ShapeDtypeStruct((rows, n), F32),
        compiler_params=_cparams("arbitrary"),
        name="mod",
    )(cc, w_mod, b_mod.reshape(1, n))


def _inproj_kernel(d_model, x_ref, mod_ref, cos_ref, sa_ref, sb_ref, w_ref, q_ref, qr_ref, kr_ref, v_ref, rw_ref):
    x = x_ref[...]
    mod = mod_ref[...]
    shift = mod[:, 0:d_model]
    scale = mod[:, d_model:2 * d_model]
    xm = (_normalize(x, LN_EPS) * (1.0 + scale) + shift).astype(BF16)

    reps = NA_WIDTH // cos_ref.shape[1]
    cos = jnp.concatenate([cos_ref[...]] * reps, axis=1)
    sa = jnp.concatenate([sa_ref[...]] * reps, axis=1)
    sb = jnp.concatenate([sb_ref[...]] * reps, axis=1)

    def rope(z):
        up = pltpu.roll(z, NA_WIDTH - HEAD_DIM // 4, 1)
        down = pltpu.roll(z, HEAD_DIM // 4, 1)
        return z * cos + up * sa + down * sb

    qk_scale = HEAD_DIM ** -0.5 * LOG2_E
    q = _dot(xm, w_ref[:, 0:NA_WIDTH]) * qk_scale
    q_ref[...] = q.astype(BF16)
    qr_ref[...] = rope(q).astype(BF16)
    k = _dot(xm, w_ref[:, NA_WIDTH:2 * NA_WIDTH])
    kr_ref[...] = rope(k).astype(BF16)
    v_ref[...] = _dot(xm, w_ref[:, 2 * NA_WIDTH:NA_COLS]).astype(BF16)
    rw_ref[...] = _dot(xm, w_ref[:, NA_COLS:NA_COLS + RW_COLS])


def _inproj_call(x2d, mod3, tables, w_in_bf16, *, tm, mod_index, table_index, name):
    rows, d = x2d.shape
    cos, sa, sb = tables
    tw = cos.shape[1]
    ncols = w_in_bf16.shape[1]
    na_spec = pl.BlockSpec((tm, NA_WIDTH), lambda i: (i, 0))
    table_spec = pl.BlockSpec((tm, tw), lambda i: (table_index(i), 0))
    na_shape = jax.ShapeDtypeStruct((rows, NA_WIDTH), BF16)
    return pl.pallas_call(
        functools.partial(_inproj_kernel, d),
        grid=(rows // tm,),
        in_specs=[pl.BlockSpec((tm, d), lambda i: (i, 0)),
                  pl.BlockSpec((None, 1, mod3.shape[2]), lambda i: (mod_index(i), 0, 0)),
                  table_spec, table_spec, table_spec,
                  pl.BlockSpec((d, ncols), lambda i: (0, 0))],
        out_specs=[na_spec, na_spec, na_spec, na_spec, pl.BlockSpec((tm, RW_COLS), lambda i: (i, 0))],
        out_shape=[na_shape, na_shape, na_shape, na_shape, jax.ShapeDtypeStruct((rows, RW_COLS), F32)],
        compiler_params=_cparams("parallel"),
        name=name,
    )(x2d, mod3, cos, sa, sb, w_in_bf16)


def _na_kernel(grid_rows, q_ref, qr_ref, k0, k1, k2, k3, v0, v1, v2, v3, kc_ref, vc_ref, bias_ref,
               o_ref, kwin, vwin):
    i = pl.program_id(1)
    grp = KV_GROUP * GRID_W
    for g, (kr, vr) in enumerate(((k0, v0), (k1, v1), (k2, v2), (k3, v3))):
        kwin[g * grp:(g + 1) * grp, :] = kr[...]
        vwin[g * grp:(g + 1) * grp, :] = vr[...]
    n_groups = grid_rows // KV_GROUP
    win_row0 = KV_GROUP * jnp.clip(2 * i - 1, 0, n_groups - 4)
    win_keys = WIN_H * GRID_W
    lane = lax.broadcasted_iota(jnp.int32, (GRID_W, 2 * HEAD_DIM), 1)
    even = lane < HEAD_DIM

    pairs = range(NA_HEADS // 2)
    lanes = [slice(pair * 2 * HEAD_DIM, (pair + 1) * 2 * HEAD_DIM) for pair in pairs]

    def rows_body(jj, carry):
        units = []
        for u in range(NA_UNROLL):
            j = NA_UNROLL * jj + u
            irow = Q_ROWS * i + j
            r0 = jnp.clip(irow - WIN_H // 2, 0, grid_rows - WIN_H)
            koff = pl.multiple_of((r0 - win_row0) * GRID_W, GRID_W)
            brow = (WIN_H - 1) - (irow - r0)
            qoff = pl.multiple_of(j * GRID_W, GRID_W)
            units += [(koff, brow, qoff, pair) for pair in pairs]
        s_loc, s_ctx = [], []
        for koff, brow, qoff, pair in units:
            qr = qr_ref[pl.ds(qoff, GRID_W), lanes[pair]]
            qp = q_ref[pl.ds(qoff, GRID_W), lanes[pair]]
            zero = jnp.zeros_like(qr)
            qr2 = jnp.concatenate([jnp.where(even, qr, zero), jnp.where(even, zero, qr)], axis=0)
            qp2 = jnp.concatenate([jnp.where(even, qp, zero), jnp.where(even, zero, qp)], axis=0)
            bias = jnp.concatenate(
                [jnp.concatenate([bias_ref[brow + 2 * m, 2 * pair], bias_ref[brow + 2 * m, 2 * pair + 1]], axis=0)
                 for m in range(WIN_H // 2)], axis=1)
            s_loc.append(_dot_nt(qr2, kwin[pl.ds(koff, win_keys), lanes[pair]]) + bias)
            s_ctx.append(_dot_nt(qp2, kc_ref[:, lanes[pair]]))
        p_loc, p_ctx, denom = [], [], []
        for n in range(len(units)):
            mx = jnp.maximum(jnp.max(s_loc[n], axis=-1, keepdims=True), jnp.max(s_ctx[n], axis=-1, keepdims=True))
            pl_ = jnp.exp2(s_loc[n] - mx)
            pc_ = jnp.exp2(s_ctx[n] - mx)
            denom.append(jnp.sum(pl_, axis=-1, keepdims=True) + jnp.sum(pc_, axis=-1, keepdims=True))
            p_loc.append(pl_.astype(BF16))
            p_ctx.append(pc_.astype(BF16))
        for n, (koff, brow, qoff, pair) in enumerate(units):
            o = (_dot(p_loc[n], vwin[pl.ds(koff, win_keys), lanes[pair]])
                 + _dot(p_ctx[n], vc_ref[:, lanes[pair]])) / denom[n]
            o_ref[pl.ds(qoff, GRID_W), lanes[pair]] = jnp.where(
                even, o[0:GRID_W], o[GRID_W:2 * GRID_W]).astype(BF16)
        return carry

    lax.fori_loop(0, Q_ROWS // NA_UNROLL, rows_body, 0)


def _na_call(q, qr, kr, v, kc, vc, bias_tab, *, batch, seq, ctx_len):
    grid_rows = seq // GRID_W
    n_blocks = grid_rows // Q_ROWS
    n_groups = grid_rows // KV_GROUP
    blk = Q_ROWS * GRID_W
    grp = KV_GROUP * GRID_W

    def group_spec(g):
        return pl.BlockSpec((grp, NA_WIDTH),
                            lambda b, i: (b * n_groups + jnp.clip(2 * i - 1, 0, n_groups - 4) + g, 0))

    q_spec = pl.BlockSpec((blk, NA_WIDTH), lambda b, i: (b * n_blocks + i, 0))
    c_spec = pl.BlockSpec((ctx_len, NA_WIDTH), lambda b, i: (b, 0))
    return pl.pallas_call(
        functools.partial(_na_kernel, grid_rows),
        grid=(batch, n_blocks),
        in_specs=[q_spec, q_spec] + [group_spec(g) for g in range(4)] + [group_spec(g) for g in range(4)]
                 + [c_spec, c_spec, pl.BlockSpec(bias_tab.shape, lambda b, i: (0, 0, 0, 0))],
        out_specs=q_spec,
        out_shape=jax.ShapeDtypeStruct((batch * seq, NA_WIDTH), BF16),
        scratch_shapes=[pltpu.VMEM((4 * grp, NA_WIDTH), BF16), pltpu.VMEM((4 * grp, NA_WIDTH), BF16)],
        compiler_params=_cparams("parallel", "arbitrary"),
        name="natten",
    )(q, qr, kr, kr, kr, kr, v, v, v, v, kc, vc, bias_tab)


def _block_diag(x_cat, bd_mask):
    xb = x_cat.astype(BF16)
    tiled = jnp.concatenate([xb] * (QUAD // CHUNK), axis=0)
    return jnp.where(bd_mask, tiled, jnp.zeros_like(tiled))


def _diag_blocks(x, head_of_lane):
    n = QUAD // HEAD_DIM
    out = x[(n - 1) * HEAD_DIM:n * HEAD_DIM]
    for h in range(n - 2, -1, -1):
        out = jnp.where(head_of_lane == h, x[h * HEAD_DIM:(h + 1) * HEAD_DIM], out)
    return out


def _rwkv_kernel(reverse, with_gate, n_ctx, n_blocks,
                 lat_ref, lat_prev_ref, lat_next_ref, ctx_ref, ctx_prev_ref, ctx_next_ref,
                 mup_ref, mun_ref, w0_ref, w2_ref, a0_ref, a2_ref, g2_ref,
                 kk_ref, ka_ref, rk_ref, ones_ref, *refs):
    if with_gate:
        acc_refs = (None, None)
        y_ref, bonus_ref, gate_ref = refs[0:3]
        refs = refs[3:]
    else:
        acc_refs = refs[0:2]
        y_ref, bonus_ref = refs[2:4]
        gate_ref = None
        refs = refs[4:]
    h_ref = refs[0]
    sets = (refs[1:7], refs[7:13])
    s = pl.program_id(1)

    @pl.when(s == 0)
    def _():
        h_ref[...] = jnp.zeros_like(h_ref)
        for ref in sets[1]:
            ref[...] = jnp.zeros_like(ref)

    @pl.when(s % 2 == 0)
    def _():
        _rwkv_step(reverse, n_ctx, n_blocks, s, sets[0], sets[1], lat_ref, lat_prev_ref, lat_next_ref,
                   ctx_ref, ctx_prev_ref, ctx_next_ref, mup_ref, mun_ref, w0_ref, w2_ref, a0_ref, a2_ref,
                   g2_ref, kk_ref, ka_ref, rk_ref, ones_ref, acc_refs, y_ref, bonus_ref, gate_ref, h_ref)

    @pl.when(s % 2 == 1)
    def _():
        _rwkv_step(reverse, n_ctx, n_blocks, s, sets[1], sets[0], lat_ref, lat_prev_ref, lat_next_ref,
                   ctx_ref, ctx_prev_ref, ctx_next_ref, mup_ref, mun_ref, w0_ref, w2_ref, a0_ref, a2_ref,
                   g2_ref, kk_ref, ka_ref, rk_ref, ones_ref, acc_refs, y_ref, bonus_ref, gate_ref, h_ref)


def _rwkv_step(reverse, n_ctx, n_blocks, s, wset, rset, lat_ref, lat_prev_ref, lat_next_ref,
               ctx_ref, ctx_prev_ref, ctx_next_ref, mup_ref, mun_ref, w0_ref, w2_ref, a0_ref, a2_ref,
               g2_ref, kk_ref, ka_ref, rk_ref, ones_ref, acc_refs, y_ref, bonus_ref, gate_ref, h_ref):
    sp = jnp.minimum(s, n_blocks - 1)
    if reverse:
        n = jnp.where(sp < n_ctx, n_ctx - 1 - sp, n_blocks + n_ctx - 1 - sp)
    else:
        n = sp

    rows = lat_ref.shape[0]
    n_ch = rows // CHUNK
    w = RW_WIDTH
    t = {}

    is_ctx = n < n_ctx
    has_prev = jnp.logical_and(n != 0, n != n_ctx)
    has_next = jnp.logical_and(n != n_ctx - 1, n != n_blocks - 1)
    row = lax.broadcasted_iota(jnp.int32, (rows, 1), 0)

    def shifted(c0, c1):
        p = jnp.where(is_ctx, ctx_ref[:, c0:c1], lat_ref[:, c0:c1])
        prow = jnp.where(is_ctx, ctx_prev_ref[HALO - 1:HALO, c0:c1], lat_prev_ref[HALO - 1:HALO, c0:c1])
        nrow = jnp.where(is_ctx, ctx_next_ref[0:1, c0:c1], lat_next_ref[0:1, c0:c1])
        prow = jnp.where(has_prev, prow, 0.0)
        nrow = jnp.where(has_next, nrow, 0.0)
        prev = jnp.where(row == 0, prow, pltpu.roll(p, 1, 0))
        nxt = jnp.where(row == rows - 1, nrow, pltpu.roll(p, rows - 1, 0))
        return p + mup_ref[:, c0:c1] * (prev - p) + mun_ref[:, c0:c1] * (nxt - p)

    def prep_lora():
        lora = shifted(3 * w, 3 * w + 3 * LORA_W)
        pw = lora[:, 0:LORA_W]
        pa = lora[:, LORA_W:2 * LORA_W]
        t["log_decay"] = -math.exp(-0.5) * jax.nn.sigmoid(
            w0_ref[...] + _dot(jnp.tanh(pw).astype(BF16), w2_ref[...]))
        t["a"] = jax.nn.sigmoid(a0_ref[...] + _dot(pa.astype(BF16), a2_ref[...]))
        if gate_ref is not None:
            gate_ref[...] = _dot(jax.nn.sigmoid(lora[:, 2 * LORA_W:3 * LORA_W]).astype(BF16), g2_ref[...])

    def prep_keys():
        k = shifted(w, 2 * w)
        ones = ones_ref[...]
        kk_raw = k * kk_ref[...]
        kk = kk_raw * lax.rsqrt(jnp.maximum(_dot_lhs_split(kk_raw * kk_raw, ones, 1), 1e-24))
        t.update(kk=kk, kd=k * (1.0 + (t["a"] - 1.0) * ka_ref[...]), bb=t["a"] * kk)

    def prep_bonus():
        r = shifted(0, w)
        v = shifted(2 * w, 3 * w)
        bonus = _dot_lhs_split(r * t["kd"] * rk_ref[...], ones_ref[...], 2) * v
        bonus_ref[...] = bonus if acc_refs[1] is None else bonus + acc_refs[1][...]
        t.update(r=r, v=v)

    def finish_prep():
        r, v, kk, kd, bb, log_decay = (t[name] for name in ("r", "v", "kk", "kd", "bb", "log_decay"))
        ti = lax.broadcasted_iota(jnp.int32, (rows, rows), 0)
        si = lax.broadcasted_iota(jnp.int32, (rows, rows), 1)
        same_chunk = (ti // CHUNK) == (si // CHUNK)
        tri = jnp.where(jnp.logical_and(same_chunk, (si >= ti) if reverse else (si <= ti)), 1.0, 0.0).astype(BF16)
        lw_hi, lw_lo = _split(log_decay, 2)
        cum = _dot(tri, lw_hi) + _dot(tri, lw_lo)
        e_neg = jnp.exp(-cum)
        last = 0 if reverse else CHUNK - 1
        w_v, w_kkt, w_rt, w_kh, w_bh, w_et = wset
        w_v[...] = v
        w_kkt[...] = kk * jnp.exp(cum - log_decay)
        w_rt[...] = r * jnp.exp(cum)
        w_kh[...] = kd * e_neg
        w_bh[...] = bb * e_neg
        for c in range(n_ch):
            w_et[c:c + 1, :] = jnp.exp(cum[c * CHUNK + last:c * CHUNK + last + 1, :])

    r_v, r_kkt, r_rt, r_kh, r_bh, r_et = rset
    trow = lax.broadcasted_iota(jnp.int32, (CHUNK, QUAD), 0)
    tcol = lax.broadcasted_iota(jnp.int32, (CHUNK, QUAD), 1) % CHUNK
    strict = (trow < tcol) if reverse else (trow > tcol)
    incl = (trow <= tcol) if reverse else (trow >= tcol)
    on_diag = trow == tcol
    eye_cat = jnp.where(on_diag, 1.0, 0.0)
    brow = lax.broadcasted_iota(jnp.int32, (QUAD, QUAD), 0)
    bcol = lax.broadcasted_iota(jnp.int32, (QUAD, QUAD), 1)
    bd_mask = (brow // HEAD_DIM) == (bcol // HEAD_DIM)
    head_of_lane = lax.broadcasted_iota(jnp.int32, (CHUNK, QUAD), 1) // HEAD_DIM
    n_doublings = int(math.log2(CHUNK)) - 1
    heads = range(QUAD // HEAD_DIM)
    quads = range(RW_WIDTH // QUAD)
    order = list(range(n_ch - 1, -1, -1) if reverse else range(n_ch))

    chains = [(c, qd) for c in order for qd in quads]

    def rsl(c):
        return slice(c * CHUNK, (c + 1) * CHUNK)

    def lsl(qd):
        return slice(qd * QUAD, (qd + 1) * QUAD)

    v_q = {ch: r_v[rsl(ch[0]), lsl(ch[1])] for ch in chains}
    kkt_q = {ch: r_kkt[rsl(ch[0]), lsl(ch[1])] for ch in chains}
    rt_q = {ch: r_rt[rsl(ch[0]), lsl(ch[1])] for ch in chains}

    a_k, a_b, b_k, b_b = {}, {}, {}, {}
    for ch in chains:
        kh = r_kh[rsl(ch[0]), lsl(ch[1])].astype(BF16)
        bh = r_bh[rsl(ch[0]), lsl(ch[1])].astype(BF16)
        zero = jnp.zeros_like(kh)
        rhs = jnp.concatenate([jnp.where(head_of_lane == h, kh, zero) for h in heads]
                              + [jnp.where(head_of_lane == h, bh, zero) for h in heads], axis=0)
        lhs = jnp.concatenate([kkt_q[ch], rt_q[ch]], axis=0).astype(BF16)
        gram = _dot_nt(lhs, rhs)
        a_k[ch] = jnp.where(strict, gram[0:CHUNK, 0:QUAD], 0.0)
        a_b[ch] = jnp.where(strict, gram[0:CHUNK, QUAD:2 * QUAD], 0.0)
        b_k[ch] = jnp.where(incl, gram[CHUNK:2 * CHUNK, 0:QUAD], 0.0)
        b_b[ch] = jnp.where(incl, gram[CHUNK:2 * CHUNK, QUAD:2 * QUAD], 0.0).astype(BF16)

    akv, bkv = {}, {}
    for ch in chains:
        both = _dot(jnp.concatenate([a_k[ch], b_k[ch]], axis=0).astype(BF16), _block_diag(v_q[ch], bd_mask))
        akv[ch] = both[0:CHUNK]
        bkv[ch] = both[CHUNK:2 * CHUNK]

    m = {ch: -a_b[ch] for ch in chains}
    t_inv = {ch: eye_cat + m[ch] for ch in chains}
    for ch in chains:
        m[ch] = _dot(m[ch].astype(BF16), _block_diag(m[ch], bd_mask))
    prep_at = {0: prep_lora, 1: prep_keys, 2: prep_bonus, 4: finish_prep}
    for step in range(n_doublings):
        final = step == n_doublings - 1
        if step in prep_at:
            prep_at[step]()
        for ch in chains:
            m_bd = _block_diag(m[ch], bd_mask)
            if final:
                t_inv[ch] = t_inv[ch] + _dot(t_inv[ch].astype(BF16), m_bd)
            else:
                both = _dot(jnp.concatenate([m[ch], t_inv[ch]], axis=0).astype(BF16), m_bd)
                m[ch] = both[0:CHUNK]
                t_inv[ch] = t_inv[ch] + both[CHUNK:2 * CHUNK]

    p1, tkk = {}, {}
    for ch in chains:
        t_b = t_inv[ch].astype(BF16)
        p1[ch] = _dot(t_b, _block_diag(akv[ch], bd_mask))
        tkk[ch] = _dot(t_b, _block_diag(kkt_q[ch], bd_mask))

    y0, q_t, h0_cat, g_cat = {}, {}, {}, {}
    for ch in chains:
        c, qd = ch
        y0[ch] = bkv[ch] - _dot(b_b[ch], _block_diag(p1[ch], bd_mask))
        q_t[ch] = rt_q[ch] - _dot(b_b[ch], _block_diag(tkk[ch], bd_mask))
        e_q = r_et[c:c + 1, lsl(qd)]
        lhs_t = jnp.concatenate([r_kh[rsl(c), lsl(qd)] * e_q, r_bh[rsl(c), lsl(qd)] * e_q], axis=0).astype(BF16)
        rhs_t = jnp.concatenate([jnp.concatenate([v_q[ch], jnp.zeros_like(v_q[ch])], axis=1),
                                 jnp.concatenate([-p1[ch], -tkk[ch]], axis=1)], axis=0).astype(BF16)
        hg = _dot_tn(lhs_t, rhs_t)
        h0_cat[ch] = _diag_blocks(hg[:, 0:QUAD], head_of_lane)
        g_cat[ch] = _diag_blocks(hg[:, QUAD:2 * QUAD], head_of_lane) + jnp.where(on_diag, e_q, 0.0)

    h = {qd: h_ref[qd] for qd in quads}
    for c in order:
        for qd in quads:
            ch = (c, qd)
            o2 = _dot(jnp.concatenate([g_cat[ch], q_t[ch]], axis=0).astype(BF16), _block_diag(h[qd], bd_mask))
            h[qd] = o2[0:CHUNK] + h0_cat[ch]
            y = y0[ch] + o2[CHUNK:2 * CHUNK]
            y_ref[rsl(c), lsl(qd)] = y if acc_refs[0] is None else y + acc_refs[0][rsl(c), lsl(qd)]
    for qd in quads:
        h_ref[qd] = h[qd]


def _rwkv_call(rw_lat, rw_ctx, weights, *, reverse, with_gate, batch, seq, ctx_len, acc=None):
    assert CHUNK == HEAD_DIM
    blk_rows = RW_BLOCK_CHUNKS * CHUNK
    assert ctx_len % blk_rows == 0 and seq % blk_rows == 0
    n_ctx = ctx_len // blk_rows
    n_lat = seq // blk_rows
    n_chunks = n_ctx + n_lat
    halo_per_blk = blk_rows // HALO

    def block_of(s):
        if reverse:
            return jnp.where(s < n_ctx, n_ctx - 1 - s, n_chunks + n_ctx - 1 - s)
        return s

    def stream_specs(first, count):
        n_halo = batch * count * halo_per_blk

        def blk(b, s):
            return b * count + jnp.clip(block_of(jnp.minimum(s, n_chunks - 1)) - first, 0, count - 1)

        return [pl.BlockSpec((blk_rows, RW_COLS), lambda b, s: (blk(b, s), 0)),
                pl.BlockSpec((HALO, RW_COLS), lambda b, s: (jnp.maximum(blk(b, s) * halo_per_blk - 1, 0), 0)),
                pl.BlockSpec((HALO, RW_COLS),
                             lambda b, s: (jnp.minimum((blk(b, s) + 1) * halo_per_blk, n_halo - 1), 0))]

    def out_block(b, step):
        lat = jnp.maximum(step, n_ctx) - n_ctx
        return (b * n_lat + (n_lat - 1 - lat if reverse else lat), 0)

    def const_spec(a):
        return pl.BlockSpec(a.shape, lambda b, s: (0,) * a.ndim)

    prep_spec = pl.BlockSpec((blk_rows, RW_WIDTH), lambda b, s: out_block(b, jnp.minimum(s, n_chunks - 1)))
    y_spec = pl.BlockSpec((blk_rows, RW_WIDTH), lambda b, s: out_block(b, jnp.maximum(s - 1, 0)))
    out_shape = jax.ShapeDtypeStruct((batch * seq, RW_WIDTH), F32)
    n_out = 3 if with_gate else 2
    assert with_gate == (acc is None)
    acc_specs = [] if acc is None else [y_spec, prep_spec]
    term_set = [pltpu.VMEM((blk_rows, RW_WIDTH), F32)] * 5 + [pltpu.VMEM((8, RW_WIDTH), F32)]
    return pl.pallas_call(
        functools.partial(_rwkv_kernel, reverse, with_gate, n_ctx, n_chunks),
        grid=(batch, n_chunks + 1),
        in_specs=stream_specs(n_ctx, n_lat) + stream_specs(0, n_ctx) + [const_spec(a) for a in weights] + acc_specs,
        out_specs=[y_spec] + [prep_spec] * (n_out - 1),
        out_shape=[out_shape] * n_out,
        scratch_shapes=[pltpu.VMEM((RW_WIDTH // QUAD, CHUNK, QUAD), F32)] + term_set + term_set,
        compiler_params=_cparams("parallel", "arbitrary"),
        name="rwkv_bwd" if reverse else "rwkv_fwd",
    )(rw_lat, rw_lat, rw_lat, rw_ctx, rw_ctx, rw_ctx, *weights, *(acc or ()))


def _tail_kernel(d_model, alpha, ff_chunk, yna_ref, y_ref, bonus_ref, gate_ref, x_ref, moda_ref, modb_ref,
                 gng_ref, gnb_ref, ones_ref, wo_ref, ln1g_ref, ln1b_ref, w1_ref, w2_ref, ln2g_ref, ln2b_ref,
                 o_ref, x1_a, xm_a, x1_b, xm_b):
    i = pl.program_id(0)

    @pl.when(i == 0)
    def _():
        x1_b[...] = jnp.zeros_like(x1_b)
        xm_b[...] = jnp.zeros_like(xm_b)

    def step(x1_w, xm_w, x1_r, xm_r):
        ones = ones_ref[...]
        inv_n = 1.0 / HEAD_DIM
        n_chunks = w1_ref.shape[1] // ff_chunk
        t = {}

        def readout_stats():
            y = y_ref[...]
            mu = _dot_lhs_split(y, ones, 2) * inv_n
            yc = y - mu
            t.update(yc=yc, var=_dot_lhs_split(yc * yc, ones, 2) * inv_n)

        def out_projection():
            yn = t["yc"] * lax.rsqrt(t["var"] + GN_EPS) * gng_ref[...] + gnb_ref[...]
            y_rw = ((yn + bonus_ref[...]) * gate_ref[...]).astype(BF16)
            t["proj"] = (_dot(yna_ref[...], wo_ref[0:NA_WIDTH, :])
                         + _dot(y_rw, wo_ref[NA_WIDTH:NA_WIDTH + RW_WIDTH, :]))

        def norm_modulate():
            mod = moda_ref[...]
            g1 = mod[:, 2 * d_model:3 * d_model]
            shift = mod[:, 3 * d_model:4 * d_model]
            scale = mod[:, 4 * d_model:5 * d_model]
            x1 = _normalize(alpha * x_ref[...] + g1 * t["proj"], LN_EPS) * ln1g_ref[...] + ln1b_ref[...]
            x1_w[...] = x1
            xm_w[...] = (_normalize(x1, LN_EPS) * (1.0 + scale) + shift).astype(BF16)

        stage_a = [(min(k, n_chunks - 1), stage) for k, stage in
                   enumerate((readout_stats, out_projection, norm_modulate))]
        xm = xm_r[...]
        acc = None
        for c in range(n_chunks):
            c0 = c * ff_chunk
            hid = jnp.maximum(_dot(xm, w1_ref[:, c0:c0 + ff_chunk]), 0.0)
            part = _dot((hid * hid).astype(BF16), w2_ref[c0:c0 + ff_chunk, :])
            acc = part if acc is None else acc + part
            for after, stage in stage_a:
                if after == c:
                    stage()
        g2 = modb_ref[...][:, 5 * d_model:6 * d_model]
        o_ref[...] = _normalize(alpha * x1_r[...] + g2 * acc, LN_EPS) * ln2g_ref[...] + ln2b_ref[...]

    @pl.when(i % 2 == 0)
    def _():
        step(x1_a, xm_a, x1_b, xm_b)

    @pl.when(i % 2 == 1)
    def _():
        step(x1_b, xm_b, x1_a, xm_a)


def _tail_call(y_na, y, bonus, gate, x2d, mod3, consts, *, tm, seq, alpha):
    rows, d = x2d.shape
    tiles_per_batch = seq // tm
    n_tiles = rows // tm

    def tile_a(i):
        return jnp.minimum(i, n_tiles - 1)

    def tile_b(i):
        return jnp.maximum(i - 1, 0)

    half = pl.BlockSpec((tm, RW_WIDTH), lambda i: (tile_a(i), 0))

    def resident(a):
        return pl.BlockSpec(a.shape, lambda i: (0,) * a.ndim, pipeline_mode=pl.Buffered(1))

    return pl.pallas_call(
        functools.partial(_tail_kernel, d, alpha, 1024),
        grid=(n_tiles + 1,),
        in_specs=[half] * 4 + [pl.BlockSpec((tm, d), lambda i: (tile_a(i), 0)),
                               pl.BlockSpec((None, 1, mod3.shape[2]), lambda i: (tile_a(i) // tiles_per_batch, 0, 0)),
                               pl.BlockSpec((None, 1, mod3.shape[2]), lambda i: (tile_b(i) // tiles_per_batch, 0, 0))]
                 + [resident(a) for a in consts],
        out_specs=pl.BlockSpec((tm, d), lambda i: (tile_b(i), 0)),
        out_shape=jax.ShapeDtypeStruct((rows, d), F32),
        scratch_shapes=[pltpu.VMEM((tm, d), F32), pltpu.VMEM((tm, d), BF16)] * 2,
        compiler_params=_cparams("arbitrary"),
        name="tail",
    )(y_na, y, bonus, gate, x2d, mod3, mod3, *consts)


def _rope_tables(seq):
    f = HEAD_DIM // 4
    t = np.arange(seq)
    row = (t // GRID_W).astype(np.float32)
    col = (t % GRID_W).astype(np.float32)
    inv = (ROPE_BASE ** (-np.arange(f, dtype=np.float32) / f)).astype(np.float32)
    ang_r = row[:, None] * inv[None, :]
    ang_c = col[:, None] * inv[None, :]
    zero = np.zeros_like(ang_r)
    cos = np.concatenate([np.cos(ang_r), np.cos(ang_r), np.cos(ang_c), np.cos(ang_c)], axis=1)
    sa = np.concatenate([-np.sin(ang_r), zero, -np.sin(ang_c), zero], axis=1)
    sb = np.concatenate([zero, np.sin(ang_r), zero, np.sin(ang_c)], axis=1)
    return tuple(jnp.asarray(np.concatenate([z, z], axis=1), F32) for z in (cos, sa, sb))


def _bias_table(rpb):
    qc = np.arange(GRID_W)[:, None]
    kc = np.arange(GRID_W)[None, :]
    c0 = np.clip(qc - WIN_W // 2, 0, GRID_W - WIN_W)
    inside = (kc >= c0) & (kc < c0 + WIN_W)
    pad = GRID_W - WIN_W
    padded = jnp.pad(rpb.astype(F32) * LOG2_E, ((0, 0), (0, 0), (pad, pad)))
    g = jnp.stack([padded[:, :, GRID_W - 1 - q:2 * GRID_W - 1 - q] for q in range(GRID_W)], axis=2)
    g = jnp.where(inside[None, None], g, NEG_BIAS)
    g = g.transpose(1, 0, 2, 3)
    return jnp.concatenate([g[:-1], g[1:]], axis=-1)


def _pad_lora(w, d):
    z = jnp.zeros_like(w[d])
    return jnp.concatenate([w[0] if d == 0 else z, w[1] if d == 1 else z], axis=0)


def kernel(x, c, ctx, c_ctx, w_mod, b_mod, w_in, w_out, ln1_g, ln1_b, mlp_w1, mlp_w2, ln2_g, ln2_b, na_rpb,
           rw_mu_prev, rw_mu_next, rw_w0, rw_w2, rw_a0, rw_a2, rw_g2, rw_k_k, rw_k_a, rw_r_k, rw_gn_g, rw_gn_b):
    depth = w_mod.shape[0]
    assert depth == 1, "single-layer trunk only (the context stream is never updated)"
    batch, seq, d = x.shape
    ctx_len = ctx.shape[1]
    alpha = (2 * depth) ** 0.25
    tm = ROW_TILE
    tm_ctx = math.gcd(ctx_len, ROW_TILE)
    assert seq % (Q_ROWS * GRID_W) == 0 and seq % tm == 0 and tm_ctx % 8 == 0
    assert batch + 1 <= 8

    cc = jnp.zeros((8, d), F32).at[:batch].set(c).at[batch].set(c_ctx)
    mod3 = _mod_call(cc, w_mod[0], b_mod[0]).reshape(8, 1, N_MOD * d)

    w_in_b = w_in[0].astype(BF16)
    tiles_lat = seq // tm
    x2d = x.reshape(batch * seq, d)
    q, qr, kr, v, rw_lat = _inproj_call(
        x2d, mod3, _rope_tables(seq), w_in_b, tm=tm,
        mod_index=lambda i: i // tiles_lat, table_index=lambda i: i % tiles_lat, name="inproj")
    ident = tuple(jnp.full((tm_ctx, 2 * HEAD_DIM), val, F32) for val in (1.0, 0.0, 0.0))
    _, _, kc, vc, rw_ctx = _inproj_call(
        ctx.reshape(batch * ctx_len, d), mod3, ident, w_in_b, tm=tm_ctx,
        mod_index=lambda i: batch, table_index=lambda i: 0, name="inproj_ctx")

    y_na = _na_call(q, qr, kr, v, kc, vc, _bias_table(na_rpb[0]), batch=batch, seq=seq, ctx_len=ctx_len)

    lane_head = jnp.arange(RW_WIDTH) // HEAD_DIM
    ones = (lane_head[:, None] == lane_head[None, :]).astype(BF16)
    row = lambda a: a.reshape(1, -1).astype(F32)
    def dir_weights(dirn):
        return (row(rw_mu_prev[0]), row(rw_mu_next[0]), row(rw_w0[0, dirn]),
                _pad_lora(rw_w2[0], dirn).astype(BF16), row(rw_a0[0, dirn]),
                _pad_lora(rw_a2[0], dirn).astype(BF16), rw_g2[0].astype(BF16),
                row(rw_k_k[0]), row(rw_k_a[0]), row(rw_r_k[0]), ones)

    y_f, bonus_f, gate = _rwkv_call(rw_lat, rw_ctx, dir_weights(0), reverse=False, with_gate=True,
                                    batch=batch, seq=seq, ctx_len=ctx_len)
    y_sum, bonus_sum = _rwkv_call(rw_lat, rw_ctx, dir_weights(1), reverse=True, with_gate=False,
                                  batch=batch, seq=seq, ctx_len=ctx_len, acc=(y_f, bonus_f))

    consts = (row(rw_gn_g[0]), row(rw_gn_b[0]), ones, w_out[0].astype(BF16), row(ln1_g[0]), row(ln1_b[0]),
              mlp_w1[0].astype(BF16), mlp_w2[0].astype(BF16), row(ln2_g[0]), row(ln2_b[0]))
    out = _tail_call(y_na, y_sum, bonus_sum, gate, x2d, mod3, consts, tm=tm, seq=seq, alpha=alpha)
    return out.reshape(batch, seq, d)
```

```python
import functools
import math

import jax
import jax.numpy as jnp
import numpy as np
from jax import lax
from jax.experimental import pallas as pl
from jax.experimental.pallas import tpu as pltpu

HEAD_DIM = 64
NA_HEADS = 8
RW_HEADS = 8
NA_WIDTH = NA_HEADS * HEAD_DIM
RW_WIDTH = RW_HEADS * HEAD_DIM
GRID_W = 64
WIN_H = 8
WIN_W = 16
ROPE_BASE = 10000.0
N_DIR = 2
DECAY_LORA = 64
AAA_LORA = 64
GATE_LORA = 128
LORA_W = N_DIR * DECAY_LORA
NA_COLS = 3 * NA_WIDTH
RW_COLS = 3 * RW_WIDTH + 3 * LORA_W
N_MOD = 6
LN_EPS = 1e-6
GN_EPS = 64e-5
NEG_BIAS = -1e30

CHUNK = 64
RW_BLOCK_CHUNKS = 4
QUAD = 2 * HEAD_DIM
HALO = 8
LOG2_E = math.log2(math.e)
Q_ROWS = 8
NA_UNROLL = 2
KV_GROUP = 4
ROW_TILE = 512
VMEM_LIMIT = 56 * 1024 * 1024

F32 = jnp.float32
BF16 = jnp.bfloat16


def _cparams(*sem):
    return pltpu.CompilerParams(dimension_semantics=sem, vmem_limit_bytes=VMEM_LIMIT)


def _dot(a, b):
    return jnp.dot(a, b, preferred_element_type=F32)


def _dot_nt(a, b):
    return lax.dot_general(a, b, (((1,), (1,)), ((), ())), preferred_element_type=F32)


def _dot_tn(a, b):
    return lax.dot_general(a, b, (((0,), (0,)), ((), ())), preferred_element_type=F32)


def _split(x, pieces):
    out = []
    for _ in range(pieces):
        p = x.astype(BF16)
        out.append(p)
        x = x - p.astype(F32)
    return out


def _dot_lhs_split(x, w_bf16, pieces):
    acc = None
    for p in _split(x, pieces):
        t = _dot(p, w_bf16)
        acc = t if acc is None else acc + t
    return acc


def _dot_x3(a, b):
    a_hi, a_lo = _split(a, 2)
    b_hi, b_lo = _split(b, 2)
    return _dot(a_hi, b_hi) + (_dot(a_lo, b_hi) + _dot(a_hi, b_lo))


def _normalize(x, eps):
    mu = jnp.mean(x, axis=-1, keepdims=True)
    xc = x - mu
    var = jnp.mean(xc * xc, axis=-1, keepdims=True)
    return xc * lax.rsqrt(var + eps)


def _mod_kernel(c_ref, w_ref, b_ref, o_ref):
    c = c_ref[...]
    s = c * jax.nn.sigmoid(c)
    o_ref[...] = _dot(s.astype(BF16), w_ref[...].astype(BF16)) + b_ref[...]


def _mod_call(cc, w_mod, b_mod):
    rows, d = cc.shape
    n = w_mod.shape[1]
    tn = 512
    return pl.pallas_call(
        _mod_kernel,
        grid=(n // tn,),
        in_specs=[pl.BlockSpec((rows, d), lambda j: (0, 0)),
                  pl.BlockSpec((d, tn), lambda j: (0, j)),
                  pl.BlockSpec((1, tn), lambda j: (0, j))],
        out_specs=pl.BlockSpec((rows, tn), lambda j: (0, j)),
        out_shape=jax.ShapeDtypeStruct((rows, n), F32),
        compiler_params=_cparams("arbitrary"),
        name="mod",
    )(cc, w_mod, b_mod.reshape(1, n))


def _inproj_kernel(d_model, x_ref, mod_ref, cos_ref, sa_ref, sb_ref, w_ref, q_ref, qr_ref, kr_ref, v_ref, rw_ref):
    x = x_ref[...]
    mod = mod_ref[...]
    shift = mod[:, 0:d_model]
    scale = mod[:, d_model:2 * d_model]
    xm = (_normalize(x, LN_EPS) * (1.0 + scale) + shift).astype(BF16)

    reps = NA_WIDTH // cos_ref.shape[1]
    cos = jnp.concatenate([cos_ref[...]] * reps, axis=1)
    sa = jnp.concatenate([sa_ref[...]] * reps, axis=1)
    sb = jnp.concatenate([sb_ref[...]] * reps, axis=1)

    def rope(z):
        up = pltpu.roll(z, NA_WIDTH - HEAD_DIM // 4, 1)
        down = pltpu.roll(z, HEAD_DIM // 4, 1)
        return z * cos + up * sa + down * sb

    qk_scale = HEAD_DIM ** -0.5 * LOG2_E
    q = _dot(xm, w_ref[:, 0:NA_WIDTH]) * qk_scale
    q_ref[...] = q.astype(BF16)
    qr_ref[...] = rope(q).astype(BF16)
    k = _dot(xm, w_ref[:, NA_WIDTH:2 * NA_WIDTH])
    kr_ref[...] = rope(k).astype(BF16)
    v_ref[...] = _dot(xm, w_ref[:, 2 * NA_WIDTH:NA_COLS]).astype(BF16)
    rw_ref[...] = _dot(xm, w_ref[:, NA_COLS:NA_COLS + RW_COLS])


def _inproj_call(x2d, mod3, tables, w_in_bf16, *, tm, mod_index, table_index, name):
    rows, d = x2d.shape
    cos, sa, sb = tables
    tw = cos.shape[1]
    ncols = w_in_bf16.shape[1]
    na_spec = pl.BlockSpec((tm, NA_WIDTH), lambda i: (i, 0))
    table_spec = pl.BlockSpec((tm, tw), lambda i: (table_index(i), 0))
    na_shape = jax.ShapeDtypeStruct((rows, NA_WIDTH), BF16)
    return pl.pallas_call(
        functools.partial(_inproj_kernel, d),
        grid=(rows // tm,),
        in_specs=[pl.BlockSpec((tm, d), lambda i: (i, 0)),
                  pl.BlockSpec((None, 1, mod3.shape[2]), lambda i: (mod_index(i), 0, 0)),
                  table_spec, table_spec, table_spec,
                  pl.BlockSpec((d, ncols), lambda i: (0, 0))],
        out_specs=[na_spec, na_spec, na_spec, na_spec, pl.BlockSpec((tm, RW_COLS), lambda i: (i, 0))],
        out_shape=[na_shape, na_shape, na_shape, na_shape, jax.ShapeDtypeStruct((rows, RW_COLS), F32)],
        compiler_params=_cparams("parallel"),
        name=name,
    )(x2d, mod3, cos, sa, sb, w_in_bf16)


def _na_kernel(grid_rows, q_ref, qr_ref, k0, k1, k2, k3, v0, v1, v2, v3, kc_ref, vc_ref, bias_ref,
               o_ref, kwin, vwin):
    i = pl.program_id(1)
    grp = KV_GROUP * GRID_W
    for g, (kr, vr) in enumerate(((k0, v0), (k1, v1), (k2, v2), (k3, v3))):
        kwin[g * grp:(g + 1) * grp, :] = kr[...]
        vwin[g * grp:(g + 1) * grp, :] = vr[...]
    n_groups = grid_rows // KV_GROUP
    win_row0 = KV_GROUP * jnp.clip(2 * i - 1, 0, n_groups - 4)
    win_keys = WIN_H * GRID_W
    lane = lax.broadcasted_iota(jnp.int32, (GRID_W, 2 * HEAD_DIM), 1)
    even = lane < HEAD_DIM

    pairs = range(NA_HEADS // 2)
    lanes = [slice(pair * 2 * HEAD_DIM, (pair + 1) * 2 * HEAD_DIM) for pair in pairs]

    def rows_body(jj, carry):
        units = []
        for u in range(NA_UNROLL):
            j = NA_UNROLL * jj + u
            irow = Q_ROWS * i + j
            r0 = jnp.clip(irow - WIN_H // 2, 0, grid_rows - WIN_H)
            koff = pl.multiple_of((r0 - win_row0) * GRID_W, GRID_W)
            brow = (WIN_H - 1) - (irow - r0)
            qoff = pl.multiple_of(j * GRID_W, GRID_W)
            units += [(koff, brow, qoff, pair) for pair in pairs]
        def split_heads(z):
            zero = jnp.zeros_like(z)
            return jnp.concatenate([jnp.where(even, z, zero), jnp.where(even, zero, z)], axis=0)

        two = 2 * GRID_W
        s_loc = []
        for koff, brow, qoff, pair in units:
            qr2 = split_heads(qr_ref[pl.ds(qoff, GRID_W), lanes[pair]])
            bias = jnp.concatenate(
                [jnp.concatenate([bias_ref[brow + 2 * m, 2 * pair], bias_ref[brow + 2 * m, 2 * pair + 1]], axis=0)
                 for m in range(WIN_H // 2)], axis=1)
            s_loc.append(_dot_nt(qr2, kwin[pl.ds(koff, win_keys), lanes[pair]]) + bias)
        s_ctx = [None] * len(units)
        for pair in pairs:
            ns = [n for n, unit in enumerate(units) if unit[3] == pair]
            qp_all = jnp.concatenate([split_heads(q_ref[pl.ds(units[n][2], GRID_W), lanes[pair]]) for n in ns],
                                     axis=0)
            sc = _dot_nt(qp_all, kc_ref[:, lanes[pair]])
            for k, n in enumerate(ns):
                s_ctx[n] = sc[k * two:(k + 1) * two]
        p_loc, p_ctx, denom = [], [], []
        for n in range(len(units)):
            mx = jnp.maximum(jnp.max(s_loc[n], axis=-1, keepdims=True), jnp.max(s_ctx[n], axis=-1, keepdims=True))
            pl_ = jnp.exp2(s_loc[n] - mx)
            pc_ = jnp.exp2(s_ctx[n] - mx)
            denom.append(jnp.sum(pl_, axis=-1, keepdims=True) + jnp.sum(pc_, axis=-1, keepdims=True))
            p_loc.append(pl_.astype(BF16))
            p_ctx.append(pc_.astype(BF16))
        o_ctx = [None] * len(units)
        for pair in pairs:
            ns = [n for n, unit in enumerate(units) if unit[3] == pair]
            oc = _dot(jnp.concatenate([p_ctx[n] for n in ns], axis=0), vc_ref[:, lanes[pair]])
            for k, n in enumerate(ns):
                o_ctx[n] = oc[k * two:(k + 1) * two]
        for n, (koff, brow, qoff, pair) in enumerate(units):
            o = (_dot(p_loc[n], vwin[pl.ds(koff, win_keys), lanes[pair]]) + o_ctx[n]) / denom[n]
            o_ref[pl.ds(qoff, GRID_W), lanes[pair]] = jnp.where(
                even, o[0:GRID_W], o[GRID_W:2 * GRID_W]).astype(BF16)
        return carry

    lax.fori_loop(0, Q_ROWS // NA_UNROLL, rows_body, 0)


def _na_call(q, qr, kr, v, kc, vc, bias_tab, *, batch, seq, ctx_len):
    grid_rows = seq // GRID_W
    n_blocks = grid_rows // Q_ROWS
    n_groups = grid_rows // KV_GROUP
    blk = Q_ROWS * GRID_W
    grp = KV_GROUP * GRID_W

    def group_spec(g):
        return pl.BlockSpec((grp, NA_WIDTH),
                            lambda b, i: (b * n_groups + jnp.clip(2 * i - 1, 0, n_groups - 4) + g, 0))

    q_spec = pl.BlockSpec((blk, NA_WIDTH), lambda b, i: (b * n_blocks + i, 0))
    c_spec = pl.BlockSpec((ctx_len, NA_WIDTH), lambda b, i: (b, 0))
    return pl.pallas_call(
        functools.partial(_na_kernel, grid_rows),
        grid=(batch, n_blocks),
        in_specs=[q_spec, q_spec] + [group_spec(g) for g in range(4)] + [group_spec(g) for g in range(4)]
                 + [c_spec, c_spec, pl.BlockSpec(bias_tab.shape, lambda b, i: (0, 0, 0, 0))],
        out_specs=q_spec,
        out_shape=jax.ShapeDtypeStruct((batch * seq, NA_WIDTH), BF16),
        scratch_shapes=[pltpu.VMEM((4 * grp, NA_WIDTH), BF16), pltpu.VMEM((4 * grp, NA_WIDTH), BF16)],
        compiler_params=_cparams("parallel", "arbitrary"),
        name="natten",
    )(q, qr, kr, kr, kr, kr, v, v, v, v, kc, vc, bias_tab)


def _block_diag(x_cat, bd_mask):
    xb = x_cat.astype(BF16)
    tiled = jnp.concatenate([xb] * (QUAD // CHUNK), axis=0)
    return jnp.where(bd_mask, tiled, jnp.zeros_like(tiled))


def _diag_blocks(x, head_of_lane):
    n = QUAD // HEAD_DIM
    out = x[(n - 1) * HEAD_DIM:n * HEAD_DIM]
    for h in range(n - 2, -1, -1):
        out = jnp.where(head_of_lane == h, x[h * HEAD_DIM:(h + 1) * HEAD_DIM], out)
    return out


def _rwkv_kernel(reverse, with_gate, n_ctx, n_blocks,
                 lat_ref, lat_prev_ref, lat_next_ref, ctx_ref, ctx_prev_ref, ctx_next_ref,
                 mup_ref, mun_ref, w0_ref, w2_ref, a0_ref, a2_ref, g2_ref,
                 kk_ref, ka_ref, rk_ref, ones_ref, *refs):
    if with_gate:
        acc_refs = (None, None)
        y_ref, bonus_ref, gate_ref = refs[0:3]
        refs = refs[3:]
    else:
        acc_refs = refs[0:2]
        y_ref, bonus_ref = refs[2:4]
        gate_ref = None
        refs = refs[4:]
    h_ref = refs[0]
    sets = (refs[1:7], refs[7:13])
    psets = (refs[13:17], refs[17:21])
    s = pl.program_id(1)

    @pl.when(s == 0)
    def _():
        h_ref[...] = jnp.zeros_like(h_ref)
        for ref in sets[1] + psets[1]:
            ref[...] = jnp.zeros_like(ref)

    @pl.when(s % 2 == 0)
    def _():
        _rwkv_step(reverse, n_ctx, n_blocks, s, sets[0], sets[1], psets[0], psets[1],
                   lat_ref, lat_prev_ref, lat_next_ref,
                   ctx_ref, ctx_prev_ref, ctx_next_ref, mup_ref, mun_ref, w0_ref, w2_ref, a0_ref, a2_ref,
                   g2_ref, kk_ref, ka_ref, rk_ref, ones_ref, acc_refs, y_ref, bonus_ref, gate_ref, h_ref)

    @pl.when(s % 2 == 1)
    def _():
        _rwkv_step(reverse, n_ctx, n_blocks, s, sets[1], sets[0], psets[1], psets[0],
                   lat_ref, lat_prev_ref, lat_next_ref,
                   ctx_ref, ctx_prev_ref, ctx_next_ref, mup_ref, mun_ref, w0_ref, w2_ref, a0_ref, a2_ref,
                   g2_ref, kk_ref, ka_ref, rk_ref, ones_ref, acc_refs, y_ref, bonus_ref, gate_ref, h_ref)


def _rwkv_step(reverse, n_ctx, n_blocks, s, wset, rset, pwset, prset, lat_ref, lat_prev_ref, lat_next_ref,
               ctx_ref, ctx_prev_ref, ctx_next_ref, mup_ref, mun_ref, w0_ref, w2_ref, a0_ref, a2_ref,
               g2_ref, kk_ref, ka_ref, rk_ref, ones_ref, acc_refs, y_ref, bonus_ref, gate_ref, h_ref):
    sp = jnp.minimum(s, n_blocks - 1)
    if reverse:
        n = jnp.where(sp < n_ctx, n_ctx - 1 - sp, n_blocks + n_ctx - 1 - sp)
    else:
        n = sp

    rows = lat_ref.shape[0]
    n_ch = rows // CHUNK
    w = RW_WIDTH
    t = {}

    is_ctx = n < n_ctx
    has_prev = jnp.logical_and(n != 0, n != n_ctx)
    has_next = jnp.logical_and(n != n_ctx - 1, n != n_blocks - 1)
    row = lax.broadcasted_iota(jnp.int32, (rows, 1), 0)

    def shifted(c0, c1):
        p = jnp.where(is_ctx, ctx_ref[:, c0:c1], lat_ref[:, c0:c1])
        prow = jnp.where(is_ctx, ctx_prev_ref[HALO - 1:HALO, c0:c1], lat_prev_ref[HALO - 1:HALO, c0:c1])
        nrow = jnp.where(is_ctx, ctx_next_ref[0:1, c0:c1], lat_next_ref[0:1, c0:c1])
        prow = jnp.where(has_prev, prow, 0.0)
        nrow = jnp.where(has_next, nrow, 0.0)
        prev = jnp.where(row == 0, prow, pltpu.roll(p, 1, 0))
        nxt = jnp.where(row == rows - 1, nrow, pltpu.roll(p, rows - 1, 0))
        return p + mup_ref[:, c0:c1] * (prev - p) + mun_ref[:, c0:c1] * (nxt - p)

    def prep_lora():
        lora = shifted(3 * w, 3 * w + 3 * LORA_W)
        pw = lora[:, 0:LORA_W]
        pa = lora[:, LORA_W:2 * LORA_W]
        t["log_decay"] = -math.exp(-0.5) * jax.nn.sigmoid(
            w0_ref[...] + _dot(jnp.tanh(pw).astype(BF16), w2_ref[...]))
        t["a"] = jax.nn.sigmoid(a0_ref[...] + _dot(pa.astype(BF16), a2_ref[...]))
        if gate_ref is not None:
            gate_ref[...] = _dot(jax.nn.sigmoid(lora[:, 2 * LORA_W:3 * LORA_W]).astype(BF16), g2_ref[...])

    def prep_keys():
        k = shifted(w, 2 * w)
        ones = ones_ref[...]
        kk_raw = k * kk_ref[...]
        kk = kk_raw * lax.rsqrt(jnp.maximum(_dot_lhs_split(kk_raw * kk_raw, ones, 1), 1e-24))
        t.update(kk=kk, kd=k * (1.0 + (t["a"] - 1.0) * ka_ref[...]), bb=t["a"] * kk)

    def prep_bonus():
        r = shifted(0, w)
        v = shifted(2 * w, 3 * w)
        bonus = _dot_lhs_split(r * t["kd"] * rk_ref[...], ones_ref[...], 2) * v
        bonus_ref[...] = bonus if acc_refs[1] is None else bonus + acc_refs[1][...]
        t.update(r=r, v=v)

    def finish_prep():
        r, v, kk, kd, bb, log_decay = (t[name] for name in ("r", "v", "kk", "kd", "bb", "log_decay"))
        ti = lax.broadcasted_iota(jnp.int32, (rows, rows), 0)
        si = lax.broadcasted_iota(jnp.int32, (rows, rows), 1)
        same_chunk = (ti // CHUNK) == (si // CHUNK)
        tri = jnp.where(jnp.logical_and(same_chunk, (si >= ti) if reverse else (si <= ti)), 1.0, 0.0).astype(BF16)
        lw_hi, lw_lo = _split(log_decay, 2)
        cum = _dot(tri, lw_hi) + _dot(tri, lw_lo)
        e_neg = jnp.exp(-cum)
        last = 0 if reverse else CHUNK - 1
        w_v, w_kkt, w_rt, w_kh, w_bh, w_et = wset
        w_v[...] = v
        w_kkt[...] = kk * jnp.exp(cum - log_decay)
        w_rt[...] = r * jnp.exp(cum)
        w_kh[...] = kd * e_neg
        w_bh[...] = bb * e_neg
        for c in range(n_ch):
            w_et[c:c + 1, :] = jnp.exp(cum[c * CHUNK + last:c * CHUNK + last + 1, :])

    prep_at = {"start": prep_lora, "gram": prep_keys, "akv": prep_bonus, "dbl1": finish_prep}

    def issue_prep(point):
        if point in prep_at:
            prep_at[point]()

    issue_prep("start")
    r_v, r_kkt, r_rt, r_kh, r_bh, r_et = rset
    trow = lax.broadcasted_iota(jnp.int32, (CHUNK, QUAD), 0)
    tcol = lax.broadcasted_iota(jnp.int32, (CHUNK, QUAD), 1) % CHUNK
    strict = (trow < tcol) if reverse else (trow > tcol)
    incl = (trow <= tcol) if reverse else (trow >= tcol)
    on_diag = trow == tcol
    eye_cat = jnp.where(on_diag, 1.0, 0.0)
    brow = lax.broadcasted_iota(jnp.int32, (QUAD, QUAD), 0)
    bcol = lax.broadcasted_iota(jnp.int32, (QUAD, QUAD), 1)
    bd_mask = (brow // HEAD_DIM) == (bcol // HEAD_DIM)
    head_of_lane = lax.broadcasted_iota(jnp.int32, (CHUNK, QUAD), 1) // HEAD_DIM
    n_doublings = int(math.log2(CHUNK)) - 1
    heads = range(QUAD // HEAD_DIM)
    quads = range(RW_WIDTH // QUAD)
    order = list(range(n_ch - 1, -1, -1) if reverse else range(n_ch))

    chains = [(c, qd) for c in order for qd in quads]

    def rsl(c):
        return slice(c * CHUNK, (c + 1) * CHUNK)

    def lsl(qd):
        return slice(qd * QUAD, (qd + 1) * QUAD)

    r_y0, r_qt, r_g, r_h0 = prset
    h = {qd: h_ref[qd] for qd in quads}
    issued = [0]

    def issue_state(upto):
        while issued[0] < min(upto, len(order)):
            c = order[issued[0]]
            issued[0] += 1
            for qd in quads:
                lhs = jnp.concatenate([r_g[rsl(c), lsl(qd)], r_qt[rsl(c), lsl(qd)]], axis=0).astype(BF16)
                o2 = _dot(lhs, _block_diag(h[qd], bd_mask))
                h[qd] = o2[0:CHUNK] + r_h0[rsl(c), lsl(qd)]
                y = r_y0[rsl(c), lsl(qd)] + o2[CHUNK:2 * CHUNK]
                y_ref[rsl(c), lsl(qd)] = y if acc_refs[0] is None else y + acc_refs[0][rsl(c), lsl(qd)]
            if issued[0] == len(order):
                for qd in quads:
                    h_ref[qd] = h[qd]

    issue_state(1)
    v_q = {ch: r_v[rsl(ch[0]), lsl(ch[1])] for ch in chains}
    kkt_q = {ch: r_kkt[rsl(ch[0]), lsl(ch[1])] for ch in chains}
    rt_q = {ch: r_rt[rsl(ch[0]), lsl(ch[1])] for ch in chains}

    a_k, a_b, b_k, b_b = {}, {}, {}, {}
    for ch in chains:
        kh = r_kh[rsl(ch[0]), lsl(ch[1])].astype(BF16)
        bh = r_bh[rsl(ch[0]), lsl(ch[1])].astype(BF16)
        zero = jnp.zeros_like(kh)
        rhs = jnp.concatenate([jnp.where(head_of_lane == h, kh, zero) for h in heads]
                              + [jnp.where(head_of_lane == h, bh, zero) for h in heads], axis=0)
        lhs = jnp.concatenate([kkt_q[ch], rt_q[ch]], axis=0).astype(BF16)
        gram = _dot_nt(lhs, rhs)
        a_k[ch] = jnp.where(strict, gram[0:CHUNK, 0:QUAD], 0.0)
        a_b[ch] = jnp.where(strict, gram[0:CHUNK, QUAD:2 * QUAD], 0.0)
        b_k[ch] = jnp.where(incl, gram[CHUNK:2 * CHUNK, 0:QUAD], 0.0)
        b_b[ch] = jnp.where(incl, gram[CHUNK:2 * CHUNK, QUAD:2 * QUAD], 0.0).astype(BF16)
    issue_prep("gram")
    issue_state(2)

    akv, bkv = {}, {}
    for ch in chains:
        both = _dot(jnp.concatenate([a_k[ch], b_k[ch]], axis=0).astype(BF16), _block_diag(v_q[ch], bd_mask))
        akv[ch] = both[0:CHUNK]
        bkv[ch] = both[CHUNK:2 * CHUNK]
    issue_prep("akv")
    issue_state(3)

    m = {ch: -a_b[ch] for ch in chains}
    t_inv = {ch: eye_cat + m[ch] for ch in chains}
    for ch in chains:
        m[ch] = _dot(m[ch].astype(BF16), _block_diag(m[ch], bd_mask))
    issue_prep("m1")
    issue_state(4)
    for step in range(n_doublings):
        final = step == n_doublings - 1
        issue_prep("dbl%d" % step)
        for ch in chains:
            m_bd = _block_diag(m[ch], bd_mask)
            if final:
                t_inv[ch] = t_inv[ch] + _dot(t_inv[ch].astype(BF16), m_bd)
            else:
                both = _dot(jnp.concatenate([m[ch], t_inv[ch]], axis=0).astype(BF16), m_bd)
                m[ch] = both[0:CHUNK]
                t_inv[ch] = t_inv[ch] + both[CHUNK:2 * CHUNK]

    p1, tkk = {}, {}
    for ch in chains:
        t_b = t_inv[ch].astype(BF16)
        p1[ch] = _dot(t_b, _block_diag(akv[ch], bd_mask))
        tkk[ch] = _dot(t_b, _block_diag(kkt_q[ch], bd_mask))

    w_y0, w_qt, w_g, w_h0 = pwset
    for ch in chains:
        c, qd = ch
        w_y0[rsl(c), lsl(qd)] = bkv[ch] - _dot(b_b[ch], _block_diag(p1[ch], bd_mask))
        w_qt[rsl(c), lsl(qd)] = rt_q[ch] - _dot(b_b[ch], _block_diag(tkk[ch], bd_mask))
        e_q = r_et[c:c + 1, lsl(qd)]
        lhs_t = jnp.concatenate([r_kh[rsl(c), lsl(qd)] * e_q, r_bh[rsl(c), lsl(qd)] * e_q], axis=0).astype(BF16)
        rhs_t = jnp.concatenate([jnp.concatenate([v_q[ch], jnp.zeros_like(v_q[ch])], axis=1),
                                 jnp.concatenate([-p1[ch], -tkk[ch]], axis=1)], axis=0).astype(BF16)
        hg = _dot_tn(lhs_t, rhs_t)
        w_h0[rsl(c), lsl(qd)] = _diag_blocks(hg[:, 0:QUAD], head_of_lane)
        w_g[rsl(c), lsl(qd)] = _diag_blocks(hg[:, QUAD:2 * QUAD], head_of_lane) + jnp.where(on_diag, e_q, 0.0)
    issue_state(len(order))


def _rwkv_call(rw_lat, rw_ctx, weights, *, reverse, with_gate, batch, seq, ctx_len, acc=None):
    assert CHUNK == HEAD_DIM
    blk_rows = RW_BLOCK_CHUNKS * CHUNK
    assert ctx_len % blk_rows == 0 and seq % blk_rows == 0
    n_ctx = ctx_len // blk_rows
    n_lat = seq // blk_rows
    n_chunks = n_ctx + n_lat
    halo_per_blk = blk_rows // HALO

    def block_of(s):
        if reverse:
            return jnp.where(s < n_ctx, n_ctx - 1 - s, n_chunks + n_ctx - 1 - s)
        return s

    def stream_specs(first, count):
        n_halo = batch * count * halo_per_blk

        def blk(b, s):
            return b * count + jnp.clip(block_of(jnp.minimum(s, n_chunks - 1)) - first, 0, count - 1)

        return [pl.BlockSpec((blk_rows, RW_COLS), lambda b, s: (blk(b, s), 0)),
                pl.BlockSpec((HALO, RW_COLS), lambda b, s: (jnp.maximum(blk(b, s) * halo_per_blk - 1, 0), 0)),
                pl.BlockSpec((HALO, RW_COLS),
                             lambda b, s: (jnp.minimum((blk(b, s) + 1) * halo_per_blk, n_halo - 1), 0))]

    def out_block(b, step):
        lat = jnp.maximum(step, n_ctx) - n_ctx
        return (b * n_lat + (n_lat - 1 - lat if reverse else lat), 0)

    def const_spec(a):
        return pl.BlockSpec(a.shape, lambda b, s: (0,) * a.ndim)

    prep_spec = pl.BlockSpec((blk_rows, RW_WIDTH), lambda b, s: out_block(b, jnp.minimum(s, n_chunks - 1)))
    y_spec = pl.BlockSpec((blk_rows, RW_WIDTH), lambda b, s: out_block(b, jnp.maximum(s - 2, 0)))
    out_shape = jax.ShapeDtypeStruct((batch * seq, RW_WIDTH), F32)
    n_out = 3 if with_gate else 2
    assert with_gate == (acc is None)
    acc_specs = [] if acc is None else [y_spec, prep_spec]
    term_set = [pltpu.VMEM((blk_rows, RW_WIDTH), F32)] * 5 + [pltpu.VMEM((8, RW_WIDTH), F32)]
    chain_set = [pltpu.VMEM((blk_rows, RW_WIDTH), F32)] * 4
    return pl.pallas_call(
        functools.partial(_rwkv_kernel, reverse, with_gate, n_ctx, n_chunks),
        grid=(batch, n_chunks + 2),
        in_specs=stream_specs(n_ctx, n_lat) + stream_specs(0, n_ctx) + [const_spec(a) for a in weights] + acc_specs,
        out_specs=[y_spec] + [prep_spec] * (n_out - 1),
        out_shape=[out_shape] * n_out,
        scratch_shapes=[pltpu.VMEM((RW_WIDTH // QUAD, CHUNK, QUAD), F32)] + term_set * 2 + chain_set * 2,
        compiler_params=_cparams("parallel", "arbitrary"),
        name="rwkv_bwd" if reverse else "rwkv_fwd",
    )(rw_lat, rw_lat, rw_lat, rw_ctx, rw_ctx, rw_ctx, *weights, *(acc or ()))


def _tail_kernel(d_model, alpha, ff_chunk, yna_ref, y_ref, bonus_ref, gate_ref, x_ref, moda_ref, modb_ref,
                 gng_ref, gnb_ref, ones_ref, wo_ref, ln1g_ref, ln1b_ref, w1_ref, w2_ref, ln2g_ref, ln2b_ref,
                 o_ref, x1_a, xm_a, x1_b, xm_b):
    i = pl.program_id(0)

    @pl.when(i == 0)
    def _():
        x1_b[...] = jnp.zeros_like(x1_b)
        xm_b[...] = jnp.zeros_like(xm_b)

    def step(x1_w, xm_w, x1_r, xm_r):
        ones = ones_ref[...]
        inv_n = 1.0 / HEAD_DIM
        n_chunks = w1_ref.shape[1] // ff_chunk
        t = {}

        def readout_stats():
            y = y_ref[...]
            mu = _dot_lhs_split(y, ones, 2) * inv_n
            yc = y - mu
            t.update(yc=yc, var=_dot_lhs_split(yc * yc, ones, 2) * inv_n)

        def out_projection():
            yn = t["yc"] * lax.rsqrt(t["var"] + GN_EPS) * gng_ref[...] + gnb_ref[...]
            y_rw = ((yn + bonus_ref[...]) * gate_ref[...]).astype(BF16)
            t["proj"] = (_dot(yna_ref[...], wo_ref[0:NA_WIDTH, :])
                         + _dot(y_rw, wo_ref[NA_WIDTH:NA_WIDTH + RW_WIDTH, :]))

        def norm_modulate():
            mod = moda_ref[...]
            g1 = mod[:, 2 * d_model:3 * d_model]
            shift = mod[:, 3 * d_model:4 * d_model]
            scale = mod[:, 4 * d_model:5 * d_model]
            x1 = _normalize(alpha * x_ref[...] + g1 * t["proj"], LN_EPS) * ln1g_ref[...] + ln1b_ref[...]
            x1_w[...] = x1
            xm_w[...] = (_normalize(x1, LN_EPS) * (1.0 + scale) + shift).astype(BF16)

        stage_a = [(min(k, n_chunks - 1), stage) for k, stage in
                   enumerate((readout_stats, out_projection, norm_modulate))]
        xm = xm_r[...]
        acc = None
        for c in range(n_chunks):
            c0 = c * ff_chunk
            hid = jnp.maximum(_dot(xm, w1_ref[:, c0:c0 + ff_chunk]), 0.0)
            part = _dot((hid * hid).astype(BF16), w2_ref[c0:c0 + ff_chunk, :])
            acc = part if acc is None else acc + part
            for after, stage in stage_a:
                if after == c:
                    stage()
        g2 = modb_ref[...][:, 5 * d_model:6 * d_model]
        o_ref[...] = _normalize(alpha * x1_r[...] + g2 * acc, LN_EPS) * ln2g_ref[...] + ln2b_ref[...]

    @pl.when(i % 2 == 0)
    def _():
        step(x1_a, xm_a, x1_b, xm_b)

    @pl.when(i % 2 == 1)
    def _():
        step(x1_b, xm_b, x1_a, xm_a)


def _tail_call(y_na, y, bonus, gate, x2d, mod3, consts, *, tm, seq, alpha):
    rows, d = x2d.shape
    tiles_per_batch = seq // tm
    n_tiles = rows // tm

    def tile_a(i):
        return jnp.minimum(i, n_tiles - 1)

    def tile_b(i):
        return jnp.maximum(i - 1, 0)

    half = pl.BlockSpec((tm, RW_WIDTH), lambda i: (tile_a(i), 0))

    def resident(a):
        return pl.BlockSpec(a.shape, lambda i: (0,) * a.ndim, pipeline_mode=pl.Buffered(1))

    return pl.pallas_call(
        functools.partial(_tail_kernel, d, alpha, 1024),
        grid=(n_tiles + 1,),
        in_specs=[half] * 4 + [pl.BlockSpec((tm, d), lambda i: (tile_a(i), 0)),
                               pl.BlockSpec((None, 1, mod3.shape[2]), lambda i: (tile_a(i) // tiles_per_batch, 0, 0)),
                               pl.BlockSpec((None, 1, mod3.shape[2]), lambda i: (tile_b(i) // tiles_per_batch, 0, 0))]
                 + [resident(a) for a in consts],
        out_specs=pl.BlockSpec((tm, d), lambda i: (tile_b(i), 0)),
        out_shape=jax.ShapeDtypeStruct((rows, d), F32),
        scratch_shapes=[pltpu.VMEM((tm, d), F32), pltpu.VMEM((tm, d), BF16)] * 2,
        compiler_params=_cparams("arbitrary"),
        name="tail",
    )(y_na, y, bonus, gate, x2d, mod3, mod3, *consts)


def _rope_tables(seq):
    f = HEAD_DIM // 4
    t = np.arange(seq)
    row = (t // GRID_W).astype(np.float32)
    col = (t % GRID_W).astype(np.float32)
    inv = (ROPE_BASE ** (-np.arange(f, dtype=np.float32) / f)).astype(np.float32)
    ang_r = row[:, None] * inv[None, :]
    ang_c = col[:, None] * inv[None, :]
    zero = np.zeros_like(ang_r)
    cos = np.concatenate([np.cos(ang_r), np.cos(ang_r), np.cos(ang_c), np.cos(ang_c)], axis=1)
    sa = np.concatenate([-np.sin(ang_r), zero, -np.sin(ang_c), zero], axis=1)
    sb = np.concatenate([zero, np.sin(ang_r), zero, np.sin(ang_c)], axis=1)
    return tuple(jnp.asarray(np.concatenate([z, z], axis=1), F32) for z in (cos, sa, sb))


def _bias_table(rpb):
    qc = np.arange(GRID_W)[:, None]
    kc = np.arange(GRID_W)[None, :]
    c0 = np.clip(qc - WIN_W // 2, 0, GRID_W - WIN_W)
    inside = (kc >= c0) & (kc < c0 + WIN_W)
    pad = GRID_W - WIN_W
    padded = jnp.pad(rpb.astype(F32) * LOG2_E, ((0, 0), (0, 0), (pad, pad)))
    g = jnp.stack([padded[:, :, GRID_W - 1 - q:2 * GRID_W - 1 - q] for q in range(GRID_W)], axis=2)
    g = jnp.where(inside[None, None], g, NEG_BIAS)
    g = g.transpose(1, 0, 2, 3)
    return jnp.concatenate([g[:-1], g[1:]], axis=-1)


def _pad_lora(w, d):
    z = jnp.zeros_like(w[d])
    return jnp.concatenate([w[0] if d == 0 else z, w[1] if d == 1 else z], axis=0)


def kernel(x, c, ctx, c_ctx, w_mod, b_mod, w_in, w_out, ln1_g, ln1_b, mlp_w1, mlp_w2, ln2_g, ln2_b, na_rpb,
           rw_mu_prev, rw_mu_next, rw_w0, rw_w2, rw_a0, rw_a2, rw_g2, rw_k_k, rw_k_a, rw_r_k, rw_gn_g, rw_gn_b):
    depth = w_mod.shape[0]
    assert depth == 1, "single-layer trunk only (the context stream is never updated)"
    batch, seq, d = x.shape
    ctx_len = ctx.shape[1]
    alpha = (2 * depth) ** 0.25
    tm = ROW_TILE
    tm_ctx = math.gcd(ctx_len, ROW_TILE)
    assert seq % (Q_ROWS * GRID_W) == 0 and seq % tm == 0 and tm_ctx % 8 == 0
    assert batch + 1 <= 8

    cc = jnp.zeros((8, d), F32).at[:batch].set(c).at[batch].set(c_ctx)
    mod3 = _mod_call(cc, w_mod[0], b_mod[0]).reshape(8, 1, N_MOD * d)

    w_in_b = w_in[0].astype(BF16)
    tiles_lat = seq // tm
    x2d = x.reshape(batch * seq, d)
    q, qr, kr, v, rw_lat = _inproj_call(
        x2d, mod3, _rope_tables(seq), w_in_b, tm=tm,
        mod_index=lambda i: i // tiles_lat, table_index=lambda i: i % tiles_lat, name="inproj")
    ident = tuple(jnp.full((tm_ctx, 2 * HEAD_DIM), val, F32) for val in (1.0, 0.0, 0.0))
    _, _, kc, vc, rw_ctx = _inproj_call(
        ctx.reshape(batch * ctx_len, d), mod3, ident, w_in_b, tm=tm_ctx,
        mod_index=lambda i: batch, table_index=lambda i: 0, name="inproj_ctx")

    y_na = _na_call(q, qr, kr, v, kc, vc, _bias_table(na_rpb[0]), batch=batch, seq=seq, ctx_len=ctx_len)

    lane_head = jnp.arange(RW_WIDTH) // HEAD_DIM
    ones = (lane_head[:, None] == lane_head[None, :]).astype(BF16)
    row = lambda a: a.reshape(1, -1).astype(F32)
    def dir_weights(dirn):
        return (row(rw_mu_prev[0]), row(rw_mu_next[0]), row(rw_w0[0, dirn]),
                _pad_lora(rw_w2[0], dirn).astype(BF16), row(rw_a0[0, dirn]),
                _pad_lora(rw_a2[0], dirn).astype(BF16), rw_g2[0].astype(BF16),
                row(rw_k_k[0]), row(rw_k_a[0]), row(rw_r_k[0]), ones)

    y_f, bonus_f, gate = _rwkv_call(rw_lat, rw_ctx, dir_weights(0), reverse=False, with_gate=True,
                                    batch=batch, seq=seq, ctx_len=ctx_len)
    y_sum, bonus_sum = _rwkv_call(rw_lat, rw_ctx, dir_weights(1), reverse=True, with_gate=False,
                                  batch=batch, seq=seq, ctx_len=ctx_len, acc=(y_f, bonus_f))

    consts = (row(rw_gn_g[0]), row(rw_gn_b[0]), ones, w_out[0].astype(BF16), row(ln1_g[0]), row(ln1_b[0]),
              mlp_w1[0].astype(BF16), mlp_w2[0].astype(BF16), row(ln2_g[0]), row(ln2_b[0]))
    out = _tail_call(y_na, y_sum, bonus_sum, gate, x2d, mod3, consts, tm=tm, seq=seq, alpha=alpha)
    return out.reshape(batch, seq, d)
```

```python
import functools
import math

import jax
import jax.numpy as jnp
import numpy as np
from jax import lax
from jax.experimental import pallas as pl
from jax.experimental.pallas import tpu as pltpu

HEAD_DIM = 64
NA_HEADS = 8
RW_HEADS = 8
NA_WIDTH = NA_HEADS * HEAD_DIM
RW_WIDTH = RW_HEADS * HEAD_DIM
GRID_W = 64
WIN_H = 8
WIN_W = 16
ROPE_BASE = 10000.0
N_DIR = 2
DECAY_LORA = 64
AAA_LORA = 64
GATE_LORA = 128
LORA_W = N_DIR * DECAY_LORA
NA_COLS = 3 * NA_WIDTH
RW_COLS = 3 * RW_WIDTH + 3 * LORA_W
N_MOD = 6
LN_EPS = 1e-6
GN_EPS = 64e-5
NEG_BIAS = -1e30

CHUNK = 64
RW_BLOCK_CHUNKS = 4
QUAD = 2 * HEAD_DIM
HALO = 8
LOG2_E = math.log2(math.e)
Q_ROWS = 8
NA_UNROLL = 4
KV_GROUP = 4
ROW_TILE = 512
VMEM_LIMIT = 56 * 1024 * 1024

F32 = jnp.float32
BF16 = jnp.bfloat16


def _cparams(*sem):
    return pltpu.CompilerParams(dimension_semantics=sem, vmem_limit_bytes=VMEM_LIMIT)


def _dot(a, b):
    return jnp.dot(a, b, preferred_element_type=F32)


def _dot_nt(a, b):
    return lax.dot_general(a, b, (((1,), (1,)), ((), ())), preferred_element_type=F32)


def _dot_tn(a, b):
    return lax.dot_general(a, b, (((0,), (0,)), ((), ())), preferred_element_type=F32)


def _split(x, pieces):
    out = []
    for _ in range(pieces):
        p = x.astype(BF16)
        out.append(p)
        x = x - p.astype(F32)
    return out


def _dot_lhs_split(x, w_bf16, pieces):
    acc = None
    for p in _split(x, pieces):
        t = _dot(p, w_bf16)
        acc = t if acc is None else acc + t
    return acc


def _dot_x3(a, b):
    a_hi, a_lo = _split(a, 2)
    b_hi, b_lo = _split(b, 2)
    return _dot(a_hi, b_hi) + (_dot(a_lo, b_hi) + _dot(a_hi, b_lo))


def _normalize(x, eps):
    mu = jnp.mean(x, axis=-1, keepdims=True)
    xc = x - mu
    var = jnp.mean(xc * xc, axis=-1, keepdims=True)
    return xc * lax.rsqrt(var + eps)


def _mod_kernel(c_ref, w_ref, b_ref, o_ref):
    c = c_ref[...]
    s = c * jax.nn.sigmoid(c)
    o_ref[...] = _dot(s.astype(BF16), w_ref[...].astype(BF16)) + b_ref[...]


def _mod_call(cc, w_mod, b_mod):
    rows, d = cc.shape
    n = w_mod.shape[1]
    tn = 512
    return pl.pallas_call(
        _mod_kernel,
        grid=(n // tn,),
        in_specs=[pl.BlockSpec((rows, d), lambda j: (0, 0)),
                  pl.BlockSpec((d, tn), lambda j: (0, j)),
                  pl.BlockSpec((1, tn), lambda j: (0, j))],
        out_specs=pl.BlockSpec((rows, tn), lambda j: (0, j)),
        out_shape=jax.ShapeDtypeStruct((rows, n), F32),
        compiler_params=_cparams("arbitrary"),
        name="mod",
    )(cc, w_mod, b_mod.reshape(1, n))


def _inproj_kernel(d_model, x_ref, mod_ref, cos_ref, sa_ref, sb_ref, w_ref, q_ref, qr_ref, kr_ref, v_ref, rw_ref):
    x = x_ref[...]
    mod = mod_ref[...]
    shift = mod[:, 0:d_model]
    scale = mod[:, d_model:2 * d_model]
    xm = (_normalize(x, LN_EPS) * (1.0 + scale) + shift).astype(BF16)

    reps = NA_WIDTH // cos_ref.shape[1]
    cos = jnp.concatenate([cos_ref[...]] * reps, axis=1)
    sa = jnp.concatenate([sa_ref[...]] * reps, axis=1)
    sb = jnp.concatenate([sb_ref[...]] * reps, axis=1)

    def rope(z):
        up = pltpu.roll(z, NA_WIDTH - HEAD_DIM // 4, 1)
        down = pltpu.roll(z, HEAD_DIM // 4, 1)
        return z * cos + up * sa + down * sb

    qk_scale = HEAD_DIM ** -0.5 * LOG2_E
    q = _dot(xm, w_ref[:, 0:NA_WIDTH]) * qk_scale
    q_ref[...] = q.astype(BF16)
    qr_ref[...] = rope(q).astype(BF16)
    k = _dot(xm, w_ref[:, NA_WIDTH:2 * NA_WIDTH])
    kr_ref[...] = rope(k).astype(BF16)
    v_ref[...] = _dot(xm, w_ref[:, 2 * NA_WIDTH:NA_COLS]).astype(BF16)
    rw_ref[...] = _dot(xm, w_ref[:, NA_COLS:NA_COLS + RW_COLS])


def _inproj_call(x2d, mod3, tables, w_in_bf16, *, tm, mod_index, table_index, name):
    rows, d = x2d.shape
    cos, sa, sb = tables
    tw = cos.shape[1]
    ncols = w_in_bf16.shape[1]
    na_spec = pl.BlockSpec((tm, NA_WIDTH), lambda i: (i, 0))
    table_spec = pl.BlockSpec((tm, tw), lambda i: (table_index(i), 0))
    na_shape = jax.ShapeDtypeStruct((rows, NA_WIDTH), BF16)
    return pl.pallas_call(
        functools.partial(_inproj_kernel, d),
        grid=(rows // tm,),
        in_specs=[pl.BlockSpec((tm, d), lambda i: (i, 0)),
                  pl.BlockSpec((None, 1, mod3.shape[2]), lambda i: (mod_index(i), 0, 0)),
                  table_spec, table_spec, table_spec,
                  pl.BlockSpec((d, ncols), lambda i: (0, 0))],
        out_specs=[na_spec, na_spec, na_spec, na_spec, pl.BlockSpec((tm, RW_COLS), lambda i: (i, 0))],
        out_shape=[na_shape, na_shape, na_shape, na_shape, jax.ShapeDtypeStruct((rows, RW_COLS), F32)],
        compiler_params=_cparams("parallel"),
        name=name,
    )(x2d, mod3, cos, sa, sb, w_in_bf16)


def _na_kernel(grid_rows, q_ref, qr_ref, k0, k1, k2, k3, v0, v1, v2, v3, kc_ref, vc_ref, bias_ref,
               o_ref, kwin, vwin):
    i = pl.program_id(1)
    grp = KV_GROUP * GRID_W
    for g, (kr, vr) in enumerate(((k0, v0), (k1, v1), (k2, v2), (k3, v3))):
        kwin[g * grp:(g + 1) * grp, :] = kr[...]
        vwin[g * grp:(g + 1) * grp, :] = vr[...]
    n_groups = grid_rows // KV_GROUP
    win_row0 = KV_GROUP * jnp.clip(2 * i - 1, 0, n_groups - 4)
    win_keys = WIN_H * GRID_W
    lane = lax.broadcasted_iota(jnp.int32, (GRID_W, 2 * HEAD_DIM), 1)
    even = lane < HEAD_DIM

    pairs = range(NA_HEADS // 2)
    lanes = [slice(pair * 2 * HEAD_DIM, (pair + 1) * 2 * HEAD_DIM) for pair in pairs]

    def rows_body(jj, carry):
        units = []
        for u in range(NA_UNROLL):
            j = NA_UNROLL * jj + u
            irow = Q_ROWS * i + j
            r0 = jnp.clip(irow - WIN_H // 2, 0, grid_rows - WIN_H)
            koff = pl.multiple_of((r0 - win_row0) * GRID_W, GRID_W)
            brow = (WIN_H - 1) - (irow - r0)
            qoff = pl.multiple_of(j * GRID_W, GRID_W)
            units += [(koff, brow, qoff, pair) for pair in pairs]
        def split_heads(z):
            zero = jnp.zeros_like(z)
            return jnp.concatenate([jnp.where(even, z, zero), jnp.where(even, zero, z)], axis=0)

        two = 2 * GRID_W
        s_loc = []
        for koff, brow, qoff, pair in units:
            qr2 = split_heads(qr_ref[pl.ds(qoff, GRID_W), lanes[pair]])
            bias = jnp.concatenate(
                [jnp.concatenate([bias_ref[brow + 2 * m, 2 * pair], bias_ref[brow + 2 * m, 2 * pair + 1]], axis=0)
                 for m in range(WIN_H // 2)], axis=1)
            s_loc.append(_dot_nt(qr2, kwin[pl.ds(koff, win_keys), lanes[pair]]) + bias)
        s_ctx = [None] * len(units)
        for pair in pairs:
            ns = [n for n, unit in enumerate(units) if unit[3] == pair]
            qp_all = jnp.concatenate([split_heads(q_ref[pl.ds(units[n][2], GRID_W), lanes[pair]]) for n in ns],
                                     axis=0)
            sc = _dot_nt(qp_all, kc_ref[:, lanes[pair]])
            for k, n in enumerate(ns):
                s_ctx[n] = sc[k * two:(k + 1) * two]
        p_loc, p_ctx, denom = [], [], []
        for n in range(len(units)):
            mx = jnp.maximum(jnp.max(s_loc[n], axis=-1, keepdims=True), jnp.max(s_ctx[n], axis=-1, keepdims=True))
            pl_ = jnp.exp2(s_loc[n] - mx)
            pc_ = jnp.exp2(s_ctx[n] - mx)
            denom.append(jnp.sum(pl_, axis=-1, keepdims=True) + jnp.sum(pc_, axis=-1, keepdims=True))
            p_loc.append(pl_.astype(BF16))
            p_ctx.append(pc_.astype(BF16))
        o_ctx = [None] * len(units)
        for pair in pairs:
            ns = [n for n, unit in enumerate(units) if unit[3] == pair]
            oc = _dot(jnp.concatenate([p_ctx[n] for n in ns], axis=0), vc_ref[:, lanes[pair]])
            for k, n in enumerate(ns):
                o_ctx[n] = oc[k * two:(k + 1) * two]
        for n, (koff, brow, qoff, pair) in enumerate(units):
            o = (_dot(p_loc[n], vwin[pl.ds(koff, win_keys), lanes[pair]]) + o_ctx[n]) / denom[n]
            o_ref[pl.ds(qoff, GRID_W), lanes[pair]] = jnp.where(
                even, o[0:GRID_W], o[GRID_W:2 * GRID_W]).astype(BF16)
        return carry

    lax.fori_loop(0, Q_ROWS // NA_UNROLL, rows_body, 0)


def _na_call(q, qr, kr, v, kc, vc, bias_tab, *, batch, seq, ctx_len):
    grid_rows = seq // GRID_W
    n_blocks = grid_rows // Q_ROWS
    n_groups = grid_rows // KV_GROUP
    blk = Q_ROWS * GRID_W
    grp = KV_GROUP * GRID_W

    def group_spec(g):
        return pl.BlockSpec((grp, NA_WIDTH),
                            lambda b, i: (b * n_groups + jnp.clip(2 * i - 1, 0, n_groups - 4) + g, 0))

    q_spec = pl.BlockSpec((blk, NA_WIDTH), lambda b, i: (b * n_blocks + i, 0))
    c_spec = pl.BlockSpec((ctx_len, NA_WIDTH), lambda b, i: (b, 0))
    return pl.pallas_call(
        functools.partial(_na_kernel, grid_rows),
        grid=(batch, n_blocks),
        in_specs=[q_spec, q_spec] + [group_spec(g) for g in range(4)] + [group_spec(g) for g in range(4)]
                 + [c_spec, c_spec, pl.BlockSpec(bias_tab.shape, lambda b, i: (0, 0, 0, 0))],
        out_specs=q_spec,
        out_shape=jax.ShapeDtypeStruct((batch * seq, NA_WIDTH), BF16),
        scratch_shapes=[pltpu.VMEM((4 * grp, NA_WIDTH), BF16), pltpu.VMEM((4 * grp, NA_WIDTH), BF16)],
        compiler_params=_cparams("parallel", "arbitrary"),
        name="natten",
    )(q, qr, kr, kr, kr, kr, v, v, v, v, kc, vc, bias_tab)


def _block_diag(x_cat, bd_mask):
    xb = x_cat.astype(BF16)
    tiled = jnp.concatenate([xb] * (QUAD // CHUNK), axis=0)
    return jnp.where(bd_mask, tiled, jnp.zeros_like(tiled))


def _diag_blocks(x, head_of_lane):
    n = QUAD // HEAD_DIM
    out = x[(n - 1) * HEAD_DIM:n * HEAD_DIM]
    for h in range(n - 2, -1, -1):
        out = jnp.where(head_of_lane == h, x[h * HEAD_DIM:(h + 1) * HEAD_DIM], out)
    return out


def _rwkv_kernel(reverse, with_gate, n_ctx, n_blocks,
                 lat_ref, lat_prev_ref, lat_next_ref, ctx_ref, ctx_prev_ref, ctx_next_ref,
                 mup_ref, mun_ref, w0_ref, w2_ref, a0_ref, a2_ref, g2_ref,
                 kk_ref, ka_ref, rk_ref, ones_ref, *refs):
    if with_gate:
        acc_refs = (None, None)
        y_ref, bonus_ref, gate_ref = refs[0:3]
        refs = refs[3:]
    else:
        acc_refs = refs[0:2]
        y_ref, bonus_ref = refs[2:4]
        gate_ref = None
        refs = refs[4:]
    h_ref = refs[0]
    sets = (refs[1:7], refs[7:13])
    psets = (refs[13:17], refs[17:21])
    s = pl.program_id(1)

    @pl.when(s == 0)
    def _():
        h_ref[...] = jnp.zeros_like(h_ref)
        for ref in sets[1] + psets[1]:
            ref[...] = jnp.zeros_like(ref)

    @pl.when(s % 2 == 0)
    def _():
        _rwkv_step(reverse, n_ctx, n_blocks, s, sets[0], sets[1], psets[0], psets[1],
                   lat_ref, lat_prev_ref, lat_next_ref,
                   ctx_ref, ctx_prev_ref, ctx_next_ref, mup_ref, mun_ref, w0_ref, w2_ref, a0_ref, a2_ref,
                   g2_ref, kk_ref, ka_ref, rk_ref, ones_ref, acc_refs, y_ref, bonus_ref, gate_ref, h_ref)

    @pl.when(s % 2 == 1)
    def _():
        _rwkv_step(reverse, n_ctx, n_blocks, s, sets[1], sets[0], psets[1], psets[0],
                   lat_ref, lat_prev_ref, lat_next_ref,
                   ctx_ref, ctx_prev_ref, ctx_next_ref, mup_ref, mun_ref, w0_ref, w2_ref, a0_ref, a2_ref,
                   g2_ref, kk_ref, ka_ref, rk_ref, ones_ref, acc_refs, y_ref, bonus_ref, gate_ref, h_ref)


def _rwkv_step(reverse, n_ctx, n_blocks, s, wset, rset, pwset, prset, lat_ref, lat_prev_ref, lat_next_ref,
               ctx_ref, ctx_prev_ref, ctx_next_ref, mup_ref, mun_ref, w0_ref, w2_ref, a0_ref, a2_ref,
               g2_ref, kk_ref, ka_ref, rk_ref, ones_ref, acc_refs, y_ref, bonus_ref, gate_ref, h_ref):
    sp = jnp.minimum(s, n_blocks - 1)
    if reverse:
        n = jnp.where(sp < n_ctx, n_ctx - 1 - sp, n_blocks + n_ctx - 1 - sp)
    else:
        n = sp

    rows = lat_ref.shape[0]
    n_ch = rows // CHUNK
    w = RW_WIDTH
    t = {}

    is_ctx = n < n_ctx
    has_prev = jnp.logical_and(n != 0, n != n_ctx)
    has_next = jnp.logical_and(n != n_ctx - 1, n != n_blocks - 1)
    row = lax.broadcasted_iota(jnp.int32, (rows, 1), 0)

    def shifted(c0, c1):
        p = jnp.where(is_ctx, ctx_ref[:, c0:c1], lat_ref[:, c0:c1])
        prow = jnp.where(is_ctx, ctx_prev_ref[HALO - 1:HALO, c0:c1], lat_prev_ref[HALO - 1:HALO, c0:c1])
        nrow = jnp.where(is_ctx, ctx_next_ref[0:1, c0:c1], lat_next_ref[0:1, c0:c1])
        prow = jnp.where(has_prev, prow, 0.0)
        nrow = jnp.where(has_next, nrow, 0.0)
        prev = jnp.where(row == 0, prow, pltpu.roll(p, 1, 0))
        nxt = jnp.where(row == rows - 1, nrow, pltpu.roll(p, rows - 1, 0))
        return p + mup_ref[:, c0:c1] * (prev - p) + mun_ref[:, c0:c1] * (nxt - p)

    def prep_lora():
        lora = shifted(3 * w, 3 * w + 3 * LORA_W)
        pw = lora[:, 0:LORA_W]
        pa = lora[:, LORA_W:2 * LORA_W]
        t["log_decay"] = -math.exp(-0.5) * jax.nn.sigmoid(
            w0_ref[...] + _dot(jnp.tanh(pw).astype(BF16), w2_ref[...]))
        t["a"] = jax.nn.sigmoid(a0_ref[...] + _dot(pa.astype(BF16), a2_ref[...]))
        if gate_ref is not None:
            gate_ref[...] = _dot(jax.nn.sigmoid(lora[:, 2 * LORA_W:3 * LORA_W]).astype(BF16), g2_ref[...])

    def prep_keys():
        k = shifted(w, 2 * w)
        ones = ones_ref[...]
        kk_raw = k * kk_ref[...]
        kk = kk_raw * lax.rsqrt(jnp.maximum(_dot_lhs_split(kk_raw * kk_raw, ones, 1), 1e-24))
        t.update(kk=kk, kd=k * (1.0 + (t["a"] - 1.0) * ka_ref[...]), bb=t["a"] * kk)

    def prep_bonus():
        r = shifted(0, w)
        v = shifted(2 * w, 3 * w)
        bonus = _dot_lhs_split(r * t["kd"] * rk_ref[...], ones_ref[...], 1) * v
        bonus_ref[...] = bonus if acc_refs[1] is None else bonus + acc_refs[1][...]
        t.update(r=r, v=v)

    def finish_prep():
        r, v, kk, kd, bb, log_decay = (t[name] for name in ("r", "v", "kk", "kd", "bb", "log_decay"))
        ti = lax.broadcasted_iota(jnp.int32, (rows, rows), 0)
        si = lax.broadcasted_iota(jnp.int32, (rows, rows), 1)
        same_chunk = (ti // CHUNK) == (si // CHUNK)
        tri = jnp.where(jnp.logical_and(same_chunk, (si >= ti) if reverse else (si <= ti)), 1.0, 0.0).astype(BF16)
        lw_hi, lw_lo = _split(log_decay, 2)
        cum = _dot(tri, lw_hi) + _dot(tri, lw_lo)
        e_neg = jnp.exp(-cum)
        last = 0 if reverse else CHUNK - 1
        w_v, w_kkt, w_rt, w_kh, w_bh, w_et = wset
        w_v[...] = v
        w_kkt[...] = kk * jnp.exp(cum - log_decay)
        w_rt[...] = r * jnp.exp(cum)
        w_kh[...] = kd * e_neg
        w_bh[...] = bb * e_neg
        for c in range(n_ch):
            w_et[c:c + 1, :] = jnp.exp(cum[c * CHUNK + last:c * CHUNK + last + 1, :])

    prep_at = {"start": prep_lora, "gram": prep_keys, "akv": prep_bonus, "dbl1": finish_prep}

    def issue_prep(point):
        if point in prep_at:
            prep_at[point]()

    issue_prep("start")
    r_v, r_kkt, r_rt, r_kh, r_bh, r_et = rset
    trow = lax.broadcasted_iota(jnp.int32, (CHUNK, QUAD), 0)
    tcol = lax.broadcasted_iota(jnp.int32, (CHUNK, QUAD), 1) % CHUNK
    strict = (trow < tcol) if reverse else (trow > tcol)
    incl = (trow <= tcol) if reverse else (trow >= tcol)
    on_diag = trow == tcol
    eye_cat = jnp.where(on_diag, 1.0, 0.0)
    brow = lax.broadcasted_iota(jnp.int32, (QUAD, QUAD), 0)
    bcol = lax.broadcasted_iota(jnp.int32, (QUAD, QUAD), 1)
    bd_mask = (brow // HEAD_DIM) == (bcol // HEAD_DIM)
    head_of_lane = lax.broadcasted_iota(jnp.int32, (CHUNK, QUAD), 1) // HEAD_DIM
    n_doublings = int(math.log2(CHUNK)) - 1
    heads = range(QUAD // HEAD_DIM)
    quads = range(RW_WIDTH // QUAD)
    order = list(range(n_ch - 1, -1, -1) if reverse else range(n_ch))

    chains = [(c, qd) for c in order for qd in quads]

    def rsl(c):
        return slice(c * CHUNK, (c + 1) * CHUNK)

    def lsl(qd):
        return slice(qd * QUAD, (qd + 1) * QUAD)

    r_y0, r_qt, r_g, r_h0 = prset
    h = {qd: h_ref[qd] for qd in quads}
    issued = [0]

    def issue_state(upto):
        while issued[0] < min(upto, len(order)):
            c = order[issued[0]]
            issued[0] += 1
            for qd in quads:
                lhs = jnp.concatenate([r_g[rsl(c), lsl(qd)], r_qt[rsl(c), lsl(qd)]], axis=0).astype(BF16)
                o2 = _dot(lhs, _block_diag(h[qd], bd_mask))
                h[qd] = o2[0:CHUNK] + r_h0[rsl(c), lsl(qd)]
                y = r_y0[rsl(c), lsl(qd)] + o2[CHUNK:2 * CHUNK]
                y_ref[rsl(c), lsl(qd)] = y if acc_refs[0] is None else y + acc_refs[0][rsl(c), lsl(qd)]
            if issued[0] == len(order):
                for qd in quads:
                    h_ref[qd] = h[qd]

    issue_state(1)
    v_q = {ch: r_v[rsl(ch[0]), lsl(ch[1])] for ch in chains}
    kkt_q = {ch: r_kkt[rsl(ch[0]), lsl(ch[1])] for ch in chains}
    rt_q = {ch: r_rt[rsl(ch[0]), lsl(ch[1])] for ch in chains}

    a_k, a_b, b_k, b_b = {}, {}, {}, {}
    for ch in chains:
        kh = r_kh[rsl(ch[0]), lsl(ch[1])].astype(BF16)
        bh = r_bh[rsl(ch[0]), lsl(ch[1])].astype(BF16)
        zero = jnp.zeros_like(kh)
        rhs = jnp.concatenate([jnp.where(head_of_lane == h, kh, zero) for h in heads]
                              + [jnp.where(head_of_lane == h, bh, zero) for h in heads], axis=0)
        lhs = jnp.concatenate([kkt_q[ch], rt_q[ch]], axis=0).astype(BF16)
        gram = _dot_nt(lhs, rhs)
        a_k[ch] = jnp.where(strict, gram[0:CHUNK, 0:QUAD], 0.0)
        a_b[ch] = jnp.where(strict, gram[0:CHUNK, QUAD:2 * QUAD], 0.0)
        b_k[ch] = jnp.where(incl, gram[CHUNK:2 * CHUNK, 0:QUAD], 0.0)
        b_b[ch] = jnp.where(incl, gram[CHUNK:2 * CHUNK, QUAD:2 * QUAD], 0.0).astype(BF16)
    issue_prep("gram")
    issue_state(2)

    akv, bkv = {}, {}
    for ch in chains:
        both = _dot(jnp.concatenate([a_k[ch], b_k[ch]], axis=0).astype(BF16), _block_diag(v_q[ch], bd_mask))
        akv[ch] = both[0:CHUNK]
        bkv[ch] = both[CHUNK:2 * CHUNK]
    issue_prep("akv")
    issue_state(3)

    m = {ch: -a_b[ch] for ch in chains}
    t_inv = {ch: eye_cat + m[ch] for ch in chains}
    for ch in chains:
        m[ch] = _dot(m[ch].astype(BF16), _block_diag(m[ch], bd_mask))
    issue_prep("m1")
    issue_state(4)
    for step in range(n_doublings):
        final = step == n_doublings - 1
        issue_prep("dbl%d" % step)
        for ch in chains:
            m_bd = _block_diag(m[ch], bd_mask)
            if final:
                t_inv[ch] = t_inv[ch] + _dot(t_inv[ch].astype(BF16), m_bd)
            else:
                both = _dot(jnp.concatenate([m[ch], t_inv[ch]], axis=0).astype(BF16), m_bd)
                m[ch] = both[0:CHUNK]
                t_inv[ch] = t_inv[ch] + both[CHUNK:2 * CHUNK]

    p1, tkk = {}, {}
    for ch in chains:
        t_b = t_inv[ch].astype(BF16)
        p1[ch] = _dot(t_b, _block_diag(akv[ch], bd_mask))
        tkk[ch] = _dot(t_b, _block_diag(kkt_q[ch], bd_mask))

    w_y0, w_qt, w_g, w_h0 = pwset
    for ch in chains:
        c, qd = ch
        w_y0[rsl(c), lsl(qd)] = bkv[ch] - _dot(b_b[ch], _block_diag(p1[ch], bd_mask))
        w_qt[rsl(c), lsl(qd)] = rt_q[ch] - _dot(b_b[ch], _block_diag(tkk[ch], bd_mask))
        e_q = r_et[c:c + 1, lsl(qd)]
        lhs_t = jnp.concatenate([r_kh[rsl(c), lsl(qd)] * e_q, r_bh[rsl(c), lsl(qd)] * e_q], axis=0).astype(BF16)
        rhs_t = jnp.concatenate([jnp.concatenate([v_q[ch], jnp.zeros_like(v_q[ch])], axis=1),
                                 jnp.concatenate([-p1[ch], -tkk[ch]], axis=1)], axis=0).astype(BF16)
        hg = _dot_tn(lhs_t, rhs_t)
        w_h0[rsl(c), lsl(qd)] = _diag_blocks(hg[:, 0:QUAD], head_of_lane)
        w_g[rsl(c), lsl(qd)] = _diag_blocks(hg[:, QUAD:2 * QUAD], head_of_lane) + jnp.where(on_diag, e_q, 0.0)
    issue_state(len(order))


def _rwkv_call(rw_lat, rw_ctx, weights, *, reverse, with_gate, batch, seq, ctx_len, acc=None):
    assert CHUNK == HEAD_DIM
    blk_rows = RW_BLOCK_CHUNKS * CHUNK
    assert ctx_len % blk_rows == 0 and seq % blk_rows == 0
    n_ctx = ctx_len // blk_rows
    n_lat = seq // blk_rows
    n_chunks = n_ctx + n_lat
    halo_per_blk = blk_rows // HALO

    def block_of(s):
        if reverse:
            return jnp.where(s < n_ctx, n_ctx - 1 - s, n_chunks + n_ctx - 1 - s)
        return s

    def stream_specs(first, count):
        n_halo = batch * count * halo_per_blk

        def blk(b, s):
            return b * count + jnp.clip(block_of(jnp.minimum(s, n_chunks - 1)) - first, 0, count - 1)

        return [pl.BlockSpec((blk_rows, RW_COLS), lambda b, s: (blk(b, s), 0)),
                pl.BlockSpec((HALO, RW_COLS), lambda b, s: (jnp.maximum(blk(b, s) * halo_per_blk - 1, 0), 0)),
                pl.BlockSpec((HALO, RW_COLS),
                             lambda b, s: (jnp.minimum((blk(b, s) + 1) * halo_per_blk, n_halo - 1), 0))]

    def out_block(b, step):
        lat = jnp.maximum(step, n_ctx) - n_ctx
        return (b * n_lat + (n_lat - 1 - lat if reverse else lat), 0)

    def const_spec(a):
        return pl.BlockSpec(a.shape, lambda b, s: (0,) * a.ndim)

    prep_spec = pl.BlockSpec((blk_rows, RW_WIDTH), lambda b, s: out_block(b, jnp.minimum(s, n_chunks - 1)))
    y_spec = pl.BlockSpec((blk_rows, RW_WIDTH), lambda b, s: out_block(b, jnp.maximum(s - 2, 0)))
    out_shape = jax.ShapeDtypeStruct((batch * seq, RW_WIDTH), F32)
    n_out = 3 if with_gate else 2
    assert with_gate == (acc is None)
    acc_specs = [] if acc is None else [y_spec, prep_spec]
    term_set = [pltpu.VMEM((blk_rows, RW_WIDTH), F32)] * 5 + [pltpu.VMEM((8, RW_WIDTH), F32)]
    chain_set = [pltpu.VMEM((blk_rows, RW_WIDTH), F32)] * 4
    return pl.pallas_call(
        functools.partial(_rwkv_kernel, reverse, with_gate, n_ctx, n_chunks),
        grid=(batch, n_chunks + 2),
        in_specs=stream_specs(n_ctx, n_lat) + stream_specs(0, n_ctx) + [const_spec(a) for a in weights] + acc_specs,
        out_specs=[y_spec] + [prep_spec] * (n_out - 1),
        out_shape=[out_shape] * n_out,
        scratch_shapes=[pltpu.VMEM((RW_WIDTH // QUAD, CHUNK, QUAD), F32)] + term_set * 2 + chain_set * 2,
        compiler_params=_cparams("parallel", "arbitrary"),
        name="rwkv_bwd" if reverse else "rwkv_fwd",
    )(rw_lat, rw_lat, rw_lat, rw_ctx, rw_ctx, rw_ctx, *weights, *(acc or ()))


def _tail_kernel(d_model, alpha, ff_chunk, yna_ref, y_ref, bonus_ref, gate_ref, x_ref, moda_ref, modc_ref,
                 gng_ref, gnb_ref, ones_ref, wo_ref, ln1g_ref, ln1b_ref, w1_ref, w2_ref, ln2g_ref, ln2b_ref,
                 o_ref, x1_a, xm_a, acc_a, x1_b, xm_b, acc_b):
    i = pl.program_id(0)

    @pl.when(i == 0)
    def _():
        for ref in (x1_a, xm_a, acc_a, x1_b, xm_b, acc_b):
            ref[...] = jnp.zeros_like(ref)

    def step(x1_cur, xm_cur, acc_cur, xm_prev, acc_prev):
        ones = ones_ref[...]
        inv_n = 1.0 / HEAD_DIM
        n_chunks = w1_ref.shape[1] // ff_chunk
        t = {}

        def readout_stats():
            y = y_ref[...]
            mu = _dot_lhs_split(y, ones, 1) * inv_n
            yc = y - mu
            t.update(yc=yc, var=_dot_lhs_split(yc * yc, ones, 1) * inv_n)

        def out_projection():
            yn = t["yc"] * lax.rsqrt(t["var"] + GN_EPS) * gng_ref[...] + gnb_ref[...]
            y_rw = ((yn + bonus_ref[...]) * gate_ref[...]).astype(BF16)
            t["proj"] = (_dot(yna_ref[...], wo_ref[0:NA_WIDTH, :])
                         + _dot(y_rw, wo_ref[NA_WIDTH:NA_WIDTH + RW_WIDTH, :]))

        half_rows = x_ref.shape[0] // 2

        def norm_modulate(part):
            rows = slice(part * half_rows, (part + 1) * half_rows)
            mod = moda_ref[...]
            g1 = mod[:, 2 * d_model:3 * d_model]
            shift = mod[:, 3 * d_model:4 * d_model]
            scale = mod[:, 4 * d_model:5 * d_model]
            x1 = (_normalize(alpha * x_ref[rows, :] + g1 * t["proj"][rows, :], LN_EPS) * ln1g_ref[...]
                  + ln1b_ref[...])
            x1_cur[rows, :] = x1
            xm_cur[rows, :] = (_normalize(x1, LN_EPS) * (1.0 + scale) + shift).astype(BF16)

        def final_norm(part):
            rows = slice(part * half_rows, (part + 1) * half_rows)
            g2 = modc_ref[...][:, 5 * d_model:6 * d_model]
            o_ref[rows, :] = (_normalize(alpha * x1_cur[rows, :] + g2 * acc_cur[rows, :], LN_EPS) * ln2g_ref[...]
                              + ln2b_ref[...])

        others = [(min(k, n_chunks - 1), stage) for k, stage in
                  ((0, readout_stats), (0, out_projection),
                   (1, functools.partial(final_norm, 0)), (1, functools.partial(norm_modulate, 0)),
                   (2, functools.partial(final_norm, 1)), (2, functools.partial(norm_modulate, 1)))]
        xm = xm_prev[...]
        acc = None
        for c in range(n_chunks):
            c0 = c * ff_chunk
            hid = jnp.maximum(_dot(xm, w1_ref[:, c0:c0 + ff_chunk]), 0.0)
            part = _dot((hid * hid).astype(BF16), w2_ref[c0:c0 + ff_chunk, :])
            acc = part if acc is None else acc + part
            for after, stage in others:
                if after == c:
                    stage()
        acc_prev[...] = acc

    @pl.when(i % 2 == 0)
    def _():
        step(x1_a, xm_a, acc_a, xm_b, acc_b)

    @pl.when(i % 2 == 1)
    def _():
        step(x1_b, xm_b, acc_b, xm_a, acc_a)


def _tail_call(y_na, y, bonus, gate, x2d, mod3, consts, *, tm, seq, alpha):
    rows, d = x2d.shape
    tiles_per_batch = seq // tm
    n_tiles = rows // tm

    def tile_a(i):
        return jnp.minimum(i, n_tiles - 1)

    def tile_c(i):
        return jnp.maximum(i - 2, 0)

    half = pl.BlockSpec((tm, RW_WIDTH), lambda i: (tile_a(i), 0))

    def resident(a):
        return pl.BlockSpec(a.shape, lambda i: (0,) * a.ndim, pipeline_mode=pl.Buffered(1))

    return pl.pallas_call(
        functools.partial(_tail_kernel, d, alpha, 1024),
        grid=(n_tiles + 2,),
        in_specs=[half] * 4 + [pl.BlockSpec((tm, d), lambda i: (tile_a(i), 0)),
                               pl.BlockSpec((None, 1, mod3.shape[2]), lambda i: (tile_a(i) // tiles_per_batch, 0, 0)),
                               pl.BlockSpec((None, 1, mod3.shape[2]), lambda i: (tile_c(i) // tiles_per_batch, 0, 0))]
                 + [resident(a) for a in consts],
        out_specs=pl.BlockSpec((tm, d), lambda i: (tile_c(i), 0)),
        out_shape=jax.ShapeDtypeStruct((rows, d), F32),
        scratch_shapes=[pltpu.VMEM((tm, d), F32), pltpu.VMEM((tm, d), BF16), pltpu.VMEM((tm, d), F32)] * 2,
        compiler_params=_cparams("arbitrary"),
        name="tail",
    )(y_na, y, bonus, gate, x2d, mod3, mod3, *consts)


def _rope_tables(seq):
    f = HEAD_DIM // 4
    t = np.arange(seq)
    row = (t // GRID_W).astype(np.float32)
    col = (t % GRID_W).astype(np.float32)
    inv = (ROPE_BASE ** (-np.arange(f, dtype=np.float32) / f)).astype(np.float32)
    ang_r = row[:, None] * inv[None, :]
    ang_c = col[:, None] * inv[None, :]
    zero = np.zeros_like(ang_r)
    cos = np.concatenate([np.cos(ang_r), np.cos(ang_r), np.cos(ang_c), np.cos(ang_c)], axis=1)
    sa = np.concatenate([-np.sin(ang_r), zero, -np.sin(ang_c), zero], axis=1)
    sb = np.concatenate([zero, np.sin(ang_r), zero, np.sin(ang_c)], axis=1)
    return tuple(jnp.asarray(np.concatenate([z, z], axis=1), F32) for z in (cos, sa, sb))


def _bias_table(rpb):
    qc = np.arange(GRID_W)[:, None]
    kc = np.arange(GRID_W)[None, :]
    c0 = np.clip(qc - WIN_W // 2, 0, GRID_W - WIN_W)
    inside = (kc >= c0) & (kc < c0 + WIN_W)
    pad = GRID_W - WIN_W
    padded = jnp.pad(rpb.astype(F32) * LOG2_E, ((0, 0), (0, 0), (pad, pad)))
    g = jnp.stack([padded[:, :, GRID_W - 1 - q:2 * GRID_W - 1 - q] for q in range(GRID_W)], axis=2)
    g = jnp.where(inside[None, None], g, NEG_BIAS)
    g = g.transpose(1, 0, 2, 3)
    return jnp.concatenate([g[:-1], g[1:]], axis=-1)


def _pad_lora(w, d):
    z = jnp.zeros_like(w[d])
    return jnp.concatenate([w[0] if d == 0 else z, w[1] if d == 1 else z], axis=0)


def kernel(x, c, ctx, c_ctx, w_mod, b_mod, w_in, w_out, ln1_g, ln1_b, mlp_w1, mlp_w2, ln2_g, ln2_b, na_rpb,
           rw_mu_prev, rw_mu_next, rw_w0, rw_w2, rw_a0, rw_a2, rw_g2, rw_k_k, rw_k_a, rw_r_k, rw_gn_g, rw_gn_b):
    depth = w_mod.shape[0]
    assert depth == 1, "single-layer trunk only (the context stream is never updated)"
    batch, seq, d = x.shape
    ctx_len = ctx.shape[1]
    alpha = (2 * depth) ** 0.25
    tm = ROW_TILE
    tm_ctx = math.gcd(ctx_len, ROW_TILE)
    assert seq % (Q_ROWS * GRID_W) == 0 and seq % tm == 0 and tm_ctx % 8 == 0
    assert batch + 1 <= 8

    cc = jnp.zeros((8, d), F32).at[:batch].set(c).at[batch].set(c_ctx)
    mod3 = _mod_call(cc, w_mod[0], b_mod[0]).reshape(8, 1, N_MOD * d)

    w_in_b = w_in[0].astype(BF16)
    tiles_lat = seq // tm
    x2d = x.reshape(batch * seq, d)
    q, qr, kr, v, rw_lat = _inproj_call(
        x2d, mod3, _rope_tables(seq), w_in_b, tm=tm,
        mod_index=lambda i: i // tiles_lat, table_index=lambda i: i % tiles_lat, name="inproj")
    ident = tuple(jnp.full((tm_ctx, 2 * HEAD_DIM), val, F32) for val in (1.0, 0.0, 0.0))
    _, _, kc, vc, rw_ctx = _inproj_call(
        ctx.reshape(batch * ctx_len, d), mod3, ident, w_in_b, tm=tm_ctx,
        mod_index=lambda i: batch, table_index=lambda i: 0, name="inproj_ctx")

    y_na = _na_call(q, qr, kr, v, kc, vc, _bias_table(na_rpb[0]), batch=batch, seq=seq, ctx_len=ctx_len)

    lane_head = jnp.arange(RW_WIDTH) // HEAD_DIM
    ones = (lane_head[:, None] == lane_head[None, :]).astype(BF16)
    row = lambda a: a.reshape(1, -1).astype(F32)
    def dir_weights(dirn):
        return (row(rw_mu_prev[0]), row(rw_mu_next[0]), row(rw_w0[0, dirn]),
                _pad_lora(rw_w2[0], dirn).astype(BF16), row(rw_a0[0, dirn]),
                _pad_lora(rw_a2[0], dirn).astype(BF16), rw_g2[0].astype(BF16),
                row(rw_k_k[0]), row(rw_k_a[0]), row(rw_r_k[0]), ones)

    y_f, bonus_f, gate = _rwkv_call(rw_lat, rw_ctx, dir_weights(0), reverse=False, with_gate=True,
                                    batch=batch, seq=seq, ctx_len=ctx_len)
    y_sum, bonus_sum = _rwkv_call(rw_lat, rw_ctx, dir_weights(1), reverse=True, with_gate=False,
                                  batch=batch, seq=seq, ctx_len=ctx_len, acc=(y_f, bonus_f))

    consts = (row(rw_gn_g[0]), row(rw_gn_b[0]), ones, w_out[0].astype(BF16), row(ln1_g[0]), row(ln1_b[0]),
              mlp_w1[0].astype(BF16), mlp_w2[0].astype(BF16), row(ln2_g[0]), row(ln2_b[0]))
    out = _tail_call(y_na, y_sum, bonus_sum, gate, x2d, mod3, consts, tm=tm, seq=seq, alpha=alpha)
    return out.reshape(batch, seq, d)
```

```python
import functools
import math

import jax
import jax.numpy as jnp
import numpy as np
from jax import lax
from jax.experimental import pallas as pl
from jax.experimental.pallas import tpu as pltpu

HEAD_DIM = 64
NA_HEADS = 8
RW_HEADS = 8
NA_WIDTH = NA_HEADS * HEAD_DIM
RW_WIDTH = RW_HEADS * HEAD_DIM
GRID_W = 64
WIN_H = 8
WIN_W = 16
ROPE_BASE = 10000.0
N_DIR = 2
DECAY_LORA = 64
AAA_LORA = 64
GATE_LORA = 128
LORA_W = N_DIR * DECAY_LORA
NA_COLS = 3 * NA_WIDTH
RW_COLS = 3 * RW_WIDTH + 3 * LORA_W
N_MOD = 6
LN_EPS = 1e-6
GN_EPS = 64e-5
NEG_BIAS = -1e30

CHUNK = 64
RW_BLOCK_CHUNKS = 4
QUAD = 2 * HEAD_DIM
HALO = 8
LOG2_E = math.log2(math.e)
Q_ROWS = 8
NA_UNROLL = 4
NA_LEAD = 4
KV_GROUP = 4
ROW_TILE = 512
FF_CHUNK = 1024
MOD_COL_TILE = 1536
SUBLANES = 8
VMEM_LIMIT = 56 * 1024 * 1024

F32 = jnp.float32
BF16 = jnp.bfloat16


def _cparams(*sem):
    return pltpu.CompilerParams(dimension_semantics=sem, vmem_limit_bytes=VMEM_LIMIT)


def _dot(a, b):
    return jnp.dot(a, b, preferred_element_type=F32)


def _dot_nt(a, b):
    return lax.dot_general(a, b, (((1,), (1,)), ((), ())), preferred_element_type=F32)


def _dot_tn(a, b):
    return lax.dot_general(a, b, (((0,), (0,)), ((), ())), preferred_element_type=F32)


def _split(x, pieces):
    out = []
    for _ in range(pieces):
        p = x.astype(BF16)
        out.append(p)
        x = x - p.astype(F32)
    return out


def _dot_lhs_split(x, w_bf16, pieces):
    acc = None
    for p in _split(x, pieces):
        t = _dot(p, w_bf16)
        acc = t if acc is None else acc + t
    return acc


def _normalize(x, eps):
    mu = jnp.mean(x, axis=-1, keepdims=True)
    xc = x - mu
    var = jnp.mean(xc * xc, axis=-1, keepdims=True)
    return xc * lax.rsqrt(var + eps)


def _mod_kernel(c_ref, w_ref, b_ref, o_ref):
    c = c_ref[...]
    s = c * jax.nn.sigmoid(c)
    o_ref[...] = _dot(s.astype(BF16), w_ref[...].astype(BF16)) + b_ref[...]


def _mod_call(cc, w_mod, b_mod):
    rows, d = cc.shape
    n = w_mod.shape[1]
    tn = MOD_COL_TILE
    assert n % tn == 0
    return pl.pallas_call(
        _mod_kernel,
        grid=(n // tn,),
        in_specs=[pl.BlockSpec((rows, d), lambda j: (0, 0)),
                  pl.BlockSpec((d, tn), lambda j: (0, j)),
                  pl.BlockSpec((1, tn), lambda j: (0, j))],
        out_specs=pl.BlockSpec((rows, tn), lambda j: (0, j)),
        out_shape=jax.ShapeDtypeStruct((rows, n), F32),
        compiler_params=_cparams("arbitrary"),
        name="mod",
    )(cc, w_mod, b_mod.reshape(1, n))


def _inproj_kernel(d_model, x_ref, mod_ref, cos_ref, sa_ref, sb_ref, w_ref, q_ref, qr_ref, kr_ref, v_ref, rw_ref):
    x = x_ref[...]
    mod = mod_ref[...]
    shift = mod[:, 0:d_model]
    scale = mod[:, d_model:2 * d_model]
    xm = (_normalize(x, LN_EPS) * (1.0 + scale) + shift).astype(BF16)

    reps = NA_WIDTH // cos_ref.shape[1]
    cos = jnp.concatenate([cos_ref[...]] * reps, axis=1)
    sa = jnp.concatenate([sa_ref[...]] * reps, axis=1)
    sb = jnp.concatenate([sb_ref[...]] * reps, axis=1)

    def rope(z):
        up = pltpu.roll(z, NA_WIDTH - HEAD_DIM // 4, 1)
        down = pltpu.roll(z, HEAD_DIM // 4, 1)
        return z * cos + up * sa + down * sb

    qk_scale = HEAD_DIM ** -0.5 * LOG2_E
    q = _dot(xm, w_ref[:, 0:NA_WIDTH]) * qk_scale
    q_ref[...] = q.astype(BF16)
    qr_ref[...] = rope(q).astype(BF16)
    k = _dot(xm, w_ref[:, NA_WIDTH:2 * NA_WIDTH])
    kr_ref[...] = rope(k).astype(BF16)
    v_ref[...] = _dot(xm, w_ref[:, 2 * NA_WIDTH:NA_COLS]).astype(BF16)
    rw_ref[...] = _dot(xm, w_ref[:, NA_COLS:NA_COLS + RW_COLS])


def _inproj_call(x2d, mod3, tables, w_in_bf16, *, tm, mod_index, table_index, name):
    rows, d = x2d.shape
    cos, sa, sb = tables
    tw = cos.shape[1]
    ncols = w_in_bf16.shape[1]
    na_spec = pl.BlockSpec((tm, NA_WIDTH), lambda i: (i, 0))
    table_spec = pl.BlockSpec((tm, tw), lambda i: (table_index(i), 0))
    na_shape = jax.ShapeDtypeStruct((rows, NA_WIDTH), BF16)
    return pl.pallas_call(
        functools.partial(_inproj_kernel, d),
        grid=(rows // tm,),
        in_specs=[pl.BlockSpec((tm, d), lambda i: (i, 0)),
                  pl.BlockSpec((None, 1, mod3.shape[2]), lambda i: (mod_index(i), 0, 0)),
                  table_spec, table_spec, table_spec,
                  pl.BlockSpec((d, ncols), lambda i: (0, 0))],
        out_specs=[na_spec, na_spec, na_spec, na_spec, pl.BlockSpec((tm, RW_COLS), lambda i: (i, 0))],
        out_shape=[na_shape, na_shape, na_shape, na_shape, jax.ShapeDtypeStruct((rows, RW_COLS), F32)],
        compiler_params=_cparams("parallel"),
        name=name,
    )(x2d, mod3, cos, sa, sb, w_in_bf16)


def _build_bias_table(rpb_ref, bias_ref):
    n_rows = 2 * WIN_H - 1
    lane = lax.broadcasted_iota(jnp.int32, (GRID_W, 2 * GRID_W), 1)
    qcol = lax.broadcasted_iota(jnp.int32, (GRID_W, 2 * GRID_W), 0)
    kcol = lane % GRID_W
    c0 = jnp.clip(qcol - WIN_W // 2, 0, GRID_W - WIN_W)
    inside = jnp.logical_and(kcol >= c0, kcol < c0 + WIN_W)
    first_half = lane < GRID_W
    for h in range(NA_HEADS):
        def toeplitz(ro, shift):
            row = jnp.broadcast_to(rpb_ref[ro, h:h + 1, :], (GRID_W, 2 * GRID_W))
            return pltpu.roll(row, shift, 1, stride=1, stride_axis=0)
        for ro in range(n_rows - 1):
            pair = jnp.where(first_half, toeplitz(ro, GRID_W + 1), toeplitz(ro + 1, 1))
            bias_ref[ro, h] = jnp.where(inside, pair, NEG_BIAS)


def _na_kernel(grid_rows, q_ref, qr_ref, k0, k1, k2, k3, v0, v1, v2, v3, kc_ref, vc_ref, rpb_ref,
               o_ref, kwin, vwin, bias_ref):
    i = pl.program_id(1)

    @pl.when(i == 0)
    def _():
        _build_bias_table(rpb_ref, bias_ref)

    grp = KV_GROUP * GRID_W
    for g, (kr, vr) in enumerate(((k0, v0), (k1, v1), (k2, v2), (k3, v3))):
        kwin[g * grp:(g + 1) * grp, :] = kr[...]
        vwin[g * grp:(g + 1) * grp, :] = vr[...]
    n_groups = grid_rows // KV_GROUP
    win_row0 = KV_GROUP * jnp.clip(2 * i - 1, 0, n_groups - 4)
    win_keys = WIN_H * GRID_W
    lane = lax.broadcasted_iota(jnp.int32, (GRID_W, 2 * HEAD_DIM), 1)
    even = lane < HEAD_DIM

    pairs = range(NA_HEADS // 2)
    lanes = [slice(pair * 2 * HEAD_DIM, (pair + 1) * 2 * HEAD_DIM) for pair in pairs]

    def rows_body(jj, carry):
        units = []
        for u in range(NA_UNROLL):
            j = NA_UNROLL * jj + u
            irow = Q_ROWS * i + j
            r0 = jnp.clip(irow - WIN_H // 2, 0, grid_rows - WIN_H)
            koff = pl.multiple_of((r0 - win_row0) * GRID_W, GRID_W)
            brow = (WIN_H - 1) - (irow - r0)
            qoff = pl.multiple_of(j * GRID_W, GRID_W)
            units += [(koff, brow, qoff, pair) for pair in pairs]
        def split_heads(z):
            zero = jnp.zeros_like(z)
            return jnp.concatenate([jnp.where(even, z, zero), jnp.where(even, zero, z)], axis=0)

        def scores(unit):
            koff, brow, qoff, pair = unit
            bias = jnp.concatenate(
                [jnp.concatenate([bias_ref[brow + 2 * m, 2 * pair], bias_ref[brow + 2 * m, 2 * pair + 1]], axis=0)
                 for m in range(WIN_H // 2)], axis=1)
            s_l = _dot_nt(split_heads(qr_ref[pl.ds(qoff, GRID_W), lanes[pair]]),
                          kwin[pl.ds(koff, win_keys), lanes[pair]]) + bias
            s_c = _dot_nt(split_heads(q_ref[pl.ds(qoff, GRID_W), lanes[pair]]), kc_ref[:, lanes[pair]])
            return s_l, s_c

        def attend(unit, s_l, s_c):
            koff, brow, qoff, pair = unit
            mx = jnp.maximum(jnp.max(s_l, axis=-1, keepdims=True), jnp.max(s_c, axis=-1, keepdims=True))
            p_l = jnp.exp2(s_l - mx)
            p_c = jnp.exp2(s_c - mx)
            denom = jnp.sum(p_l, axis=-1, keepdims=True) + jnp.sum(p_c, axis=-1, keepdims=True)
            o = (_dot(p_l.astype(BF16), vwin[pl.ds(koff, win_keys), lanes[pair]])
                 + _dot(p_c.astype(BF16), vc_ref[:, lanes[pair]])) / denom
            o_ref[pl.ds(qoff, GRID_W), lanes[pair]] = jnp.where(
                even, o[0:GRID_W], o[GRID_W:2 * GRID_W]).astype(BF16)

        pending = [scores(unit) for unit in units[:NA_LEAD]]
        for n, unit in enumerate(units):
            if n + NA_LEAD < len(units):
                pending.append(scores(units[n + NA_LEAD]))
            attend(unit, *pending[n])
        return carry

    lax.fori_loop(0, Q_ROWS // NA_UNROLL, rows_body, 0)


def _na_call(q, qr, kr, v, kc, vc, rpb_rows, *, batch, seq, ctx_len):
    grid_rows = seq // GRID_W
    n_blocks = grid_rows // Q_ROWS
    n_groups = grid_rows // KV_GROUP
    blk = Q_ROWS * GRID_W
    grp = KV_GROUP * GRID_W

    def group_spec(g):
        return pl.BlockSpec((grp, NA_WIDTH),
                            lambda b, i: (b * n_groups + jnp.clip(2 * i - 1, 0, n_groups - 4) + g, 0))

    q_spec = pl.BlockSpec((blk, NA_WIDTH), lambda b, i: (b * n_blocks + i, 0))
    c_spec = pl.BlockSpec((ctx_len, NA_WIDTH), lambda b, i: (b, 0))
    return pl.pallas_call(
        functools.partial(_na_kernel, grid_rows),
        grid=(batch, n_blocks),
        in_specs=[q_spec, q_spec] + [group_spec(g) for g in range(4)] + [group_spec(g) for g in range(4)]
                 + [c_spec, c_spec, pl.BlockSpec(rpb_rows.shape, lambda b, i: (0, 0, 0))],
        out_specs=q_spec,
        out_shape=jax.ShapeDtypeStruct((batch * seq, NA_WIDTH), BF16),
        scratch_shapes=[pltpu.VMEM((4 * grp, NA_WIDTH), BF16), pltpu.VMEM((4 * grp, NA_WIDTH), BF16),
                        pltpu.VMEM((2 * WIN_H - 2, NA_HEADS, GRID_W, 2 * GRID_W), F32)],
        compiler_params=_cparams("parallel", "arbitrary"),
        name="natten",
    )(q, qr, kr, kr, kr, kr, v, v, v, v, kc, vc, rpb_rows)


def _block_diag(x_cat, bd_mask):
    xb = x_cat.astype(BF16)
    tiled = jnp.concatenate([xb] * (QUAD // CHUNK), axis=0)
    return jnp.where(bd_mask, tiled, jnp.zeros_like(tiled))


def _diag_blocks(x, head_of_lane):
    n = QUAD // HEAD_DIM
    out = x[(n - 1) * HEAD_DIM:n * HEAD_DIM]
    for h in range(n - 2, -1, -1):
        out = jnp.where(head_of_lane == h, x[h * HEAD_DIM:(h + 1) * HEAD_DIM], out)
    return out


def _rwkv_kernel(reverse, with_gate, n_ctx, n_blocks,
                 lat_ref, lat_prev_ref, lat_next_ref, ctx_ref, ctx_prev_ref, ctx_next_ref,
                 mup_ref, mun_ref, w0_ref, w2_ref, a0_ref, a2_ref, g2_ref,
                 kk_ref, ka_ref, rk_ref, ones_ref, *refs):
    if with_gate:
        acc_refs = (None, None)
        y_ref, bonus_ref, gate_ref = refs[0:3]
        refs = refs[3:]
    else:
        acc_refs = refs[0:2]
        y_ref, bonus_ref = refs[2:4]
        gate_ref = None
        refs = refs[4:]
    h_ref = refs[0]
    sets = (refs[1:7], refs[7:13])
    psets = (refs[13:17], refs[17:21])
    s = pl.program_id(1)

    @pl.when(s == 0)
    def _():
        h_ref[...] = jnp.zeros_like(h_ref)
        for ref in sets[1] + psets[1]:
            ref[...] = jnp.zeros_like(ref)

    @pl.when(s % 2 == 0)
    def _():
        _rwkv_step(reverse, n_ctx, n_blocks, s, sets[0], sets[1], psets[0], psets[1],
                   lat_ref, lat_prev_ref, lat_next_ref,
                   ctx_ref, ctx_prev_ref, ctx_next_ref, mup_ref, mun_ref, w0_ref, w2_ref, a0_ref, a2_ref,
                   g2_ref, kk_ref, ka_ref, rk_ref, ones_ref, acc_refs, y_ref, bonus_ref, gate_ref, h_ref)

    @pl.when(s % 2 == 1)
    def _():
        _rwkv_step(reverse, n_ctx, n_blocks, s, sets[1], sets[0], psets[1], psets[0],
                   lat_ref, lat_prev_ref, lat_next_ref,
                   ctx_ref, ctx_prev_ref, ctx_next_ref, mup_ref, mun_ref, w0_ref, w2_ref, a0_ref, a2_ref,
                   g2_ref, kk_ref, ka_ref, rk_ref, ones_ref, acc_refs, y_ref, bonus_ref, gate_ref, h_ref)


def _rwkv_step(reverse, n_ctx, n_blocks, s, wset, rset, pwset, prset, lat_ref, lat_prev_ref, lat_next_ref,
               ctx_ref, ctx_prev_ref, ctx_next_ref, mup_ref, mun_ref, w0_ref, w2_ref, a0_ref, a2_ref,
               g2_ref, kk_ref, ka_ref, rk_ref, ones_ref, acc_refs, y_ref, bonus_ref, gate_ref, h_ref):
    sp = jnp.minimum(s, n_blocks - 1)
    if reverse:
        n = jnp.where(sp < n_ctx, n_ctx - 1 - sp, n_blocks + n_ctx - 1 - sp)
    else:
        n = sp

    rows = lat_ref.shape[0]
    n_ch = rows // CHUNK
    w = RW_WIDTH
    t = {}

    is_ctx = n < n_ctx
    has_prev = jnp.logical_and(n != 0, n != n_ctx)
    has_next = jnp.logical_and(n != n_ctx - 1, n != n_blocks - 1)
    row = lax.broadcasted_iota(jnp.int32, (rows, 1), 0)

    def shifted(c0, c1):
        p = jnp.where(is_ctx, ctx_ref[:, c0:c1], lat_ref[:, c0:c1])
        prow = jnp.where(is_ctx, ctx_prev_ref[HALO - 1:HALO, c0:c1], lat_prev_ref[HALO - 1:HALO, c0:c1])
        nrow = jnp.where(is_ctx, ctx_next_ref[0:1, c0:c1], lat_next_ref[0:1, c0:c1])
        prow = jnp.where(has_prev, prow, 0.0)
        nrow = jnp.where(has_next, nrow, 0.0)
        prev = jnp.where(row == 0, prow, pltpu.roll(p, 1, 0))
        nxt = jnp.where(row == rows - 1, nrow, pltpu.roll(p, rows - 1, 0))
        return p + mup_ref[:, c0:c1] * (prev - p) + mun_ref[:, c0:c1] * (nxt - p)

    def prep_lora():
        lora = shifted(3 * w, 3 * w + 3 * LORA_W)
        pw = lora[:, 0:LORA_W]
        pa = lora[:, LORA_W:2 * LORA_W]
        t["log_decay"] = -math.exp(-0.5) * jax.nn.sigmoid(
            w0_ref[...] + _dot(jnp.tanh(pw).astype(BF16), w2_ref[...]))
        t["a"] = jax.nn.sigmoid(a0_ref[...] + _dot(pa.astype(BF16), a2_ref[...]))
        if gate_ref is not None:
            gate_ref[...] = _dot(jax.nn.sigmoid(lora[:, 2 * LORA_W:3 * LORA_W]).astype(BF16), g2_ref[...])

    def prep_keys():
        k = shifted(w, 2 * w)
        ones = ones_ref[...]
        kk_raw = k * kk_ref[...]
        kk = kk_raw * lax.rsqrt(jnp.maximum(_dot_lhs_split(kk_raw * kk_raw, ones, 1), 1e-24))
        t.update(kk=kk, kd=k * (1.0 + (t["a"] - 1.0) * ka_ref[...]), bb=t["a"] * kk)

    def prep_bonus():
        r = shifted(0, w)
        v = shifted(2 * w, 3 * w)
        bonus = _dot_lhs_split(r * t["kd"] * rk_ref[...], ones_ref[...], 1) * v
        bonus_ref[...] = bonus if acc_refs[1] is None else bonus + acc_refs[1][...]
        t.update(r=r, v=v)

    def finish_prep():
        r, v, kk, kd, bb, log_decay = (t[name] for name in ("r", "v", "kk", "kd", "bb", "log_decay"))
        ti = lax.broadcasted_iota(jnp.int32, (rows, rows), 0)
        si = lax.broadcasted_iota(jnp.int32, (rows, rows), 1)
        same_chunk = (ti // CHUNK) == (si // CHUNK)
        tri = jnp.where(jnp.logical_and(same_chunk, (si >= ti) if reverse else (si <= ti)), 1.0, 0.0).astype(BF16)
        lw_hi, lw_lo = _split(log_decay, 2)
        cum = _dot(tri, lw_hi) + _dot(tri, lw_lo)
        e_neg = jnp.exp(-cum)
        last = 0 if reverse else CHUNK - 1
        w_v, w_kkt, w_rt, w_kh, w_bh, w_et = wset
        w_v[...] = v
        w_kkt[...] = kk * jnp.exp(cum - log_decay)
        w_rt[...] = r * jnp.exp(cum)
        w_kh[...] = kd * e_neg
        w_bh[...] = bb * e_neg
        for c in range(n_ch):
            w_et[c:c + 1, :] = jnp.exp(cum[c * CHUNK + last:c * CHUNK + last + 1, :])

    prep_at = {"start": prep_lora, "gram": prep_keys, "akv": prep_bonus, "dbl1": finish_prep}

    def issue_prep(point):
        if point in prep_at:
            prep_at[point]()

    issue_prep("start")
    r_v, r_kkt, r_rt, r_kh, r_bh, r_et = rset
    trow = lax.broadcasted_iota(jnp.int32, (CHUNK, QUAD), 0)
    tcol = lax.broadcasted_iota(jnp.int32, (CHUNK, QUAD), 1) % CHUNK
    strict = (trow < tcol) if reverse else (trow > tcol)
    incl = (trow <= tcol) if reverse else (trow >= tcol)
    on_diag = trow == tcol
    eye_cat = jnp.where(on_diag, 1.0, 0.0)
    brow = lax.broadcasted_iota(jnp.int32, (QUAD, QUAD), 0)
    bcol = lax.broadcasted_iota(jnp.int32, (QUAD, QUAD), 1)
    bd_mask = (brow // HEAD_DIM) == (bcol // HEAD_DIM)
    head_of_lane = lax.broadcasted_iota(jnp.int32, (CHUNK, QUAD), 1) // HEAD_DIM
    n_doublings = int(math.log2(CHUNK)) - 1
    heads = range(QUAD // HEAD_DIM)
    quads = range(RW_WIDTH // QUAD)
    order = list(range(n_ch - 1, -1, -1) if reverse else range(n_ch))

    chains = [(c, qd) for c in order for qd in quads]

    def rsl(c):
        return slice(c * CHUNK, (c + 1) * CHUNK)

    def lsl(qd):
        return slice(qd * QUAD, (qd + 1) * QUAD)

    r_y0, r_qt, r_g, r_h0 = prset
    h = {qd: h_ref[qd] for qd in quads}
    issued = [0]

    def issue_state(upto):
        while issued[0] < min(upto, len(order)):
            c = order[issued[0]]
            issued[0] += 1
            for qd in quads:
                lhs = jnp.concatenate([r_g[rsl(c), lsl(qd)], r_qt[rsl(c), lsl(qd)]], axis=0).astype(BF16)
                o2 = _dot(lhs, _block_diag(h[qd], bd_mask))
                h[qd] = o2[0:CHUNK] + r_h0[rsl(c), lsl(qd)]
                y = r_y0[rsl(c), lsl(qd)] + o2[CHUNK:2 * CHUNK]
                y_ref[rsl(c), lsl(qd)] = y if acc_refs[0] is None else y + acc_refs[0][rsl(c), lsl(qd)]
            if issued[0] == len(order):
                for qd in quads:
                    h_ref[qd] = h[qd]

    issue_state(1)
    v_q = {ch: r_v[rsl(ch[0]), lsl(ch[1])] for ch in chains}
    kkt_q = {ch: r_kkt[rsl(ch[0]), lsl(ch[1])] for ch in chains}
    rt_q = {ch: r_rt[rsl(ch[0]), lsl(ch[1])] for ch in chains}

    a_k, a_b, b_k, b_b = {}, {}, {}, {}
    for ch in chains:
        kh = r_kh[rsl(ch[0]), lsl(ch[1])].astype(BF16)
        bh = r_bh[rsl(ch[0]), lsl(ch[1])].astype(BF16)
        zero = jnp.zeros_like(kh)
        rhs = jnp.concatenate([jnp.where(head_of_lane == h, kh, zero) for h in heads]
                              + [jnp.where(head_of_lane == h, bh, zero) for h in heads], axis=0)
        lhs = jnp.concatenate([kkt_q[ch], rt_q[ch]], axis=0).astype(BF16)
        gram = _dot_nt(lhs, rhs)
        a_k[ch] = jnp.where(strict, gram[0:CHUNK, 0:QUAD], 0.0)
        a_b[ch] = jnp.where(strict, gram[0:CHUNK, QUAD:2 * QUAD], 0.0)
        b_k[ch] = jnp.where(incl, gram[CHUNK:2 * CHUNK, 0:QUAD], 0.0)
        b_b[ch] = jnp.where(incl, gram[CHUNK:2 * CHUNK, QUAD:2 * QUAD], 0.0).astype(BF16)
    issue_prep("gram")
    issue_state(2)

    akv, bkv = {}, {}
    for ch in chains:
        both = _dot(jnp.concatenate([a_k[ch], b_k[ch]], axis=0).astype(BF16), _block_diag(v_q[ch], bd_mask))
        akv[ch] = both[0:CHUNK]
        bkv[ch] = both[CHUNK:2 * CHUNK]
    issue_prep("akv")
    issue_state(3)

    m = {ch: -a_b[ch] for ch in chains}
    t_inv = {ch: eye_cat + m[ch] for ch in chains}
    for ch in chains:
        m[ch] = _dot(m[ch].astype(BF16), _block_diag(m[ch], bd_mask))
    issue_prep("m1")
    issue_state(4)
    for step in range(n_doublings):
        final = step == n_doublings - 1
        issue_prep("dbl%d" % step)
        for ch in chains:
            m_bd = _block_diag(m[ch], bd_mask)
            if final:
                t_inv[ch] = t_inv[ch] + _dot(t_inv[ch].astype(BF16), m_bd)
            else:
                both = _dot(jnp.concatenate([m[ch], t_inv[ch]], axis=0).astype(BF16), m_bd)
                m[ch] = both[0:CHUNK]
                t_inv[ch] = t_inv[ch] + both[CHUNK:2 * CHUNK]

    p1, tkk = {}, {}
    for ch in chains:
        t_b = t_inv[ch].astype(BF16)
        p1[ch] = _dot(t_b, _block_diag(akv[ch], bd_mask))
        tkk[ch] = _dot(t_b, _block_diag(kkt_q[ch], bd_mask))

    w_y0, w_qt, w_g, w_h0 = pwset
    for ch in chains:
        c, qd = ch
        w_y0[rsl(c), lsl(qd)] = bkv[ch] - _dot(b_b[ch], _block_diag(p1[ch], bd_mask))
        w_qt[rsl(c), lsl(qd)] = rt_q[ch] - _dot(b_b[ch], _block_diag(tkk[ch], bd_mask))
        e_q = r_et[c:c + 1, lsl(qd)]
        lhs_t = jnp.concatenate([r_kh[rsl(c), lsl(qd)] * e_q, r_bh[rsl(c), lsl(qd)] * e_q], axis=0).astype(BF16)
        rhs_t = jnp.concatenate([jnp.concatenate([v_q[ch], jnp.zeros_like(v_q[ch])], axis=1),
                                 jnp.concatenate([-p1[ch], -tkk[ch]], axis=1)], axis=0).astype(BF16)
        hg = _dot_tn(lhs_t, rhs_t)
        w_h0[rsl(c), lsl(qd)] = _diag_blocks(hg[:, 0:QUAD], head_of_lane)
        w_g[rsl(c), lsl(qd)] = _diag_blocks(hg[:, QUAD:2 * QUAD], head_of_lane) + jnp.where(on_diag, e_q, 0.0)
    issue_state(len(order))


def _rwkv_call(rw_lat, rw_ctx, weights, *, reverse, with_gate, batch, seq, ctx_len, acc=None):
    assert CHUNK == HEAD_DIM
    blk_rows = RW_BLOCK_CHUNKS * CHUNK
    assert ctx_len % blk_rows == 0 and seq % blk_rows == 0
    n_ctx = ctx_len // blk_rows
    n_lat = seq // blk_rows
    n_chunks = n_ctx + n_lat
    halo_per_blk = blk_rows // HALO

    def block_of(s):
        if reverse:
            return jnp.where(s < n_ctx, n_ctx - 1 - s, n_chunks + n_ctx - 1 - s)
        return s

    def stream_specs(first, count):
        n_halo = batch * count * halo_per_blk

        def blk(b, s):
            return b * count + jnp.clip(block_of(jnp.minimum(s, n_chunks - 1)) - first, 0, count - 1)

        return [pl.BlockSpec((blk_rows, RW_COLS), lambda b, s: (blk(b, s), 0)),
                pl.BlockSpec((HALO, RW_COLS), lambda b, s: (jnp.maximum(blk(b, s) * halo_per_blk - 1, 0), 0)),
                pl.BlockSpec((HALO, RW_COLS),
                             lambda b, s: (jnp.minimum((blk(b, s) + 1) * halo_per_blk, n_halo - 1), 0))]

    def out_block(b, step):
        lat = jnp.maximum(step, n_ctx) - n_ctx
        return (b * n_lat + (n_lat - 1 - lat if reverse else lat), 0)

    def const_spec(a):
        return pl.BlockSpec(a.shape, lambda b, s: (0,) * a.ndim)

    prep_spec = pl.BlockSpec((blk_rows, RW_WIDTH), lambda b, s: out_block(b, jnp.minimum(s, n_chunks - 1)))
    y_spec = pl.BlockSpec((blk_rows, RW_WIDTH), lambda b, s: out_block(b, jnp.maximum(s - 2, 0)))
    out_shape = jax.ShapeDtypeStruct((batch * seq, RW_WIDTH), F32)
    n_out = 3 if with_gate else 2
    assert with_gate == (acc is None)
    acc_specs = [] if acc is None else [y_spec, prep_spec]
    assert RW_BLOCK_CHUNKS <= SUBLANES
    term_set = [pltpu.VMEM((blk_rows, RW_WIDTH), F32)] * 5 + [pltpu.VMEM((SUBLANES, RW_WIDTH), F32)]
    chain_set = [pltpu.VMEM((blk_rows, RW_WIDTH), F32)] * 4
    return pl.pallas_call(
        functools.partial(_rwkv_kernel, reverse, with_gate, n_ctx, n_chunks),
        grid=(batch, n_chunks + 2),
        in_specs=stream_specs(n_ctx, n_lat) + stream_specs(0, n_ctx) + [const_spec(a) for a in weights] + acc_specs,
        out_specs=[y_spec] + [prep_spec] * (n_out - 1),
        out_shape=[out_shape] * n_out,
        scratch_shapes=[pltpu.VMEM((RW_WIDTH // QUAD, CHUNK, QUAD), F32)] + term_set * 2 + chain_set * 2,
        compiler_params=_cparams("parallel", "arbitrary"),
        name="rwkv_bwd" if reverse else "rwkv_fwd",
    )(rw_lat, rw_lat, rw_lat, rw_ctx, rw_ctx, rw_ctx, *weights, *(acc or ()))


def _tail_kernel(d_model, alpha, ff_chunk, yna_ref, y_ref, bonus_ref, gate_ref, x_ref, moda_ref, modc_ref,
                 gng_ref, gnb_ref, ones_ref, wo_ref, ln1g_ref, ln1b_ref, w1_ref, w2_ref, ln2g_ref, ln2b_ref,
                 o_ref, x1_a, xm_a, acc_a, x1_b, xm_b, acc_b):
    i = pl.program_id(0)

    @pl.when(i == 0)
    def _():
        for ref in (x1_a, xm_a, acc_a, x1_b, xm_b, acc_b):
            ref[...] = jnp.zeros_like(ref)

    def step(x1_cur, xm_cur, acc_cur, xm_prev, acc_prev):
        ones = ones_ref[...]
        inv_n = 1.0 / HEAD_DIM
        n_chunks = w1_ref.shape[1] // ff_chunk
        t = {}

        def readout_stats():
            y = y_ref[...]
            mu = _dot_lhs_split(y, ones, 1) * inv_n
            yc = y - mu
            t.update(yc=yc, var=_dot_lhs_split(yc * yc, ones, 1) * inv_n)

        def out_projection():
            yn = t["yc"] * lax.rsqrt(t["var"] + GN_EPS) * gng_ref[...] + gnb_ref[...]
            y_rw = ((yn + bonus_ref[...]) * gate_ref[...]).astype(BF16)
            t["proj"] = (_dot(yna_ref[...], wo_ref[0:NA_WIDTH, :])
                         + _dot(y_rw, wo_ref[NA_WIDTH:NA_WIDTH + RW_WIDTH, :]))

        half_rows = x_ref.shape[0] // 2

        def norm_modulate(part):
            rows = slice(part * half_rows, (part + 1) * half_rows)
            mod = moda_ref[...]
            g1 = mod[:, 2 * d_model:3 * d_model]
            shift = mod[:, 3 * d_model:4 * d_model]
            scale = mod[:, 4 * d_model:5 * d_model]
            x1 = (_normalize(alpha * x_ref[rows, :] + g1 * t["proj"][rows, :], LN_EPS) * ln1g_ref[...]
                  + ln1b_ref[...])
            x1_cur[rows, :] = x1
            xm_cur[rows, :] = (_normalize(x1, LN_EPS) * (1.0 + scale) + shift).astype(BF16)

        def final_norm(part):
            rows = slice(part * half_rows, (part + 1) * half_rows)
            g2 = modc_ref[...][:, 5 * d_model:6 * d_model]
            o_ref[rows, :] = (_normalize(alpha * x1_cur[rows, :] + g2 * acc_cur[rows, :], LN_EPS) * ln2g_ref[...]
                              + ln2b_ref[...])

        others = [(min(k, n_chunks - 1), stage) for k, stage in
                  ((0, readout_stats), (0, out_projection),
                   (1, functools.partial(final_norm, 0)), (1, functools.partial(norm_modulate, 0)),
                   (2, functools.partial(final_norm, 1)), (2, functools.partial(norm_modulate, 1)))]
        xm = xm_prev[...]
        acc = None
        for c in range(n_chunks):
            c0 = c * ff_chunk
            hid = jnp.maximum(_dot(xm, w1_ref[:, c0:c0 + ff_chunk]), 0.0)
            part = _dot((hid * hid).astype(BF16), w2_ref[c0:c0 + ff_chunk, :])
            acc = part if acc is None else acc + part
            for after, stage in others:
                if after == c:
                    stage()
        acc_prev[...] = acc

    @pl.when(i % 2 == 0)
    def _():
        step(x1_a, xm_a, acc_a, xm_b, acc_b)

    @pl.when(i % 2 == 1)
    def _():
        step(x1_b, xm_b, acc_b, xm_a, acc_a)


def _tail_call(y_na, y, bonus, gate, x2d, mod3, consts, *, tm, seq, alpha):
    rows, d = x2d.shape
    tiles_per_batch = seq // tm
    n_tiles = rows // tm

    def tile_a(i):
        return jnp.minimum(i, n_tiles - 1)

    def tile_c(i):
        return jnp.maximum(i - 2, 0)

    half = pl.BlockSpec((tm, RW_WIDTH), lambda i: (tile_a(i), 0))

    def resident(a):
        return pl.BlockSpec(a.shape, lambda i: (0,) * a.ndim, pipeline_mode=pl.Buffered(1))

    return pl.pallas_call(
        functools.partial(_tail_kernel, d, alpha, FF_CHUNK),
        grid=(n_tiles + 2,),
        in_specs=[half] * 4 + [pl.BlockSpec((tm, d), lambda i: (tile_a(i), 0)),
                               pl.BlockSpec((None, 1, mod3.shape[2]), lambda i: (tile_a(i) // tiles_per_batch, 0, 0)),
                               pl.BlockSpec((None, 1, mod3.shape[2]), lambda i: (tile_c(i) // tiles_per_batch, 0, 0))]
                 + [resident(a) for a in consts],
        out_specs=pl.BlockSpec((tm, d), lambda i: (tile_c(i), 0)),
        out_shape=jax.ShapeDtypeStruct((rows, d), F32),
        scratch_shapes=[pltpu.VMEM((tm, d), F32), pltpu.VMEM((tm, d), BF16), pltpu.VMEM((tm, d), F32)] * 2,
        compiler_params=_cparams("arbitrary"),
        name="tail",
    )(y_na, y, bonus, gate, x2d, mod3, mod3, *consts)


def _rope_tables(seq):
    f = HEAD_DIM // 4
    t = np.arange(seq)
    row = (t // GRID_W).astype(np.float32)
    col = (t % GRID_W).astype(np.float32)
    inv = (ROPE_BASE ** (-np.arange(f, dtype=np.float32) / f)).astype(np.float32)
    ang_r = row[:, None] * inv[None, :]
    ang_c = col[:, None] * inv[None, :]
    zero = np.zeros_like(ang_r)
    cos = np.concatenate([np.cos(ang_r), np.cos(ang_r), np.cos(ang_c), np.cos(ang_c)], axis=1)
    sa = np.concatenate([-np.sin(ang_r), zero, -np.sin(ang_c), zero], axis=1)
    sb = np.concatenate([zero, np.sin(ang_r), zero, np.sin(ang_c)], axis=1)
    return tuple(jnp.asarray(np.concatenate([z, z], axis=1), F32) for z in (cos, sa, sb))


def _bias_rows(rpb):
    pad = GRID_W - WIN_W
    padded = jnp.pad(rpb.astype(F32) * LOG2_E, ((0, 0), (0, 0), (pad, pad + 1)))
    return padded.transpose(1, 0, 2)


def _pad_lora(w, d):
    z = jnp.zeros_like(w[d])
    return jnp.concatenate([w[0] if d == 0 else z, w[1] if d == 1 else z], axis=0)


def kernel(x, c, ctx, c_ctx, w_mod, b_mod, w_in, w_out, ln1_g, ln1_b, mlp_w1, mlp_w2, ln2_g, ln2_b, na_rpb,
           rw_mu_prev, rw_mu_next, rw_w0, rw_w2, rw_a0, rw_a2, rw_g2, rw_k_k, rw_k_a, rw_r_k, rw_gn_g, rw_gn_b):
    depth = w_mod.shape[0]
    assert depth == 1, "single-layer trunk only (the context stream is never updated)"
    batch, seq, d = x.shape
    ctx_len = ctx.shape[1]
    alpha = (2 * depth) ** 0.25
    tm = ROW_TILE
    tm_ctx = math.gcd(ctx_len, ROW_TILE)
    assert seq % (Q_ROWS * GRID_W) == 0 and seq % tm == 0 and tm_ctx % 8 == 0
    assert batch + 1 <= SUBLANES

    cc = jnp.zeros((SUBLANES, d), F32).at[:batch].set(c).at[batch].set(c_ctx)
    mod3 = _mod_call(cc, w_mod[0], b_mod[0]).reshape(SUBLANES, 1, N_MOD * d)

    w_in_b = w_in[0].astype(BF16)
    tiles_lat = seq // tm
    x2d = x.reshape(batch * seq, d)
    q, qr, kr, v, rw_lat = _inproj_call(
        x2d, mod3, _rope_tables(seq), w_in_b, tm=tm,
        mod_index=lambda i: i // tiles_lat, table_index=lambda i: i % tiles_lat, name="inproj")
    ident = tuple(jnp.full((tm_ctx, 2 * HEAD_DIM), val, F32) for val in (1.0, 0.0, 0.0))
    _, _, kc, vc, rw_ctx = _inproj_call(
        ctx.reshape(batch * ctx_len, d), mod3, ident, w_in_b, tm=tm_ctx,
        mod_index=lambda i: batch, table_index=lambda i: 0, name="inproj_ctx")

    y_na = _na_call(q, qr, kr, v, kc, vc, _bias_rows(na_rpb[0]), batch=batch, seq=seq, ctx_len=ctx_len)

    lane_head = jnp.arange(RW_WIDTH) // HEAD_DIM
    ones = (lane_head[:, None] == lane_head[None, :]).astype(BF16)
    row = lambda a: a.reshape(1, -1).astype(F32)
    def dir_weights(dirn):
        return (row(rw_mu_prev[0]), row(rw_mu_next[0]), row(rw_w0[0, dirn]),
                _pad_lora(rw_w2[0], dirn).astype(BF16), row(rw_a0[0, dirn]),
                _pad_lora(rw_a2[0], dirn).astype(BF16), rw_g2[0].astype(BF16),
                row(rw_k_k[0]), row(rw_k_a[0]), row(rw_r_k[0]), ones)

    y_f, bonus_f, gate = _rwkv_call(rw_lat, rw_ctx, dir_weights(0), reverse=False, with_gate=True,
                                    batch=batch, seq=seq, ctx_len=ctx_len)
    y_sum, bonus_sum = _rwkv_call(rw_lat, rw_ctx, dir_weights(1), reverse=True, with_gate=False,
                                  batch=batch, seq=seq, ctx_len=ctx_len, acc=(y_f, bonus_f))

    consts = (row(rw_gn_g[0]), row(rw_gn_b[0]), ones, w_out[0].astype(BF16), row(ln1_g[0]), row(ln1_b[0]),
              mlp_w1[0].astype(BF16), mlp_w2[0].astype(BF16), row(ln2_g[0]), row(ln2_b[0]))
    out = _tail_call(y_na, y_sum, bonus_sum, gate, x2d, mod3, consts, tm=tm, seq=seq, alpha=alpha)
    return out.reshape(batch, seq, d)
```

```python
import functools
import math

import jax
import jax.numpy as jnp
import numpy as np
from jax import lax
from jax.experimental import pallas as pl
from jax.experimental.pallas import tpu as pltpu

HEAD_DIM = 64
NA_HEADS = 8
RW_HEADS = 8
NA_WIDTH = NA_HEADS * HEAD_DIM
RW_WIDTH = RW_HEADS * HEAD_DIM
GRID_W = 64
WIN_H = 8
WIN_W = 16
ROPE_BASE = 10000.0
N_DIR = 2
DECAY_LORA = 64
AAA_LORA = 64
GATE_LORA = 128
LORA_W = N_DIR * DECAY_LORA
NA_COLS = 3 * NA_WIDTH
RW_COLS = 3 * RW_WIDTH + 3 * LORA_W
N_MOD = 6
LN_EPS = 1e-6
GN_EPS = 64e-5
NEG_BIAS = -1e30

CHUNK = 64
RW_BLOCK_CHUNKS = 4
QUAD = 2 * HEAD_DIM
HALO = 8
LOG2_E = math.log2(math.e)
Q_ROWS = 8
NA_UNROLL = 4
NA_LEAD = 4
KV_GROUP = 4
WIN_GROUPS = (Q_ROWS + WIN_H) // KV_GROUP
ROW_TILE = 512
FF_CHUNK = 1024
MOD_COL_TILE = 1536
SUBLANES = 8
VMEM_LIMIT = 56 * 1024 * 1024

F32 = jnp.float32
BF16 = jnp.bfloat16


def _cparams(*sem):
    return pltpu.CompilerParams(dimension_semantics=sem, vmem_limit_bytes=VMEM_LIMIT)


def _dot(a, b):
    return jnp.dot(a, b, preferred_element_type=F32)


def _dot_nt(a, b):
    return lax.dot_general(a, b, (((1,), (1,)), ((), ())), preferred_element_type=F32)


def _dot_tn(a, b):
    return lax.dot_general(a, b, (((0,), (0,)), ((), ())), preferred_element_type=F32)


def _split(x, pieces):
    out = []
    for _ in range(pieces):
        p = x.astype(BF16)
        out.append(p)
        x = x - p.astype(F32)
    return out


def _head_sums(x, ones_bf16):
    return _dot(x.astype(BF16), ones_bf16)


def _normalize(x, eps):
    mu = jnp.mean(x, axis=-1, keepdims=True)
    xc = x - mu
    var = jnp.mean(xc * xc, axis=-1, keepdims=True)
    return xc * lax.rsqrt(var + eps)


def _mod_kernel(c_ref, w_ref, b_ref, o_ref):
    c = c_ref[...]
    s = c * jax.nn.sigmoid(c)
    o_ref[...] = _dot(s.astype(BF16), w_ref[...].astype(BF16)) + b_ref[...]


def _mod_call(cc, w_mod, b_mod):
    rows, d = cc.shape
    n = w_mod.shape[1]
    tn = MOD_COL_TILE
    assert n % tn == 0
    return pl.pallas_call(
        _mod_kernel,
        grid=(n // tn,),
        in_specs=[pl.BlockSpec((rows, d), lambda j: (0, 0)),
                  pl.BlockSpec((d, tn), lambda j: (0, j)),
                  pl.BlockSpec((1, tn), lambda j: (0, j))],
        out_specs=pl.BlockSpec((rows, tn), lambda j: (0, j)),
        out_shape=jax.ShapeDtypeStruct((rows, n), F32),
        compiler_params=_cparams("arbitrary"),
        name="mod",
    )(cc, w_mod, b_mod.reshape(1, n))


def _inproj_kernel(d_model, x_ref, mod_ref, cos_ref, sa_ref, sb_ref, w_ref, q_ref, qr_ref, kr_ref, v_ref, rw_ref):
    x = x_ref[...]
    mod = mod_ref[...]
    shift = mod[:, 0:d_model]
    scale = mod[:, d_model:2 * d_model]
    xm = (_normalize(x, LN_EPS) * (1.0 + scale) + shift).astype(BF16)

    reps = NA_WIDTH // cos_ref.shape[1]
    cos = jnp.concatenate([cos_ref[...]] * reps, axis=1)
    sa = jnp.concatenate([sa_ref[...]] * reps, axis=1)
    sb = jnp.concatenate([sb_ref[...]] * reps, axis=1)

    def rope(z):
        up = pltpu.roll(z, NA_WIDTH - HEAD_DIM // 4, 1)
        down = pltpu.roll(z, HEAD_DIM // 4, 1)
        return z * cos + up * sa + down * sb

    qk_scale = HEAD_DIM ** -0.5 * LOG2_E
    q = _dot(xm, w_ref[:, 0:NA_WIDTH]) * qk_scale
    q_ref[...] = q.astype(BF16)
    qr_ref[...] = rope(q).astype(BF16)
    k = _dot(xm, w_ref[:, NA_WIDTH:2 * NA_WIDTH])
    kr_ref[...] = rope(k).astype(BF16)
    v_ref[...] = _dot(xm, w_ref[:, 2 * NA_WIDTH:NA_COLS]).astype(BF16)
    rw_ref[...] = _dot(xm, w_ref[:, NA_COLS:NA_COLS + RW_COLS])


def _inproj_call(x2d, mod3, tables, w_in_bf16, *, tm, mod_index, table_index, name):
    rows, d = x2d.shape
    cos, sa, sb = tables
    tw = cos.shape[1]
    ncols = w_in_bf16.shape[1]
    na_spec = pl.BlockSpec((tm, NA_WIDTH), lambda i: (i, 0))
    table_spec = pl.BlockSpec((tm, tw), lambda i: (table_index(i), 0))
    na_shape = jax.ShapeDtypeStruct((rows, NA_WIDTH), BF16)
    return pl.pallas_call(
        functools.partial(_inproj_kernel, d),
        grid=(rows // tm,),
        in_specs=[pl.BlockSpec((tm, d), lambda i: (i, 0)),
                  pl.BlockSpec((None, 1, mod3.shape[2]), lambda i: (mod_index(i), 0, 0)),
                  table_spec, table_spec, table_spec,
                  pl.BlockSpec((d, ncols), lambda i: (0, 0))],
        out_specs=[na_spec, na_spec, na_spec, na_spec, pl.BlockSpec((tm, RW_COLS), lambda i: (i, 0))],
        out_shape=[na_shape, na_shape, na_shape, na_shape, jax.ShapeDtypeStruct((rows, RW_COLS), F32)],
        compiler_params=_cparams("parallel"),
        name=name,
    )(x2d, mod3, cos, sa, sb, w_in_bf16)


def _window_group(i, n_groups):
    return jnp.clip((Q_ROWS * i - WIN_H // 2) // KV_GROUP, 0, n_groups - WIN_GROUPS)


def _build_bias_table(rpb_ref, bias_ref):
    n_rows = 2 * WIN_H - 1
    lane = lax.broadcasted_iota(jnp.int32, (GRID_W, 2 * GRID_W), 1)
    qcol = lax.broadcasted_iota(jnp.int32, (GRID_W, 2 * GRID_W), 0)
    kcol = lane % GRID_W
    c0 = jnp.clip(qcol - WIN_W // 2, 0, GRID_W - WIN_W)
    inside = jnp.logical_and(kcol >= c0, kcol < c0 + WIN_W)
    first_half = lane < GRID_W
    for h in range(NA_HEADS):
        def toeplitz(ro, shift):
            row = jnp.broadcast_to(rpb_ref[ro, h:h + 1, :], (GRID_W, 2 * GRID_W))
            return pltpu.roll(row, shift, 1, stride=1, stride_axis=0)
        for ro in range(n_rows - 1):
            pair = jnp.where(first_half, toeplitz(ro, GRID_W + 1), toeplitz(ro + 1, 1))
            bias_ref[ro, h] = jnp.where(inside, pair, NEG_BIAS)


def _na_kernel(grid_rows, q_ref, qr_ref, *refs):
    k_refs = refs[0:WIN_GROUPS]
    v_refs = refs[WIN_GROUPS:2 * WIN_GROUPS]
    kc_ref, vc_ref, rpb_ref, o_ref, kwin, vwin, bias_ref = refs[2 * WIN_GROUPS:]
    i = pl.program_id(1)

    @pl.when(i == 0)
    def _():
        _build_bias_table(rpb_ref, bias_ref)

    grp = KV_GROUP * GRID_W
    for g, (kr, vr) in enumerate(zip(k_refs, v_refs)):
        kwin[g * grp:(g + 1) * grp, :] = kr[...]
        vwin[g * grp:(g + 1) * grp, :] = vr[...]
    n_groups = grid_rows // KV_GROUP
    win_row0 = KV_GROUP * _window_group(i, n_groups)
    win_keys = WIN_H * GRID_W
    lane = lax.broadcasted_iota(jnp.int32, (GRID_W, 2 * HEAD_DIM), 1)
    even = lane < HEAD_DIM

    pairs = range(NA_HEADS // 2)
    lanes = [slice(pair * 2 * HEAD_DIM, (pair + 1) * 2 * HEAD_DIM) for pair in pairs]

    def rows_body(jj, carry):
        units = []
        for u in range(NA_UNROLL):
            j = NA_UNROLL * jj + u
            irow = Q_ROWS * i + j
            r0 = jnp.clip(irow - WIN_H // 2, 0, grid_rows - WIN_H)
            koff = pl.multiple_of((r0 - win_row0) * GRID_W, GRID_W)
            brow = (WIN_H - 1) - (irow - r0)
            qoff = pl.multiple_of(j * GRID_W, GRID_W)
            units += [(koff, brow, qoff, pair) for pair in pairs]

        def split_heads(z):
            zero = jnp.zeros_like(z)
            return jnp.concatenate([jnp.where(even, z, zero), jnp.where(even, zero, z)], axis=0)

        def scores(unit):
            koff, brow, qoff, pair = unit
            bias = jnp.concatenate(
                [jnp.concatenate([bias_ref[brow + 2 * m, 2 * pair], bias_ref[brow + 2 * m, 2 * pair + 1]], axis=0)
                 for m in range(WIN_H // 2)], axis=1)
            s_l = _dot_nt(split_heads(qr_ref[pl.ds(qoff, GRID_W), lanes[pair]]),
                          kwin[pl.ds(koff, win_keys), lanes[pair]]) + bias
            s_c = _dot_nt(split_heads(q_ref[pl.ds(qoff, GRID_W), lanes[pair]]), kc_ref[:, lanes[pair]])
            return s_l, s_c

        def attend(unit, s_l, s_c):
            koff, brow, qoff, pair = unit
            mx = jnp.maximum(jnp.max(s_l, axis=-1, keepdims=True), jnp.max(s_c, axis=-1, keepdims=True))
            p_l = jnp.exp2(s_l - mx)
            p_c = jnp.exp2(s_c - mx)
            denom = jnp.sum(p_l, axis=-1, keepdims=True) + jnp.sum(p_c, axis=-1, keepdims=True)
            o = (_dot(p_l.astype(BF16), vwin[pl.ds(koff, win_keys), lanes[pair]])
                 + _dot(p_c.astype(BF16), vc_ref[:, lanes[pair]])) / denom
            o_ref[pl.ds(qoff, GRID_W), lanes[pair]] = jnp.where(
                even, o[0:GRID_W], o[GRID_W:2 * GRID_W]).astype(BF16)

        pending = [scores(unit) for unit in units[:NA_LEAD]]
        for n, unit in enumerate(units):
            if n + NA_LEAD < len(units):
                pending.append(scores(units[n + NA_LEAD]))
            attend(unit, *pending[n])
        return carry

    lax.fori_loop(0, Q_ROWS // NA_UNROLL, rows_body, 0)


def _na_call(q, qr, kr, v, kc, vc, rpb_rows, *, batch, seq, ctx_len):
    grid_rows = seq // GRID_W
    n_blocks = grid_rows // Q_ROWS
    n_groups = grid_rows // KV_GROUP
    blk = Q_ROWS * GRID_W
    grp = KV_GROUP * GRID_W

    def group_spec(g):
        return pl.BlockSpec((grp, NA_WIDTH),
                            lambda b, i: (b * n_groups + _window_group(i, n_groups) + g, 0))

    q_spec = pl.BlockSpec((blk, NA_WIDTH), lambda b, i: (b * n_blocks + i, 0))
    c_spec = pl.BlockSpec((ctx_len, NA_WIDTH), lambda b, i: (b, 0))
    return pl.pallas_call(
        functools.partial(_na_kernel, grid_rows),
        grid=(batch, n_blocks),
        in_specs=[q_spec, q_spec] + [group_spec(g) for g in range(WIN_GROUPS)] * 2
                 + [c_spec, c_spec, pl.BlockSpec(rpb_rows.shape, lambda b, i: (0, 0, 0))],
        out_specs=q_spec,
        out_shape=jax.ShapeDtypeStruct((batch * seq, NA_WIDTH), BF16),
        scratch_shapes=[pltpu.VMEM((WIN_GROUPS * grp, NA_WIDTH), BF16)] * 2 + [
                        pltpu.VMEM((2 * WIN_H - 2, NA_HEADS, GRID_W, 2 * GRID_W), F32)],
        compiler_params=_cparams("parallel", "arbitrary"),
        name="natten",
    )(q, qr, *([kr] * WIN_GROUPS), *([v] * WIN_GROUPS), kc, vc, rpb_rows)


def _block_diag(x_cat, bd_mask):
    xb = x_cat.astype(BF16)
    tiled = jnp.concatenate([xb] * (QUAD // CHUNK), axis=0)
    return jnp.where(bd_mask, tiled, jnp.zeros_like(tiled))


def _diag_blocks(x, head_of_lane):
    n = QUAD // HEAD_DIM
    out = x[(n - 1) * HEAD_DIM:n * HEAD_DIM]
    for h in range(n - 2, -1, -1):
        out = jnp.where(head_of_lane == h, x[h * HEAD_DIM:(h + 1) * HEAD_DIM], out)
    return out


def _rwkv_kernel(reverse, with_gate, n_ctx, n_blocks,
                 lat_ref, lat_prev_ref, lat_next_ref, ctx_ref, ctx_prev_ref, ctx_next_ref,
                 mup_ref, mun_ref, w0_ref, w2_ref, a0_ref, a2_ref, g2_ref,
                 kk_ref, ka_ref, rk_ref, ones_ref, *refs):
    if with_gate:
        acc_refs = (None, None)
        y_ref, bonus_ref, gate_ref = refs[0:3]
        refs = refs[3:]
    else:
        acc_refs = refs[0:2]
        y_ref, bonus_ref = refs[2:4]
        gate_ref = None
        refs = refs[4:]
    h_ref = refs[0]
    sets = (refs[1:7], refs[7:13])
    psets = (refs[13:17], refs[17:21])
    s = pl.program_id(1)

    last = n_blocks + 1

    def run(parity, stages):
        _rwkv_step(reverse, n_ctx, n_blocks, s, stages, sets[parity], sets[1 - parity], psets[parity],
                   psets[1 - parity], lat_ref, lat_prev_ref, lat_next_ref,
                   ctx_ref, ctx_prev_ref, ctx_next_ref, mup_ref, mun_ref, w0_ref, w2_ref, a0_ref, a2_ref,
                   g2_ref, kk_ref, ka_ref, rk_ref, ones_ref, acc_refs, y_ref, bonus_ref, gate_ref, h_ref)

    @pl.when(s == 0)
    def _():
        h_ref[...] = jnp.zeros_like(h_ref)
        for ref in psets[0]:
            ref[...] = jnp.zeros_like(ref)
        run(0, ("prep",))

    @pl.when(s == last)
    def _():
        run(last % 2, ("state",))

    inner = jnp.logical_and(s > 0, s < last)

    @pl.when(jnp.logical_and(inner, s % 2 == 0))
    def _():
        run(0, ("prep", "chains", "state"))

    @pl.when(jnp.logical_and(inner, s % 2 == 1))
    def _():
        run(1, ("prep", "chains", "state"))


def _rwkv_step(reverse, n_ctx, n_blocks, s, stages, wset, rset, pwset, prset, lat_ref, lat_prev_ref, lat_next_ref,
               ctx_ref, ctx_prev_ref, ctx_next_ref, mup_ref, mun_ref, w0_ref, w2_ref, a0_ref, a2_ref,
               g2_ref, kk_ref, ka_ref, rk_ref, ones_ref, acc_refs, y_ref, bonus_ref, gate_ref, h_ref):
    sp = jnp.minimum(s, n_blocks - 1)
    if reverse:
        n = jnp.where(sp < n_ctx, n_ctx - 1 - sp, n_blocks + n_ctx - 1 - sp)
    else:
        n = sp

    rows = lat_ref.shape[0]
    n_ch = rows // CHUNK
    w = RW_WIDTH
    t = {}

    is_ctx = n < n_ctx
    has_prev = jnp.logical_and(n != 0, n != n_ctx)
    has_next = jnp.logical_and(n != n_ctx - 1, n != n_blocks - 1)
    row = lax.broadcasted_iota(jnp.int32, (rows, 1), 0)

    def shifted(c0, c1):
        p = jnp.where(is_ctx, ctx_ref[:, c0:c1], lat_ref[:, c0:c1])
        prow = jnp.where(is_ctx, ctx_prev_ref[HALO - 1:HALO, c0:c1], lat_prev_ref[HALO - 1:HALO, c0:c1])
        nrow = jnp.where(is_ctx, ctx_next_ref[0:1, c0:c1], lat_next_ref[0:1, c0:c1])
        prow = jnp.where(has_prev, prow, 0.0)
        nrow = jnp.where(has_next, nrow, 0.0)
        prev = jnp.where(row == 0, prow, pltpu.roll(p, 1, 0))
        nxt = jnp.where(row == rows - 1, nrow, pltpu.roll(p, rows - 1, 0))
        return p + mup_ref[:, c0:c1] * (prev - p) + mun_ref[:, c0:c1] * (nxt - p)

    def prep_lora():
        lora = shifted(3 * w, 3 * w + 3 * LORA_W)
        pw = lora[:, 0:LORA_W]
        pa = lora[:, LORA_W:2 * LORA_W]
        t["log_decay"] = -math.exp(-0.5) * jax.nn.sigmoid(
            w0_ref[...] + _dot(jnp.tanh(pw).astype(BF16), w2_ref[...]))
        t["a"] = jax.nn.sigmoid(a0_ref[...] + _dot(pa.astype(BF16), a2_ref[...]))
        if gate_ref is not None:
            gate_ref[...] = _dot(jax.nn.sigmoid(lora[:, 2 * LORA_W:3 * LORA_W]).astype(BF16), g2_ref[...])

    def prep_keys():
        k = shifted(w, 2 * w)
        ones = ones_ref[...]
        kk_raw = k * kk_ref[...]
        kk = kk_raw * lax.rsqrt(jnp.maximum(_head_sums(kk_raw * kk_raw, ones), 1e-24))
        t.update(kk=kk, kd=k * (1.0 + (t["a"] - 1.0) * ka_ref[...]), bb=t["a"] * kk)

    def prep_bonus():
        r = shifted(0, w)
        v = shifted(2 * w, 3 * w)
        bonus = _head_sums(r * t["kd"] * rk_ref[...], ones_ref[...]) * v
        bonus_ref[...] = bonus if acc_refs[1] is None else bonus + acc_refs[1][...]
        t.update(r=r, v=v)

    def finish_prep():
        r, v, kk, kd, bb, log_decay = (t[name] for name in ("r", "v", "kk", "kd", "bb", "log_decay"))
        ti = lax.broadcasted_iota(jnp.int32, (rows, rows), 0)
        si = lax.broadcasted_iota(jnp.int32, (rows, rows), 1)
        same_chunk = (ti // CHUNK) == (si // CHUNK)
        tri = jnp.where(jnp.logical_and(same_chunk, (si >= ti) if reverse else (si <= ti)), 1.0, 0.0).astype(BF16)
        lw_hi, lw_lo = _split(log_decay, 2)
        cum = _dot(tri, lw_hi) + _dot(tri, lw_lo)
        e_neg = jnp.exp(-cum)
        last = 0 if reverse else CHUNK - 1
        w_v, w_kkt, w_rt, w_kh, w_bh, w_et = wset
        w_v[...] = v
        w_kkt[...] = kk * jnp.exp(cum - log_decay)
        w_rt[...] = r * jnp.exp(cum)
        w_kh[...] = kd * e_neg
        w_bh[...] = bb * e_neg
        for c in range(n_ch):
            w_et[c:c + 1, :] = jnp.exp(cum[c * CHUNK + last:c * CHUNK + last + 1, :])

    prep_at = {"start": prep_lora, "gram": prep_keys, "akv": prep_bonus, "dbl1": finish_prep}

    def issue_prep(point):
        if point in prep_at and "prep" in stages:
            prep_at[point]()

    issue_prep("start")
    r_v, r_kkt, r_rt, r_kh, r_bh, r_et = rset
    trow = lax.broadcasted_iota(jnp.int32, (CHUNK, QUAD), 0)
    tcol = lax.broadcasted_iota(jnp.int32, (CHUNK, QUAD), 1) % CHUNK
    strict = (trow < tcol) if reverse else (trow > tcol)
    incl = (trow <= tcol) if reverse else (trow >= tcol)
    on_diag = trow == tcol
    eye_cat = jnp.where(on_diag, 1.0, 0.0)
    brow = lax.broadcasted_iota(jnp.int32, (QUAD, QUAD), 0)
    bcol = lax.broadcasted_iota(jnp.int32, (QUAD, QUAD), 1)
    bd_mask = (brow // HEAD_DIM) == (bcol // HEAD_DIM)
    head_of_lane = lax.broadcasted_iota(jnp.int32, (CHUNK, QUAD), 1) // HEAD_DIM
    n_doublings = int(math.log2(CHUNK)) - 1
    heads = range(QUAD // HEAD_DIM)
    quads = range(RW_WIDTH // QUAD)
    order = list(range(n_ch - 1, -1, -1) if reverse else range(n_ch))

    chains = [(c, qd) for c in order for qd in quads]

    def rsl(c):
        return slice(c * CHUNK, (c + 1) * CHUNK)

    def lsl(qd):
        return slice(qd * QUAD, (qd + 1) * QUAD)

    r_y0, r_qt, r_g, r_h0 = prset
    h = {qd: h_ref[qd] for qd in quads}
    issued = [0]

    def issue_state(upto):
        while "state" in stages and issued[0] < min(upto, len(order)):
            c = order[issued[0]]
            issued[0] += 1
            for qd in quads:
                lhs = jnp.concatenate([r_g[rsl(c), lsl(qd)], r_qt[rsl(c), lsl(qd)]], axis=0).astype(BF16)
                o2 = _dot(lhs, _block_diag(h[qd], bd_mask))
                h[qd] = o2[0:CHUNK] + r_h0[rsl(c), lsl(qd)]
                y = r_y0[rsl(c), lsl(qd)] + o2[CHUNK:2 * CHUNK]
                y_ref[rsl(c), lsl(qd)] = y if acc_refs[0] is None else y + acc_refs[0][rsl(c), lsl(qd)]
            if issued[0] == len(order):
                for qd in quads:
                    h_ref[qd] = h[qd]

    issue_state(1)
    if "chains" not in stages:
        for point in ["gram", "akv", "m1"] + ["dbl%d" % step for step in range(n_doublings)]:
            issue_prep(point)
        issue_state(len(order))
        return
    v_q = {ch: r_v[rsl(ch[0]), lsl(ch[1])] for ch in chains}
    kkt_q = {ch: r_kkt[rsl(ch[0]), lsl(ch[1])] for ch in chains}
    rt_q = {ch: r_rt[rsl(ch[0]), lsl(ch[1])] for ch in chains}

    a_k, a_b, b_k, b_b = {}, {}, {}, {}
    for ch in chains:
        kh = r_kh[rsl(ch[0]), lsl(ch[1])].astype(BF16)
        bh = r_bh[rsl(ch[0]), lsl(ch[1])].astype(BF16)
        zero = jnp.zeros_like(kh)
        rhs = jnp.concatenate([jnp.where(head_of_lane == h, kh, zero) for h in heads]
                              + [jnp.where(head_of_lane == h, bh, zero) for h in heads], axis=0)
        lhs = jnp.concatenate([kkt_q[ch], rt_q[ch]], axis=0).astype(BF16)
        gram = _dot_nt(lhs, rhs)
        a_k[ch] = jnp.where(strict, gram[0:CHUNK, 0:QUAD], 0.0)
        a_b[ch] = jnp.where(strict, gram[0:CHUNK, QUAD:2 * QUAD], 0.0)
        b_k[ch] = jnp.where(incl, gram[CHUNK:2 * CHUNK, 0:QUAD], 0.0)
        b_b[ch] = jnp.where(incl, gram[CHUNK:2 * CHUNK, QUAD:2 * QUAD], 0.0).astype(BF16)
    issue_prep("gram")
    issue_state(2)

    akv, bkv = {}, {}
    for ch in chains:
        both = _dot(jnp.concatenate([a_k[ch], b_k[ch]], axis=0).astype(BF16), _block_diag(v_q[ch], bd_mask))
        akv[ch] = both[0:CHUNK]
        bkv[ch] = both[CHUNK:2 * CHUNK]
    issue_prep("akv")
    issue_state(3)

    m = {ch: -a_b[ch] for ch in chains}
    t_inv = {ch: eye_cat + m[ch] for ch in chains}
    for ch in chains:
        m[ch] = _dot(m[ch].astype(BF16), _block_diag(m[ch], bd_mask))
    issue_prep("m1")
    issue_state(4)
    for step in range(n_doublings):
        final = step == n_doublings - 1
        issue_prep("dbl%d" % step)
        for ch in chains:
            m_bd = _block_diag(m[ch], bd_mask)
            if final:
                t_inv[ch] = t_inv[ch] + _dot(t_inv[ch].astype(BF16), m_bd)
            else:
                both = _dot(jnp.concatenate([m[ch], t_inv[ch]], axis=0).astype(BF16), m_bd)
                m[ch] = both[0:CHUNK]
                t_inv[ch] = t_inv[ch] + both[CHUNK:2 * CHUNK]

    p1, tkk = {}, {}
    for ch in chains:
        t_b = t_inv[ch].astype(BF16)
        p1[ch] = _dot(t_b, _block_diag(akv[ch], bd_mask))
        tkk[ch] = _dot(t_b, _block_diag(kkt_q[ch], bd_mask))

    w_y0, w_qt, w_g, w_h0 = pwset
    for ch in chains:
        c, qd = ch
        w_y0[rsl(c), lsl(qd)] = bkv[ch] - _dot(b_b[ch], _block_diag(p1[ch], bd_mask))
        w_qt[rsl(c), lsl(qd)] = rt_q[ch] - _dot(b_b[ch], _block_diag(tkk[ch], bd_mask))
        e_q = r_et[c:c + 1, lsl(qd)]
        lhs_t = jnp.concatenate([r_kh[rsl(c), lsl(qd)] * e_q, r_bh[rsl(c), lsl(qd)] * e_q], axis=0).astype(BF16)
        rhs_t = jnp.concatenate([jnp.concatenate([v_q[ch], jnp.zeros_like(v_q[ch])], axis=1),
                                 jnp.concatenate([-p1[ch], -tkk[ch]], axis=1)], axis=0).astype(BF16)
        hg = _dot_tn(lhs_t, rhs_t)
        w_h0[rsl(c), lsl(qd)] = _diag_blocks(hg[:, 0:QUAD], head_of_lane)
        w_g[rsl(c), lsl(qd)] = _diag_blocks(hg[:, QUAD:2 * QUAD], head_of_lane) + jnp.where(on_diag, e_q, 0.0)
    issue_state(len(order))


def _rwkv_call(rw_lat, rw_ctx, weights, *, reverse, with_gate, batch, seq, ctx_len, acc=None):
    assert CHUNK == HEAD_DIM
    blk_rows = RW_BLOCK_CHUNKS * CHUNK
    assert ctx_len % blk_rows == 0 and seq % blk_rows == 0
    n_ctx = ctx_len // blk_rows
    n_lat = seq // blk_rows
    n_chunks = n_ctx + n_lat
    halo_per_blk = blk_rows // HALO

    def block_of(s):
        if reverse:
            return jnp.where(s < n_ctx, n_ctx - 1 - s, n_chunks + n_ctx - 1 - s)
        return s

    def stream_specs(first, count):
        n_halo = batch * count * halo_per_blk

        def blk(b, s):
            return b * count + jnp.clip(block_of(jnp.minimum(s, n_chunks - 1)) - first, 0, count - 1)

        return [pl.BlockSpec((blk_rows, RW_COLS), lambda b, s: (blk(b, s), 0)),
                pl.BlockSpec((HALO, RW_COLS), lambda b, s: (jnp.maximum(blk(b, s) * halo_per_blk - 1, 0), 0)),
                pl.BlockSpec((HALO, RW_COLS),
                             lambda b, s: (jnp.minimum((blk(b, s) + 1) * halo_per_blk, n_halo - 1), 0))]

    def out_block(b, step):
        lat = jnp.maximum(step, n_ctx) - n_ctx
        return (b * n_lat + (n_lat - 1 - lat if reverse else lat), 0)

    def const_spec(a):
        return pl.BlockSpec(a.shape, lambda b, s: (0,) * a.ndim)

    prep_spec = pl.BlockSpec((blk_rows, RW_WIDTH), lambda b, s: out_block(b, jnp.minimum(s, n_chunks - 1)))
    y_spec = pl.BlockSpec((blk_rows, RW_WIDTH), lambda b, s: out_block(b, jnp.maximum(s - 2, 0)))
    out_shape = jax.ShapeDtypeStruct((batch * seq, RW_WIDTH), F32)
    n_out = 3 if with_gate else 2
    assert with_gate == (acc is None)
    acc_specs = [] if acc is None else [y_spec, prep_spec]
    assert RW_BLOCK_CHUNKS <= SUBLANES
    term_set = [pltpu.VMEM((blk_rows, RW_WIDTH), F32)] * 5 + [pltpu.VMEM((SUBLANES, RW_WIDTH), F32)]
    chain_set = [pltpu.VMEM((blk_rows, RW_WIDTH), F32)] * 4
    return pl.pallas_call(
        functools.partial(_rwkv_kernel, reverse, with_gate, n_ctx, n_chunks),
        grid=(batch, n_chunks + 2),
        in_specs=stream_specs(n_ctx, n_lat) + stream_specs(0, n_ctx) + [const_spec(a) for a in weights] + acc_specs,
        out_specs=[y_spec] + [prep_spec] * (n_out - 1),
        out_shape=[out_shape] * n_out,
        scratch_shapes=[pltpu.VMEM((RW_WIDTH // QUAD, CHUNK, QUAD), F32)] + term_set * 2 + chain_set * 2,
        compiler_params=_cparams("parallel", "arbitrary"),
        name="rwkv_bwd" if reverse else "rwkv_fwd",
    )(rw_lat, rw_lat, rw_lat, rw_ctx, rw_ctx, rw_ctx, *weights, *(acc or ()))


def _tail_kernel(d_model, alpha, ff_chunk, yna_ref, y_ref, bonus_ref, gate_ref, x_ref, moda_ref, modc_ref,
                 gng_ref, gnb_ref, wo_ref, ln1g_ref, ln1b_ref, w1_ref, w2_ref, ln2g_ref, ln2b_ref,
                 o_ref, x1_a, xm_a, acc_a, x1_b, xm_b, acc_b):
    i = pl.program_id(0)
    last = pl.num_programs(0) - 1

    def step(stages, x1_cur, xm_cur, acc_cur, xm_prev, acc_prev):
        inv_n = 1.0 / HEAD_DIM
        n_chunks = w1_ref.shape[1] // ff_chunk
        t = {}

        def lane_head_sums(z):
            lane = lax.broadcasted_iota(jnp.int32, (z.shape[0], 2 * HEAD_DIM), 1)
            first = lane < HEAD_DIM
            slabs = []
            for p in range(z.shape[1] // (2 * HEAD_DIM)):
                zs = z[:, p * 2 * HEAD_DIM:(p + 1) * 2 * HEAD_DIM]
                both = jnp.sum(zs, axis=-1, keepdims=True)
                lo = jnp.sum(jnp.where(first, zs, 0.0), axis=-1, keepdims=True)
                slabs.append(jnp.where(first, lo, both - lo))
            return jnp.concatenate(slabs, axis=1)

        def readout_stats():
            y = y_ref[...]
            mu = lane_head_sums(y) * inv_n
            yc = y - mu
            t.update(yc=yc, var=lane_head_sums(yc * yc) * inv_n)

        def out_projection():
            yn = t["yc"] * lax.rsqrt(t["var"] + GN_EPS) * gng_ref[...] + gnb_ref[...]
            y_rw = ((yn + bonus_ref[...]) * gate_ref[...]).astype(BF16)
            t["proj"] = (_dot(yna_ref[...], wo_ref[0:NA_WIDTH, :])
                         + _dot(y_rw, wo_ref[NA_WIDTH:NA_WIDTH + RW_WIDTH, :]))

        half_rows = x_ref.shape[0] // 2

        def norm_modulate(part):
            rows = slice(part * half_rows, (part + 1) * half_rows)
            mod = moda_ref[...]
            g1 = mod[:, 2 * d_model:3 * d_model]
            shift = mod[:, 3 * d_model:4 * d_model]
            scale = mod[:, 4 * d_model:5 * d_model]
            x1 = (_normalize(alpha * x_ref[rows, :] + g1 * t["proj"][rows, :], LN_EPS) * ln1g_ref[...]
                  + ln1b_ref[...])
            x1_cur[rows, :] = x1
            xm_cur[rows, :] = (_normalize(x1, LN_EPS) * (1.0 + scale) + shift).astype(BF16)

        def final_norm(part):
            rows = slice(part * half_rows, (part + 1) * half_rows)
            g2 = modc_ref[...][:, 5 * d_model:6 * d_model]
            o_ref[rows, :] = (_normalize(alpha * x1_cur[rows, :] + g2 * acc_cur[rows, :], LN_EPS) * ln2g_ref[...]
                              + ln2b_ref[...])

        others = [(min(k, n_chunks - 1), name, stage) for k, name, stage in
                  ((0, "a", readout_stats), (0, "a", out_projection),
                   (1, "c", functools.partial(final_norm, 0)), (1, "a", functools.partial(norm_modulate, 0)),
                   (2, "c", functools.partial(final_norm, 1)), (2, "a", functools.partial(norm_modulate, 1)))]
        if "b" not in stages:
            for _, name, stage in others:
                if name in stages:
                    stage()
            return
        xm = xm_prev[...]
        acc = None
        for c in range(n_chunks):
            c0 = c * ff_chunk
            hid = jnp.maximum(_dot(xm, w1_ref[:, c0:c0 + ff_chunk]), 0.0)
            part = _dot((hid * hid).astype(BF16), w2_ref[c0:c0 + ff_chunk, :])
            acc = part if acc is None else acc + part
            for after, name, stage in others:
                if after == c and name in stages:
                    stage()
        acc_prev[...] = acc

    slots = ((x1_a, xm_a, acc_a, xm_b, acc_b), (x1_b, xm_b, acc_b, xm_a, acc_a))

    @pl.when(i == 0)
    def _():
        x1_b[...] = jnp.zeros_like(x1_b)
        acc_b[...] = jnp.zeros_like(acc_b)
        step("a", *slots[0])

    inner = jnp.logical_and(i > 0, i < last)

    @pl.when(jnp.logical_and(inner, i % 2 == 0))
    def _():
        step("abc", *slots[0])

    @pl.when(jnp.logical_and(inner, i % 2 == 1))
    def _():
        step("abc", *slots[1])

    @pl.when(jnp.logical_and(i == last, i % 2 == 0))
    def _():
        step("c", *slots[0])

    @pl.when(jnp.logical_and(i == last, i % 2 == 1))
    def _():
        step("c", *slots[1])


def _tail_call(y_na, y, bonus, gate, x2d, mod3, consts, *, tm, seq, alpha):
    rows, d = x2d.shape
    tiles_per_batch = seq // tm
    n_tiles = rows // tm

    def tile_a(i):
        return jnp.minimum(i, n_tiles - 1)

    def tile_c(i):
        return jnp.maximum(i - 2, 0)

    half = pl.BlockSpec((tm, RW_WIDTH), lambda i: (tile_a(i), 0))

    def resident(a):
        return pl.BlockSpec(a.shape, lambda i: (0,) * a.ndim, pipeline_mode=pl.Buffered(1))

    return pl.pallas_call(
        functools.partial(_tail_kernel, d, alpha, FF_CHUNK),
        grid=(n_tiles + 2,),
        in_specs=[half] * 4 + [pl.BlockSpec((tm, d), lambda i: (tile_a(i), 0)),
                               pl.BlockSpec((None, 1, mod3.shape[2]), lambda i: (tile_a(i) // tiles_per_batch, 0, 0)),
                               pl.BlockSpec((None, 1, mod3.shape[2]), lambda i: (tile_c(i) // tiles_per_batch, 0, 0))]
                 + [resident(a) for a in consts],
        out_specs=pl.BlockSpec((tm, d), lambda i: (tile_c(i), 0)),
        out_shape=jax.ShapeDtypeStruct((rows, d), F32),
        scratch_shapes=[pltpu.VMEM((tm, d), F32), pltpu.VMEM((tm, d), BF16), pltpu.VMEM((tm, d), F32)] * 2,
        compiler_params=_cparams("arbitrary"),
        name="tail",
    )(y_na, y, bonus, gate, x2d, mod3, mod3, *consts)


def _rope_tables(seq):
    f = HEAD_DIM // 4
    t = np.arange(seq)
    row = (t // GRID_W).astype(np.float32)
    col = (t % GRID_W).astype(np.float32)
    inv = (ROPE_BASE ** (-np.arange(f, dtype=np.float32) / f)).astype(np.float32)
    ang_r = row[:, None] * inv[None, :]
    ang_c = col[:, None] * inv[None, :]
    zero = np.zeros_like(ang_r)
    cos = np.concatenate([np.cos(ang_r), np.cos(ang_r), np.cos(ang_c), np.cos(ang_c)], axis=1)
    sa = np.concatenate([-np.sin(ang_r), zero, -np.sin(ang_c), zero], axis=1)
    sb = np.concatenate([zero, np.sin(ang_r), zero, np.sin(ang_c)], axis=1)
    return tuple(jnp.asarray(np.concatenate([z, z], axis=1), F32) for z in (cos, sa, sb))


def _bias_rows(rpb):
    pad = GRID_W - WIN_W
    padded = jnp.pad(rpb.astype(F32) * LOG2_E, ((0, 0), (0, 0), (pad, pad + 1)))
    return padded.transpose(1, 0, 2)


def _pad_lora(w, d):
    z = jnp.zeros_like(w[d])
    return jnp.concatenate([w[0] if d == 0 else z, w[1] if d == 1 else z], axis=0)


def kernel(x, c, ctx, c_ctx, w_mod, b_mod, w_in, w_out, ln1_g, ln1_b, mlp_w1, mlp_w2, ln2_g, ln2_b, na_rpb,
           rw_mu_prev, rw_mu_next, rw_w0, rw_w2, rw_a0, rw_a2, rw_g2, rw_k_k, rw_k_a, rw_r_k, rw_gn_g, rw_gn_b):
    depth = w_mod.shape[0]
    assert depth == 1, "single-layer trunk only (the context stream is never updated)"
    batch, seq, d = x.shape
    ctx_len = ctx.shape[1]
    alpha = (2 * depth) ** 0.25
    tm = ROW_TILE
    tm_ctx = math.gcd(ctx_len, ROW_TILE)
    assert seq % (Q_ROWS * GRID_W) == 0 and seq % tm == 0 and tm_ctx % 8 == 0
    assert batch + 1 <= SUBLANES

    cc = jnp.zeros((SUBLANES, d), F32).at[:batch].set(c).at[batch].set(c_ctx)
    mod3 = _mod_call(cc, w_mod[0], b_mod[0]).reshape(SUBLANES, 1, N_MOD * d)

    w_in_b = w_in[0].astype(BF16)
    tiles_lat = seq // tm
    x2d = x.reshape(batch * seq, d)
    q, qr, kr, v, rw_lat = _inproj_call(
        x2d, mod3, _rope_tables(seq), w_in_b, tm=tm,
        mod_index=lambda i: i // tiles_lat, table_index=lambda i: i % tiles_lat, name="inproj")
    ident = tuple(jnp.full((tm_ctx, 2 * HEAD_DIM), val, F32) for val in (1.0, 0.0, 0.0))
    _, _, kc, vc, rw_ctx = _inproj_call(
        ctx.reshape(batch * ctx_len, d), mod3, ident, w_in_b, tm=tm_ctx,
        mod_index=lambda i: batch, table_index=lambda i: 0, name="inproj_ctx")

    y_na = _na_call(q, qr, kr, v, kc, vc, _bias_rows(na_rpb[0]), batch=batch, seq=seq, ctx_len=ctx_len)

    lane_head = jnp.arange(RW_WIDTH) // HEAD_DIM
    ones = (lane_head[:, None] == lane_head[None, :]).astype(BF16)
    row = lambda a: a.reshape(1, -1).astype(F32)
    def dir_weights(dirn):
        return (row(rw_mu_prev[0]), row(rw_mu_next[0]), row(rw_w0[0, dirn]),
                _pad_lora(rw_w2[0], dirn).astype(BF16), row(rw_a0[0, dirn]),
                _pad_lora(rw_a2[0], dirn).astype(BF16), rw_g2[0].astype(BF16),
                row(rw_k_k[0]), row(rw_k_a[0]), row(rw_r_k[0]), ones)

    y_f, bonus_f, gate = _rwkv_call(rw_lat, rw_ctx, dir_weights(0), reverse=False, with_gate=True,
                                    batch=batch, seq=seq, ctx_len=ctx_len)
    y_sum, bonus_sum = _rwkv_call(rw_lat, rw_ctx, dir_weights(1), reverse=True, with_gate=False,
                                  batch=batch, seq=seq, ctx_len=ctx_len, acc=(y_f, bonus_f))

    consts = (row(rw_gn_g[0]), row(rw_gn_b[0]), w_out[0].astype(BF16), row(ln1_g[0]), row(ln1_b[0]),
              mlp_w1[0].astype(BF16), mlp_w2[0].astype(BF16), row(ln2_g[0]), row(ln2_b[0]))
    out = _tail_call(y_na, y_sum, bonus_sum, gate, x2d, mod3, consts, tm=tm, seq=seq, alpha=alpha)
    return out.reshape(batch, seq, d)
```

```python
import functools
import math

import jax
import jax.numpy as jnp
import numpy as np
from jax import lax
from jax.experimental import pallas as pl
from jax.experimental.pallas import tpu as pltpu

HEAD_DIM = 64
NA_HEADS = 8
RW_HEADS = 8
NA_WIDTH = NA_HEADS * HEAD_DIM
RW_WIDTH = RW_HEADS * HEAD_DIM
GRID_W = 64
WIN_H = 8
WIN_W = 16
ROPE_BASE = 10000.0
N_DIR = 2
DECAY_LORA = 64
AAA_LORA = 64
GATE_LORA = 128
LORA_W = N_DIR * DECAY_LORA
NA_COLS = 3 * NA_WIDTH
RW_COLS = 3 * RW_WIDTH + 3 * LORA_W
N_MOD = 6
LN_EPS = 1e-6
GN_EPS = 64e-5
NEG_BIAS = -1e30

CHUNK = 64
RW_BLOCK_CHUNKS = 4
QUAD = 2 * HEAD_DIM
HALO = 8
LOG2_E = math.log2(math.e)
Q_ROWS = 8
NA_UNROLL = 4
NA_LEAD = 4
KV_GROUP = 4
WIN_GROUPS = (Q_ROWS + WIN_H) // KV_GROUP
ROW_TILE = 512
FF_CHUNK = 1024
MOD_COL_TILE = 1536
SUBLANES = 8
VMEM_LIMIT = 56 * 1024 * 1024

F32 = jnp.float32
BF16 = jnp.bfloat16


def _cparams(*sem):
    return pltpu.CompilerParams(dimension_semantics=sem, vmem_limit_bytes=VMEM_LIMIT)


def _dot(a, b):
    return jnp.dot(a, b, preferred_element_type=F32)


def _dot_nt(a, b):
    return lax.dot_general(a, b, (((1,), (1,)), ((), ())), preferred_element_type=F32)


def _dot_tn(a, b):
    return lax.dot_general(a, b, (((0,), (0,)), ((), ())), preferred_element_type=F32)


def _split(x, pieces):
    out = []
    for _ in range(pieces):
        p = x.astype(BF16)
        out.append(p)
        x = x - p.astype(F32)
    return out


def _head_sums(x, ones_bf16):
    return _dot(x.astype(BF16), ones_bf16)


def _normalize(x, eps):
    mu = jnp.mean(x, axis=-1, keepdims=True)
    xc = x - mu
    var = jnp.mean(xc * xc, axis=-1, keepdims=True)
    return xc * lax.rsqrt(var + eps)


def _mod_kernel(c_ref, w_ref, b_ref, o_ref):
    c = c_ref[...]
    s = c * jax.nn.sigmoid(c)
    o_ref[...] = _dot(s.astype(BF16), w_ref[...].astype(BF16)) + b_ref[...]


def _mod_call(cc, w_mod, b_mod):
    rows, d = cc.shape
    n = w_mod.shape[1]
    tn = MOD_COL_TILE
    assert n % tn == 0
    return pl.pallas_call(
        _mod_kernel,
        grid=(n // tn,),
        in_specs=[pl.BlockSpec((rows, d), lambda j: (0, 0)),
                  pl.BlockSpec((d, tn), lambda j: (0, j)),
                  pl.BlockSpec((1, tn), lambda j: (0, j))],
        out_specs=pl.BlockSpec((rows, tn), lambda j: (0, j)),
        out_shape=jax.ShapeDtypeStruct((rows, n), F32),
        compiler_params=_cparams("arbitrary"),
        name="mod",
    )(cc, w_mod, b_mod.reshape(1, n))


def _inproj_kernel(d_model, x_ref, mod_ref, cos_ref, sa_ref, sb_ref, w_ref, q_ref, qr_ref, kr_ref, v_ref, rw_ref):
    x = x_ref[...]
    mod = mod_ref[...]
    shift = mod[:, 0:d_model]
    scale = mod[:, d_model:2 * d_model]
    xm = (_normalize(x, LN_EPS) * (1.0 + scale) + shift).astype(BF16)

    reps = NA_WIDTH // cos_ref.shape[1]
    cos = jnp.concatenate([cos_ref[...]] * reps, axis=1)
    sa = jnp.concatenate([sa_ref[...]] * reps, axis=1)
    sb = jnp.concatenate([sb_ref[...]] * reps, axis=1)

    def rope(z):
        up = pltpu.roll(z, NA_WIDTH - HEAD_DIM // 4, 1)
        down = pltpu.roll(z, HEAD_DIM // 4, 1)
        return z * cos + up * sa + down * sb

    qk_scale = HEAD_DIM ** -0.5 * LOG2_E
    q = _dot(xm, w_ref[:, 0:NA_WIDTH]) * qk_scale
    q_ref[...] = q.astype(BF16)
    qr_ref[...] = rope(q).astype(BF16)
    k = _dot(xm, w_ref[:, NA_WIDTH:2 * NA_WIDTH])
    kr_ref[...] = rope(k).astype(BF16)
    v_ref[...] = _dot(xm, w_ref[:, 2 * NA_WIDTH:NA_COLS]).astype(BF16)
    rw_ref[...] = _dot(xm, w_ref[:, NA_COLS:NA_COLS + RW_COLS])


def _inproj_call(x2d, mod3, tables, w_in_bf16, *, tm, mod_index, table_index, name):
    rows, d = x2d.shape
    cos, sa, sb = tables
    tw = cos.shape[1]
    ncols = w_in_bf16.shape[1]
    na_spec = pl.BlockSpec((tm, NA_WIDTH), lambda i: (i, 0))
    table_spec = pl.BlockSpec((tm, tw), lambda i: (table_index(i), 0))
    na_shape = jax.ShapeDtypeStruct((rows, NA_WIDTH), BF16)
    return pl.pallas_call(
        functools.partial(_inproj_kernel, d),
        grid=(rows // tm,),
        in_specs=[pl.BlockSpec((tm, d), lambda i: (i, 0)),
                  pl.BlockSpec((None, 1, mod3.shape[2]), lambda i: (mod_index(i), 0, 0)),
                  table_spec, table_spec, table_spec,
                  pl.BlockSpec((d, ncols), lambda i: (0, 0))],
        out_specs=[na_spec, na_spec, na_spec, na_spec, pl.BlockSpec((tm, RW_COLS), lambda i: (i, 0))],
        out_shape=[na_shape, na_shape, na_shape, na_shape, jax.ShapeDtypeStruct((rows, RW_COLS), F32)],
        compiler_params=_cparams("parallel"),
        name=name,
    )(x2d, mod3, cos, sa, sb, w_in_bf16)


def _window_group(i, n_groups):
    return jnp.clip((Q_ROWS * i - WIN_H // 2) // KV_GROUP, 0, n_groups - WIN_GROUPS)


def _build_bias_table(rpb_ref, bias_ref):
    n_rows = 2 * WIN_H - 1
    lane = lax.broadcasted_iota(jnp.int32, (GRID_W, 2 * GRID_W), 1)
    qcol = lax.broadcasted_iota(jnp.int32, (GRID_W, 2 * GRID_W), 0)
    kcol = lane % GRID_W
    c0 = jnp.clip(qcol - WIN_W // 2, 0, GRID_W - WIN_W)
    inside = jnp.logical_and(kcol >= c0, kcol < c0 + WIN_W)
    first_half = lane < GRID_W
    for h in range(NA_HEADS):
        def toeplitz(ro, shift):
            row = jnp.broadcast_to(rpb_ref[ro, h:h + 1, :], (GRID_W, 2 * GRID_W))
            return pltpu.roll(row, shift, 1, stride=1, stride_axis=0)
        for ro in range(n_rows - 1):
            pair = jnp.where(first_half, toeplitz(ro, GRID_W + 1), toeplitz(ro + 1, 1))
            bias_ref[ro, h] = jnp.where(inside, pair, NEG_BIAS)


def _na_kernel(grid_rows, q_ref, qr_ref, *refs):
    k_refs = refs[0:WIN_GROUPS]
    v_refs = refs[WIN_GROUPS:2 * WIN_GROUPS]
    kc_ref, vc_ref, rpb_ref, o_ref, kwin, vwin, bias_ref = refs[2 * WIN_GROUPS:]
    i = pl.program_id(1)

    @pl.when(i == 0)
    def _():
        _build_bias_table(rpb_ref, bias_ref)

    grp = KV_GROUP * GRID_W
    for g, (kr, vr) in enumerate(zip(k_refs, v_refs)):
        kwin[g * grp:(g + 1) * grp, :] = kr[...]
        vwin[g * grp:(g + 1) * grp, :] = vr[...]
    n_groups = grid_rows // KV_GROUP
    win_row0 = KV_GROUP * _window_group(i, n_groups)
    win_keys = WIN_H * GRID_W
    lane = lax.broadcasted_iota(jnp.int32, (GRID_W, 2 * HEAD_DIM), 1)
    even = lane < HEAD_DIM

    pairs = range(NA_HEADS // 2)
    lanes = [slice(pair * 2 * HEAD_DIM, (pair + 1) * 2 * HEAD_DIM) for pair in pairs]

    def rows_body(jj, carry):
        units = []
        for u in range(NA_UNROLL):
            j = NA_UNROLL * jj + u
            irow = Q_ROWS * i + j
            r0 = jnp.clip(irow - WIN_H // 2, 0, grid_rows - WIN_H)
            koff = pl.multiple_of((r0 - win_row0) * GRID_W, GRID_W)
            brow = (WIN_H - 1) - (irow - r0)
            qoff = pl.multiple_of(j * GRID_W, GRID_W)
            units += [(koff, brow, qoff, pair) for pair in pairs]

        def split_heads(z):
            zero = jnp.zeros_like(z)
            return jnp.concatenate([jnp.where(even, z, zero), jnp.where(even, zero, z)], axis=0)

        def scores(unit):
            koff, brow, qoff, pair = unit
            bias = jnp.concatenate(
                [jnp.concatenate([bias_ref[brow + 2 * m, 2 * pair], bias_ref[brow + 2 * m, 2 * pair + 1]], axis=0)
                 for m in range(WIN_H // 2)], axis=1)
            s_l = _dot_nt(split_heads(qr_ref[pl.ds(qoff, GRID_W), lanes[pair]]),
                          kwin[pl.ds(koff, win_keys), lanes[pair]]) + bias
            s_c = _dot_nt(split_heads(q_ref[pl.ds(qoff, GRID_W), lanes[pair]]), kc_ref[:, lanes[pair]])
            return s_l, s_c

        def attend(unit, s_l, s_c):
            koff, brow, qoff, pair = unit
            mx = jnp.maximum(jnp.max(s_l, axis=-1, keepdims=True), jnp.max(s_c, axis=-1, keepdims=True))
            p_l = jnp.exp2(s_l - mx)
            p_c = jnp.exp2(s_c - mx)
            denom = jnp.sum(p_l, axis=-1, keepdims=True) + jnp.sum(p_c, axis=-1, keepdims=True)
            o = (_dot(p_l.astype(BF16), vwin[pl.ds(koff, win_keys), lanes[pair]])
                 + _dot(p_c.astype(BF16), vc_ref[:, lanes[pair]])) / denom
            o_ref[pl.ds(qoff, GRID_W), lanes[pair]] = jnp.where(
                even, o[0:GRID_W], o[GRID_W:2 * GRID_W]).astype(BF16)

        pending = [scores(unit) for unit in units[:NA_LEAD]]
        for n, unit in enumerate(units):
            if n + NA_LEAD < len(units):
                pending.append(scores(units[n + NA_LEAD]))
            attend(unit, *pending[n])
        return carry

    lax.fori_loop(0, Q_ROWS // NA_UNROLL, rows_body, 0)


def _na_call(q, qr, kr, v, kc, vc, rpb_rows, *, batch, seq, ctx_len):
    grid_rows = seq // GRID_W
    n_blocks = grid_rows // Q_ROWS
    n_groups = grid_rows // KV_GROUP
    blk = Q_ROWS * GRID_W
    grp = KV_GROUP * GRID_W

    def group_spec(g):
        return pl.BlockSpec((grp, NA_WIDTH),
                            lambda b, i: (b * n_groups + _window_group(i, n_groups) + g, 0))

    q_spec = pl.BlockSpec((blk, NA_WIDTH), lambda b, i: (b * n_blocks + i, 0))
    c_spec = pl.BlockSpec((ctx_len, NA_WIDTH), lambda b, i: (b, 0))
    return pl.pallas_call(
        functools.partial(_na_kernel, grid_rows),
        grid=(batch, n_blocks),
        in_specs=[q_spec, q_spec] + [group_spec(g) for g in range(WIN_GROUPS)] * 2
                 + [c_spec, c_spec, pl.BlockSpec(rpb_rows.shape, lambda b, i: (0, 0, 0))],
        out_specs=q_spec,
        out_shape=jax.ShapeDtypeStruct((batch * seq, NA_WIDTH), BF16),
        scratch_shapes=[pltpu.VMEM((WIN_GROUPS * grp, NA_WIDTH), BF16)] * 2 + [
                        pltpu.VMEM((2 * WIN_H - 2, NA_HEADS, GRID_W, 2 * GRID_W), F32)],
        compiler_params=_cparams("parallel", "arbitrary"),
        name="natten",
    )(q, qr, *([kr] * WIN_GROUPS), *([v] * WIN_GROUPS), kc, vc, rpb_rows)


def _block_diag(x_cat, bd_mask):
    xb = x_cat.astype(BF16)
    tiled = jnp.concatenate([xb] * (QUAD // CHUNK), axis=0)
    return jnp.where(bd_mask, tiled, jnp.zeros_like(tiled))


def _diag_blocks(x, head_of_lane):
    n = QUAD // HEAD_DIM
    out = x[(n - 1) * HEAD_DIM:n * HEAD_DIM]
    for h in range(n - 2, -1, -1):
        out = jnp.where(head_of_lane == h, x[h * HEAD_DIM:(h + 1) * HEAD_DIM], out)
    return out


def _rwkv_kernel(reverse, with_gate, n_ctx, n_blocks,
                 lat_ref, lat_prev_ref, lat_next_ref, ctx_ref, ctx_prev_ref, ctx_next_ref,
                 mup_ref, mun_ref, w0_ref, w2_ref, a0_ref, a2_ref, g2_ref,
                 kk_ref, ka_ref, rk_ref, ones_ref, *refs):
    if with_gate:
        acc_refs = (None, None)
        y_ref, bonus_ref, gate_ref = refs[0:3]
        refs = refs[3:]
    else:
        acc_refs = refs[0:2]
        y_ref, bonus_ref = refs[2:4]
        gate_ref = None
        refs = refs[4:]
    h_ref = refs[0]
    sets = (refs[1:7], refs[7:13])
    psets = (refs[13:17], refs[17:21])
    s = pl.program_id(1)

    last = n_blocks + 1

    def run(parity, stages):
        _rwkv_step(reverse, n_ctx, n_blocks, s, stages, sets[parity], sets[1 - parity], psets[parity],
                   psets[1 - parity], lat_ref, lat_prev_ref, lat_next_ref,
                   ctx_ref, ctx_prev_ref, ctx_next_ref, mup_ref, mun_ref, w0_ref, w2_ref, a0_ref, a2_ref,
                   g2_ref, kk_ref, ka_ref, rk_ref, ones_ref, acc_refs, y_ref, bonus_ref, gate_ref, h_ref)

    @pl.when(s == 0)
    def _():
        h_ref[...] = jnp.zeros_like(h_ref)
        for ref in psets[0]:
            ref[...] = jnp.zeros_like(ref)
        run(0, ("prep",))

    @pl.when(s == last)
    def _():
        run(last % 2, ("state",))

    inner = jnp.logical_and(s > 0, s < last)

    @pl.when(jnp.logical_and(inner, s % 2 == 0))
    def _():
        run(0, ("prep", "chains", "state"))

    @pl.when(jnp.logical_and(inner, s % 2 == 1))
    def _():
        run(1, ("prep", "chains", "state"))


def _rwkv_step(reverse, n_ctx, n_blocks, s, stages, wset, rset, pwset, prset, lat_ref, lat_prev_ref, lat_next_ref,
               ctx_ref, ctx_prev_ref, ctx_next_ref, mup_ref, mun_ref, w0_ref, w2_ref, a0_ref, a2_ref,
               g2_ref, kk_ref, ka_ref, rk_ref, ones_ref, acc_refs, y_ref, bonus_ref, gate_ref, h_ref):
    sp = jnp.minimum(s, n_blocks - 1)
    if reverse:
        n = jnp.where(sp < n_ctx, n_ctx - 1 - sp, n_blocks + n_ctx - 1 - sp)
    else:
        n = sp

    rows = lat_ref.shape[0]
    n_ch = rows // CHUNK
    w = RW_WIDTH
    t = {}

    is_ctx = n < n_ctx
    has_prev = jnp.logical_and(n != 0, n != n_ctx)
    has_next = jnp.logical_and(n != n_ctx - 1, n != n_blocks - 1)
    row = lax.broadcasted_iota(jnp.int32, (rows, 1), 0)

    def shifted(c0, c1):
        p = jnp.where(is_ctx, ctx_ref[:, c0:c1], lat_ref[:, c0:c1])
        prow = jnp.where(is_ctx, ctx_prev_ref[HALO - 1:HALO, c0:c1], lat_prev_ref[HALO - 1:HALO, c0:c1])
        nrow = jnp.where(is_ctx, ctx_next_ref[0:1, c0:c1], lat_next_ref[0:1, c0:c1])
        prow = jnp.where(has_prev, prow, 0.0)
        nrow = jnp.where(has_next, nrow, 0.0)
        prev = jnp.where(row == 0, prow, pltpu.roll(p, 1, 0))
        nxt = jnp.where(row == rows - 1, nrow, pltpu.roll(p, rows - 1, 0))
        return p + mup_ref[:, c0:c1] * (prev - p) + mun_ref[:, c0:c1] * (nxt - p)

    def prep_lora():
        lora = shifted(3 * w, 3 * w + 3 * LORA_W)
        pw = lora[:, 0:LORA_W]
        pa = lora[:, LORA_W:2 * LORA_W]
        t["log_decay"] = -math.exp(-0.5) * jax.nn.sigmoid(
            w0_ref[...] + _dot(jnp.tanh(pw).astype(BF16), w2_ref[...]))
        t["a"] = jax.nn.sigmoid(a0_ref[...] + _dot(pa.astype(BF16), a2_ref[...]))
        if gate_ref is not None:
            gate_ref[...] = _dot(jax.nn.sigmoid(lora[:, 2 * LORA_W:3 * LORA_W]).astype(BF16), g2_ref[...])

    def prep_keys():
        k = shifted(w, 2 * w)
        ones = ones_ref[...]
        kk_raw = k * kk_ref[...]
        kk = kk_raw * lax.rsqrt(jnp.maximum(_head_sums(kk_raw * kk_raw, ones), 1e-24))
        t.update(kk=kk, kd=k * (1.0 + (t["a"] - 1.0) * ka_ref[...]), bb=t["a"] * kk)

    def prep_bonus():
        r = shifted(0, w)
        v = shifted(2 * w, 3 * w)
        bonus = _head_sums(r * t["kd"] * rk_ref[...], ones_ref[...]) * v
        bonus_ref[...] = bonus if acc_refs[1] is None else bonus + acc_refs[1][...]
        t.update(r=r, v=v)

    def finish_prep():
        r, v, kk, kd, bb, log_decay = (t[name] for name in ("r", "v", "kk", "kd", "bb", "log_decay"))
        ti = lax.broadcasted_iota(jnp.int32, (rows, rows), 0)
        si = lax.broadcasted_iota(jnp.int32, (rows, rows), 1)
        same_chunk = (ti // CHUNK) == (si // CHUNK)
        tri = jnp.where(jnp.logical_and(same_chunk, (si >= ti) if reverse else (si <= ti)), 1.0, 0.0).astype(BF16)
        lw_hi, lw_lo = _split(log_decay, 2)
        cum = _dot(tri, lw_hi) + _dot(tri, lw_lo)
        e_neg = jnp.exp(-cum)
        last = 0 if reverse else CHUNK - 1
        w_v, w_kkt, w_rt, w_kh, w_bh, w_et = wset
        w_v[...] = v.astype(BF16)
        w_kkt[...] = (kk * jnp.exp(cum - log_decay)).astype(BF16)
        w_rt[...] = r * jnp.exp(cum)
        w_kh[...] = (kd * e_neg).astype(BF16)
        w_bh[...] = (bb * e_neg).astype(BF16)
        for c in range(n_ch):
            w_et[c:c + 1, :] = jnp.exp(cum[c * CHUNK + last:c * CHUNK + last + 1, :])

    prep_at = {"start": prep_lora, "gram": prep_keys, "akv": prep_bonus, "dbl1": finish_prep}

    def issue_prep(point):
        if point in prep_at and "prep" in stages:
            prep_at[point]()

    issue_prep("start")
    r_v, r_kkt, r_rt, r_kh, r_bh, r_et = rset
    trow = lax.broadcasted_iota(jnp.int32, (CHUNK, QUAD), 0)
    tcol = lax.broadcasted_iota(jnp.int32, (CHUNK, QUAD), 1) % CHUNK
    strict = (trow < tcol) if reverse else (trow > tcol)
    incl = (trow <= tcol) if reverse else (trow >= tcol)
    on_diag = trow == tcol
    eye_cat = jnp.where(on_diag, 1.0, 0.0)
    brow = lax.broadcasted_iota(jnp.int32, (QUAD, QUAD), 0)
    bcol = lax.broadcasted_iota(jnp.int32, (QUAD, QUAD), 1)
    bd_mask = (brow // HEAD_DIM) == (bcol // HEAD_DIM)
    head_of_lane = lax.broadcasted_iota(jnp.int32, (CHUNK, QUAD), 1) // HEAD_DIM
    n_doublings = int(math.log2(CHUNK)) - 1
    heads = range(QUAD // HEAD_DIM)
    quads = range(RW_WIDTH // QUAD)
    order = list(range(n_ch - 1, -1, -1) if reverse else range(n_ch))

    chains = [(c, qd) for c in order for qd in quads]

    def rsl(c):
        return slice(c * CHUNK, (c + 1) * CHUNK)

    def lsl(qd):
        return slice(qd * QUAD, (qd + 1) * QUAD)

    r_y0, r_qt, r_g, r_h0 = prset
    h = {qd: h_ref[qd] for qd in quads}
    issued = [0]

    def issue_state(upto):
        while "state" in stages and issued[0] < min(upto, len(order)):
            c = order[issued[0]]
            issued[0] += 1
            for qd in quads:
                lhs = jnp.concatenate([r_g[rsl(c), lsl(qd)], r_qt[rsl(c), lsl(qd)]], axis=0)
                o2 = _dot(lhs, _block_diag(h[qd], bd_mask))
                h[qd] = o2[0:CHUNK] + r_h0[rsl(c), lsl(qd)]
                y = r_y0[rsl(c), lsl(qd)] + o2[CHUNK:2 * CHUNK]
                y_ref[rsl(c), lsl(qd)] = y if acc_refs[0] is None else y + acc_refs[0][rsl(c), lsl(qd)]
            if issued[0] == len(order):
                for qd in quads:
                    h_ref[qd] = h[qd]

    issue_state(1)
    if "chains" not in stages:
        for point in ["gram", "akv", "m1"] + ["dbl%d" % step for step in range(n_doublings)]:
            issue_prep(point)
        issue_state(len(order))
        return
    v_q = {ch: r_v[rsl(ch[0]), lsl(ch[1])] for ch in chains}
    kkt_q = {ch: r_kkt[rsl(ch[0]), lsl(ch[1])] for ch in chains}
    rt_q = {ch: r_rt[rsl(ch[0]), lsl(ch[1])] for ch in chains}

    a_k, a_b, b_k, b_b = {}, {}, {}, {}
    for ch in chains:
        kh = r_kh[rsl(ch[0]), lsl(ch[1])].astype(BF16)
        bh = r_bh[rsl(ch[0]), lsl(ch[1])].astype(BF16)
        zero = jnp.zeros_like(kh)
        rhs = jnp.concatenate([jnp.where(head_of_lane == h, kh, zero) for h in heads]
                              + [jnp.where(head_of_lane == h, bh, zero) for h in heads], axis=0)
        lhs = jnp.concatenate([kkt_q[ch], rt_q[ch].astype(BF16)], axis=0)
        gram = _dot_nt(lhs, rhs)
        a_k[ch] = jnp.where(strict, gram[0:CHUNK, 0:QUAD], 0.0)
        a_b[ch] = jnp.where(strict, gram[0:CHUNK, QUAD:2 * QUAD], 0.0)
        b_k[ch] = jnp.where(incl, gram[CHUNK:2 * CHUNK, 0:QUAD], 0.0)
        b_b[ch] = jnp.where(incl, gram[CHUNK:2 * CHUNK, QUAD:2 * QUAD], 0.0).astype(BF16)
    issue_prep("gram")
    issue_state(2)

    akv, bkv = {}, {}
    for ch in chains:
        both = _dot(jnp.concatenate([a_k[ch], b_k[ch]], axis=0).astype(BF16), _block_diag(v_q[ch], bd_mask))
        akv[ch] = both[0:CHUNK]
        bkv[ch] = both[CHUNK:2 * CHUNK]
    issue_prep("akv")
    issue_state(3)

    m = {ch: -a_b[ch] for ch in chains}
    t_inv = {ch: eye_cat + m[ch] for ch in chains}
    for ch in chains:
        m[ch] = _dot(m[ch].astype(BF16), _block_diag(m[ch], bd_mask))
    issue_prep("m1")
    issue_state(4)
    for step in range(n_doublings):
        final = step == n_doublings - 1
        issue_prep("dbl%d" % step)
        for ch in chains:
            m_bd = _block_diag(m[ch], bd_mask)
            if final:
                t_inv[ch] = t_inv[ch] + _dot(t_inv[ch].astype(BF16), m_bd)
            else:
                both = _dot(jnp.concatenate([m[ch], t_inv[ch]], axis=0).astype(BF16), m_bd)
                m[ch] = both[0:CHUNK]
                t_inv[ch] = t_inv[ch] + both[CHUNK:2 * CHUNK]

    p1, tkk = {}, {}
    for ch in chains:
        t_b = t_inv[ch].astype(BF16)
        p1[ch] = _dot(t_b, _block_diag(akv[ch], bd_mask))
        tkk[ch] = _dot(t_b, _block_diag(kkt_q[ch], bd_mask))

    w_y0, w_qt, w_g, w_h0 = pwset
    for ch in chains:
        c, qd = ch
        w_y0[rsl(c), lsl(qd)] = bkv[ch] - _dot(b_b[ch], _block_diag(p1[ch], bd_mask))
        w_qt[rsl(c), lsl(qd)] = (rt_q[ch] - _dot(b_b[ch], _block_diag(tkk[ch], bd_mask))).astype(BF16)
        e_q = r_et[c:c + 1, lsl(qd)]
        lhs_t = jnp.concatenate([r_kh[rsl(c), lsl(qd)] * e_q, r_bh[rsl(c), lsl(qd)] * e_q], axis=0).astype(BF16)
        rhs_t = jnp.concatenate([jnp.concatenate([v_q[ch], jnp.zeros_like(v_q[ch])], axis=1),
                                 jnp.concatenate([-p1[ch], -tkk[ch]], axis=1)], axis=0).astype(BF16)
        hg = _dot_tn(lhs_t, rhs_t)
        w_h0[rsl(c), lsl(qd)] = _diag_blocks(hg[:, 0:QUAD], head_of_lane)
        w_g[rsl(c), lsl(qd)] = (_diag_blocks(hg[:, QUAD:2 * QUAD], head_of_lane)
                                + jnp.where(on_diag, e_q, 0.0)).astype(BF16)
    issue_state(len(order))


def _rwkv_call(rw_lat, rw_ctx, weights, *, reverse, with_gate, batch, seq, ctx_len, acc=None):
    assert CHUNK == HEAD_DIM
    blk_rows = RW_BLOCK_CHUNKS * CHUNK
    assert ctx_len % blk_rows == 0 and seq % blk_rows == 0
    n_ctx = ctx_len // blk_rows
    n_lat = seq // blk_rows
    n_chunks = n_ctx + n_lat
    halo_per_blk = blk_rows // HALO

    def block_of(s):
        if reverse:
            return jnp.where(s < n_ctx, n_ctx - 1 - s, n_chunks + n_ctx - 1 - s)
        return s

    def stream_specs(first, count):
        n_halo = batch * count * halo_per_blk

        def blk(b, s):
            return b * count + jnp.clip(block_of(jnp.minimum(s, n_chunks - 1)) - first, 0, count - 1)

        return [pl.BlockSpec((blk_rows, RW_COLS), lambda b, s: (blk(b, s), 0)),
                pl.BlockSpec((HALO, RW_COLS), lambda b, s: (jnp.maximum(blk(b, s) * halo_per_blk - 1, 0), 0)),
                pl.BlockSpec((HALO, RW_COLS),
                             lambda b, s: (jnp.minimum((blk(b, s) + 1) * halo_per_blk, n_halo - 1), 0))]

    def out_block(b, step):
        lat = jnp.maximum(step, n_ctx) - n_ctx
        return (b * n_lat + (n_lat - 1 - lat if reverse else lat), 0)

    def const_spec(a):
        return pl.BlockSpec(a.shape, lambda b, s: (0,) * a.ndim)

    prep_spec = pl.BlockSpec((blk_rows, RW_WIDTH), lambda b, s: out_block(b, jnp.minimum(s, n_chunks - 1)))
    y_spec = pl.BlockSpec((blk_rows, RW_WIDTH), lambda b, s: out_block(b, jnp.maximum(s - 2, 0)))
    out_shape = jax.ShapeDtypeStruct((batch * seq, RW_WIDTH), F32)
    n_out = 3 if with_gate else 2
    assert with_gate == (acc is None)
    acc_specs = [] if acc is None else [y_spec, prep_spec]
    assert RW_BLOCK_CHUNKS <= SUBLANES
    def block_scratch(dtype):
        return pltpu.VMEM((blk_rows, RW_WIDTH), dtype)

    term_set = [block_scratch(BF16), block_scratch(BF16), block_scratch(F32), block_scratch(BF16),
                block_scratch(BF16), pltpu.VMEM((SUBLANES, RW_WIDTH), F32)]
    chain_set = [block_scratch(F32), block_scratch(BF16), block_scratch(BF16), block_scratch(F32)]
    return pl.pallas_call(
        functools.partial(_rwkv_kernel, reverse, with_gate, n_ctx, n_chunks),
        grid=(batch, n_chunks + 2),
        in_specs=stream_specs(n_ctx, n_lat) + stream_specs(0, n_ctx) + [const_spec(a) for a in weights] + acc_specs,
        out_specs=[y_spec] + [prep_spec] * (n_out - 1),
        out_shape=[out_shape] * n_out,
        scratch_shapes=[pltpu.VMEM((RW_WIDTH // QUAD, CHUNK, QUAD), F32)] + term_set * 2 + chain_set * 2,
        compiler_params=_cparams("parallel", "arbitrary"),
        name="rwkv_bwd" if reverse else "rwkv_fwd",
    )(rw_lat, rw_lat, rw_lat, rw_ctx, rw_ctx, rw_ctx, *weights, *(acc or ()))


def _tail_kernel(d_model, alpha, ff_chunk, yna_ref, y_ref, bonus_ref, gate_ref, x_ref, moda_ref, modc_ref,
                 gng_ref, gnb_ref, wo_ref, ln1g_ref, ln1b_ref, w1_ref, w2_ref, ln2g_ref, ln2b_ref,
                 o_ref, x1_a, xm_a, acc_a, x1_b, xm_b, acc_b):
    i = pl.program_id(0)
    last = pl.num_programs(0) - 1

    def step(stages, x1_cur, xm_cur, acc_cur, xm_prev, acc_prev):
        inv_n = 1.0 / HEAD_DIM
        n_chunks = w1_ref.shape[1] // ff_chunk
        t = {}

        def lane_head_sums(z):
            lane = lax.broadcasted_iota(jnp.int32, (z.shape[0], 2 * HEAD_DIM), 1)
            first = lane < HEAD_DIM
            slabs = []
            for p in range(z.shape[1] // (2 * HEAD_DIM)):
                zs = z[:, p * 2 * HEAD_DIM:(p + 1) * 2 * HEAD_DIM]
                both = jnp.sum(zs, axis=-1, keepdims=True)
                lo = jnp.sum(jnp.where(first, zs, 0.0), axis=-1, keepdims=True)
                slabs.append(jnp.where(first, lo, both - lo))
            return jnp.concatenate(slabs, axis=1)

        def readout_stats():
            y = y_ref[...]
            mu = lane_head_sums(y) * inv_n
            yc = y - mu
            t.update(yc=yc, var=lane_head_sums(yc * yc) * inv_n)

        def out_projection():
            yn = t["yc"] * lax.rsqrt(t["var"] + GN_EPS) * gng_ref[...] + gnb_ref[...]
            y_rw = ((yn + bonus_ref[...]) * gate_ref[...]).astype(BF16)
            t["proj"] = (_dot(yna_ref[...], wo_ref[0:NA_WIDTH, :])
                         + _dot(y_rw, wo_ref[NA_WIDTH:NA_WIDTH + RW_WIDTH, :]))

        half_rows = x_ref.shape[0] // 2

        def norm_modulate(part):
            rows = slice(part * half_rows, (part + 1) * half_rows)
            mod = moda_ref[...]
            g1 = mod[:, 2 * d_model:3 * d_model]
            shift = mod[:, 3 * d_model:4 * d_model]
            scale = mod[:, 4 * d_model:5 * d_model]
            x1 = (_normalize(alpha * x_ref[rows, :] + g1 * t["proj"][rows, :], LN_EPS) * ln1g_ref[...]
                  + ln1b_ref[...])
            x1_cur[rows, :] = x1
            xm_cur[rows, :] = (_normalize(x1, LN_EPS) * (1.0 + scale) + shift).astype(BF16)

        def final_norm(part):
            rows = slice(part * half_rows, (part + 1) * half_rows)
            g2 = modc_ref[...][:, 5 * d_model:6 * d_model]
            o_ref[rows, :] = (_normalize(alpha * x1_cur[rows, :] + g2 * acc_cur[rows, :], LN_EPS) * ln2g_ref[...]
                              + ln2b_ref[...])

        others = [(min(k, n_chunks - 1), name, stage) for k, name, stage in
                  ((0, "a", readout_stats), (0, "a", out_projection),
                   (1, "c", functools.partial(final_norm, 0)), (1, "a", functools.partial(norm_modulate, 0)),
                   (2, "c", functools.partial(final_norm, 1)), (2, "a", functools.partial(norm_modulate, 1)))]
        if "b" not in stages:
            for _, name, stage in others:
                if name in stages:
                    stage()
            return
        xm = xm_prev[...]
        acc = None
        for c in range(n_chunks):
            c0 = c * ff_chunk
            hid = jnp.maximum(_dot(xm, w1_ref[:, c0:c0 + ff_chunk]), 0.0)
            part = _dot((hid * hid).astype(BF16), w2_ref[c0:c0 + ff_chunk, :])
            acc = part if acc is None else acc + part
            for after, name, stage in others:
                if after == c and name in stages:
                    stage()
        acc_prev[...] = acc

    slots = ((x1_a, xm_a, acc_a, xm_b, acc_b), (x1_b, xm_b, acc_b, xm_a, acc_a))

    @pl.when(i == 0)
    def _():
        x1_b[...] = jnp.zeros_like(x1_b)
        acc_b[...] = jnp.zeros_like(acc_b)
        step("a", *slots[0])

    inner = jnp.logical_and(i > 0, i < last)

    @pl.when(jnp.logical_and(inner, i % 2 == 0))
    def _():
        step("abc", *slots[0])

    @pl.when(jnp.logical_and(inner, i % 2 == 1))
    def _():
        step("abc", *slots[1])

    @pl.when(jnp.logical_and(i == last, i % 2 == 0))
    def _():
        step("c", *slots[0])

    @pl.when(jnp.logical_and(i == last, i % 2 == 1))
    def _():
        step("c", *slots[1])


def _tail_call(y_na, y, bonus, gate, x2d, mod3, consts, *, tm, seq, alpha):
    rows, d = x2d.shape
    tiles_per_batch = seq // tm
    n_tiles = rows // tm

    def tile_a(i):
        return jnp.minimum(i, n_tiles - 1)

    def tile_c(i):
        return jnp.maximum(i - 2, 0)

    half = pl.BlockSpec((tm, RW_WIDTH), lambda i: (tile_a(i), 0))

    def resident(a):
        return pl.BlockSpec(a.shape, lambda i: (0,) * a.ndim, pipeline_mode=pl.Buffered(1))

    return pl.pallas_call(
        functools.partial(_tail_kernel, d, alpha, FF_CHUNK),
        grid=(n_tiles + 2,),
        in_specs=[half] * 4 + [pl.BlockSpec((tm, d), lambda i: (tile_a(i), 0)),
                               pl.BlockSpec((None, 1, mod3.shape[2]), lambda i: (tile_a(i) // tiles_per_batch, 0, 0)),
                               pl.BlockSpec((None, 1, mod3.shape[2]), lambda i: (tile_c(i) // tiles_per_batch, 0, 0))]
                 + [resident(a) for a in consts],
        out_specs=pl.BlockSpec((tm, d), lambda i: (tile_c(i), 0)),
        out_shape=jax.ShapeDtypeStruct((rows, d), F32),
        scratch_shapes=[pltpu.VMEM((tm, d), F32), pltpu.VMEM((tm, d), BF16), pltpu.VMEM((tm, d), F32)] * 2,
        compiler_params=_cparams("arbitrary"),
        name="tail",
    )(y_na, y, bonus, gate, x2d, mod3, mod3, *consts)


def _rope_tables(seq):
    f = HEAD_DIM // 4
    t = np.arange(seq)
    row = (t // GRID_W).astype(np.float32)
    col = (t % GRID_W).astype(np.float32)
    inv = (ROPE_BASE ** (-np.arange(f, dtype=np.float32) / f)).astype(np.float32)
    ang_r = row[:, None] * inv[None, :]
    ang_c = col[:, None] * inv[None, :]
    zero = np.zeros_like(ang_r)
    cos = np.concatenate([np.cos(ang_r), np.cos(ang_r), np.cos(ang_c), np.cos(ang_c)], axis=1)
    sa = np.concatenate([-np.sin(ang_r), zero, -np.sin(ang_c), zero], axis=1)
    sb = np.concatenate([zero, np.sin(ang_r), zero, np.sin(ang_c)], axis=1)
    return tuple(jnp.asarray(np.concatenate([z, z], axis=1), F32) for z in (cos, sa, sb))


def _bias_rows(rpb):
    pad = GRID_W - WIN_W
    padded = jnp.pad(rpb.astype(F32) * LOG2_E, ((0, 0), (0, 0), (pad, pad + 1)))
    return padded.transpose(1, 0, 2)


def _pad_lora(w, d):
    z = jnp.zeros_like(w[d])
    return jnp.concatenate([w[0] if d == 0 else z, w[1] if d == 1 else z], axis=0)


def kernel(x, c, ctx, c_ctx, w_mod, b_mod, w_in, w_out, ln1_g, ln1_b, mlp_w1, mlp_w2, ln2_g, ln2_b, na_rpb,
           rw_mu_prev, rw_mu_next, rw_w0, rw_w2, rw_a0, rw_a2, rw_g2, rw_k_k, rw_k_a, rw_r_k, rw_gn_g, rw_gn_b):
    depth = w_mod.shape[0]
    assert depth == 1, "single-layer trunk only (the context stream is never updated)"
    batch, seq, d = x.shape
    ctx_len = ctx.shape[1]
    alpha = (2 * depth) ** 0.25
    tm = ROW_TILE
    tm_ctx = math.gcd(ctx_len, ROW_TILE)
    assert seq % (Q_ROWS * GRID_W) == 0 and seq % tm == 0 and tm_ctx % 8 == 0
    assert batch + 1 <= SUBLANES

    cc = jnp.zeros((SUBLANES, d), F32).at[:batch].set(c).at[batch].set(c_ctx)
    mod3 = _mod_call(cc, w_mod[0], b_mod[0]).reshape(SUBLANES, 1, N_MOD * d)

    w_in_b = w_in[0].astype(BF16)
    tiles_lat = seq // tm
    x2d = x.reshape(batch * seq, d)
    q, qr, kr, v, rw_lat = _inproj_call(
        x2d, mod3, _rope_tables(seq), w_in_b, tm=tm,
        mod_index=lambda i: i // tiles_lat, table_index=lambda i: i % tiles_lat, name="inproj")
    ident = tuple(jnp.full((tm_ctx, 2 * HEAD_DIM), val, F32) for val in (1.0, 0.0, 0.0))
    _, _, kc, vc, rw_ctx = _inproj_call(
        ctx.reshape(batch * ctx_len, d), mod3, ident, w_in_b, tm=tm_ctx,
        mod_index=lambda i: batch, table_index=lambda i: 0, name="inproj_ctx")

    y_na = _na_call(q, qr, kr, v, kc, vc, _bias_rows(na_rpb[0]), batch=batch, seq=seq, ctx_len=ctx_len)

    lane_head = jnp.arange(RW_WIDTH) // HEAD_DIM
    ones = (lane_head[:, None] == lane_head[None, :]).astype(BF16)
    row = lambda a: a.reshape(1, -1).astype(F32)
    def dir_weights(dirn):
        return (row(rw_mu_prev[0]), row(rw_mu_next[0]), row(rw_w0[0, dirn]),
                _pad_lora(rw_w2[0], dirn).astype(BF16), row(rw_a0[0, dirn]),
                _pad_lora(rw_a2[0], dirn).astype(BF16), rw_g2[0].astype(BF16),
                row(rw_k_k[0]), row(rw_k_a[0]), row(rw_r_k[0]), ones)

    y_f, bonus_f, gate = _rwkv_call(rw_lat, rw_ctx, dir_weights(0), reverse=False, with_gate=True,
                                    batch=batch, seq=seq, ctx_len=ctx_len)
    y_sum, bonus_sum = _rwkv_call(rw_lat, rw_ctx, dir_weights(1), reverse=True, with_gate=False,
                                  batch=batch, seq=seq, ctx_len=ctx_len, acc=(y_f, bonus_f))

    consts = (row(rw_gn_g[0]), row(rw_gn_b[0]), w_out[0].astype(BF16), row(ln1_g[0]), row(ln1_b[0]),
              mlp_w1[0].astype(BF16), mlp_w2[0].astype(BF16), row(ln2_g[0]), row(ln2_b[0]))
    out = _tail_call(y_na, y_sum, bonus_sum, gate, x2d, mod3, consts, tm=tm, seq=seq, alpha=alpha)
    return out.reshape(batch, seq, d)
```

```python
import functools
import math

import jax
import jax.numpy as jnp
import numpy as np
from jax import lax
from jax.experimental import pallas as pl
from jax.experimental.pallas import tpu as pltpu

HEAD_DIM = 64
NA_HEADS = 8
RW_HEADS = 8
NA_WIDTH = NA_HEADS * HEAD_DIM
RW_WIDTH = RW_HEADS * HEAD_DIM
GRID_W = 64
WIN_H = 8
WIN_W = 16
ROPE_BASE = 10000.0
N_DIR = 2
DECAY_LORA = 64
AAA_LORA = 64
GATE_LORA = 128
LORA_W = N_DIR * DECAY_LORA
NA_COLS = 3 * NA_WIDTH
RW_COLS = 3 * RW_WIDTH + 3 * LORA_W
N_MOD = 6
LN_EPS = 1e-6
GN_EPS = 64e-5
NEG_BIAS = -1e30

CHUNK = 64
RW_BLOCK_CHUNKS = 4
QUAD = 2 * HEAD_DIM
HALO = 8
LOG2_E = math.log2(math.e)
Q_ROWS = 8
NA_UNROLL = 4
NA_LEAD = 4
KV_GROUP = 4
WIN_GROUPS = (Q_ROWS + WIN_H) // KV_GROUP
ROW_TILE = 512
FF_CHUNK = 1024
MOD_COL_TILE = 1536
SUBLANES = 8
VMEM_LIMIT = 56 * 1024 * 1024

F32 = jnp.float32
BF16 = jnp.bfloat16


def _cparams(*sem):
    return pltpu.CompilerParams(dimension_semantics=sem, vmem_limit_bytes=VMEM_LIMIT)


def _dot(a, b):
    return jnp.dot(a, b, preferred_element_type=F32)


def _dot_nt(a, b):
    return lax.dot_general(a, b, (((1,), (1,)), ((), ())), preferred_element_type=F32)


def _dot_tn(a, b):
    return lax.dot_general(a, b, (((0,), (0,)), ((), ())), preferred_element_type=F32)


def _split(x, pieces):
    out = []
    for _ in range(pieces):
        p = x.astype(BF16)
        out.append(p)
        x = x - p.astype(F32)
    return out


def _head_sums(x, ones_bf16):
    return _dot(x.astype(BF16), ones_bf16)


def _normalize(x, eps):
    mu = jnp.mean(x, axis=-1, keepdims=True)
    xc = x - mu
    var = jnp.mean(xc * xc, axis=-1, keepdims=True)
    return xc * lax.rsqrt(var + eps)


def _mod_kernel(c_ref, w_ref, b_ref, o_ref):
    c = c_ref[...]
    s = c * jax.nn.sigmoid(c)
    o_ref[...] = _dot(s.astype(BF16), w_ref[...].astype(BF16)) + b_ref[...]


def _mod_call(cc, w_mod, b_mod):
    rows, d = cc.shape
    n = w_mod.shape[1]
    tn = MOD_COL_TILE
    assert n % tn == 0
    return pl.pallas_call(
        _mod_kernel,
        grid=(n // tn,),
        in_specs=[pl.BlockSpec((rows, d), lambda j: (0, 0)),
                  pl.BlockSpec((d, tn), lambda j: (0, j)),
                  pl.BlockSpec((1, tn), lambda j: (0, j))],
        out_specs=pl.BlockSpec((rows, tn), lambda j: (0, j)),
        out_shape=jax.ShapeDtypeStruct((rows, n), F32),
        compiler_params=_cparams("arbitrary"),
        name="mod",
    )(cc, w_mod, b_mod.reshape(1, n))


def _inproj_kernel(d_model, x_ref, mod_ref, cos_ref, sa_ref, sb_ref, w_ref, q_ref, qr_ref, kr_ref, v_ref, rw_ref):
    x = x_ref[...]
    mod = mod_ref[...]
    shift = mod[:, 0:d_model]
    scale = mod[:, d_model:2 * d_model]
    xm = (_normalize(x, LN_EPS) * (1.0 + scale) + shift).astype(BF16)

    reps = NA_WIDTH // cos_ref.shape[1]
    cos = jnp.concatenate([cos_ref[...]] * reps, axis=1)
    sa = jnp.concatenate([sa_ref[...]] * reps, axis=1)
    sb = jnp.concatenate([sb_ref[...]] * reps, axis=1)

    def rope(z):
        up = pltpu.roll(z, NA_WIDTH - HEAD_DIM // 4, 1)
        down = pltpu.roll(z, HEAD_DIM // 4, 1)
        return z * cos + up * sa + down * sb

    qk_scale = HEAD_DIM ** -0.5 * LOG2_E
    q = _dot(xm, w_ref[:, 0:NA_WIDTH]) * qk_scale
    q_ref[...] = q.astype(BF16)
    qr_ref[...] = rope(q).astype(BF16)
    k = _dot(xm, w_ref[:, NA_WIDTH:2 * NA_WIDTH])
    kr_ref[...] = rope(k).astype(BF16)
    v_ref[...] = _dot(xm, w_ref[:, 2 * NA_WIDTH:NA_COLS]).astype(BF16)
    rw_ref[...] = _dot(xm, w_ref[:, NA_COLS:NA_COLS + RW_COLS])


def _inproj_call(x2d, mod3, tables, w_in_bf16, *, tm, mod_index, table_index, name):
    rows, d = x2d.shape
    cos, sa, sb = tables
    tw = cos.shape[1]
    ncols = w_in_bf16.shape[1]
    na_spec = pl.BlockSpec((tm, NA_WIDTH), lambda i: (i, 0))
    table_spec = pl.BlockSpec((tm, tw), lambda i: (table_index(i), 0))
    na_shape = jax.ShapeDtypeStruct((rows, NA_WIDTH), BF16)
    return pl.pallas_call(
        functools.partial(_inproj_kernel, d),
        grid=(rows // tm,),
        in_specs=[pl.BlockSpec((tm, d), lambda i: (i, 0)),
                  pl.BlockSpec((None, 1, mod3.shape[2]), lambda i: (mod_index(i), 0, 0)),
                  table_spec, table_spec, table_spec,
                  pl.BlockSpec((d, ncols), lambda i: (0, 0))],
        out_specs=[na_spec, na_spec, na_spec, na_spec, pl.BlockSpec((tm, RW_COLS), lambda i: (i, 0))],
        out_shape=[na_shape, na_shape, na_shape, na_shape, jax.ShapeDtypeStruct((rows, RW_COLS), F32)],
        compiler_params=_cparams("parallel"),
        name=name,
    )(x2d, mod3, cos, sa, sb, w_in_bf16)


def _window_group(i, n_groups):
    return jnp.clip((Q_ROWS * i - WIN_H // 2) // KV_GROUP, 0, n_groups - WIN_GROUPS)


def _build_bias_table(rpb_ref, bias_ref):
    n_rows = 2 * WIN_H - 1
    lane = lax.broadcasted_iota(jnp.int32, (GRID_W, 2 * GRID_W), 1)
    qcol = lax.broadcasted_iota(jnp.int32, (GRID_W, 2 * GRID_W), 0)
    kcol = lane % GRID_W
    c0 = jnp.clip(qcol - WIN_W // 2, 0, GRID_W - WIN_W)
    inside = jnp.logical_and(kcol >= c0, kcol < c0 + WIN_W)
    first_half = lane < GRID_W
    for h in range(NA_HEADS):
        def toeplitz(ro, shift):
            row = jnp.broadcast_to(rpb_ref[ro, h:h + 1, :], (GRID_W, 2 * GRID_W))
            return pltpu.roll(row, shift, 1, stride=1, stride_axis=0)
        for ro in range(n_rows - 1):
            pair = jnp.where(first_half, toeplitz(ro, GRID_W + 1), toeplitz(ro + 1, 1))
            bias_ref[ro, h] = jnp.where(inside, pair, NEG_BIAS)


def _na_kernel(grid_rows, q_ref, qr_ref, *refs):
    kwin, vwin, kc_ref, vc_ref, rpb_ref, o_ref, bias_ref = refs
    i = pl.program_id(1)

    @pl.when(i == 0)
    def _():
        _build_bias_table(rpb_ref, bias_ref)

    n_groups = grid_rows // KV_GROUP
    win_row0 = KV_GROUP * _window_group(i, n_groups)
    win_keys = WIN_H * GRID_W
    lane = lax.broadcasted_iota(jnp.int32, (GRID_W, 2 * HEAD_DIM), 1)
    even = lane < HEAD_DIM

    pairs = range(NA_HEADS // 2)
    lanes = [slice(pair * 2 * HEAD_DIM, (pair + 1) * 2 * HEAD_DIM) for pair in pairs]

    def rows_body(jj, carry):
        units = []
        for u in range(NA_UNROLL):
            j = NA_UNROLL * jj + u
            irow = Q_ROWS * i + j
            r0 = jnp.clip(irow - WIN_H // 2, 0, grid_rows - WIN_H)
            koff = pl.multiple_of((r0 - win_row0) * GRID_W, GRID_W)
            brow = (WIN_H - 1) - (irow - r0)
            qoff = pl.multiple_of(j * GRID_W, GRID_W)
            units += [(koff, brow, qoff, pair) for pair in pairs]

        def split_heads(z):
            zero = jnp.zeros_like(z)
            return jnp.concatenate([jnp.where(even, z, zero), jnp.where(even, zero, z)], axis=0)

        def scores(unit):
            koff, brow, qoff, pair = unit
            bias = jnp.concatenate(
                [jnp.concatenate([bias_ref[brow + 2 * m, 2 * pair], bias_ref[brow + 2 * m, 2 * pair + 1]], axis=0)
                 for m in range(WIN_H // 2)], axis=1)
            s_l = _dot_nt(split_heads(qr_ref[pl.ds(qoff, GRID_W), lanes[pair]]),
                          kwin[pl.ds(koff, win_keys), lanes[pair]]) + bias
            s_c = _dot_nt(split_heads(q_ref[pl.ds(qoff, GRID_W), lanes[pair]]), kc_ref[:, lanes[pair]])
            return s_l, s_c

        def attend(unit, s_l, s_c):
            koff, brow, qoff, pair = unit
            mx = jnp.maximum(jnp.max(s_l, axis=-1, keepdims=True), jnp.max(s_c, axis=-1, keepdims=True))
            p_l = jnp.exp2(s_l - mx)
            p_c = jnp.exp2(s_c - mx)
            denom = jnp.sum(p_l, axis=-1, keepdims=True) + jnp.sum(p_c, axis=-1, keepdims=True)
            o = (_dot(p_l.astype(BF16), vwin[pl.ds(koff, win_keys), lanes[pair]])
                 + _dot(p_c.astype(BF16), vc_ref[:, lanes[pair]])) / denom
            o_ref[pl.ds(qoff, GRID_W), lanes[pair]] = jnp.where(
                even, o[0:GRID_W], o[GRID_W:2 * GRID_W]).astype(BF16)

        pending = [scores(unit) for unit in units[:NA_LEAD]]
        for n, unit in enumerate(units):
            if n + NA_LEAD < len(units):
                pending.append(scores(units[n + NA_LEAD]))
            attend(unit, *pending[n])
        return carry

    lax.fori_loop(0, Q_ROWS // NA_UNROLL, rows_body, 0)


def _na_call(q, qr, kr, v, kc, vc, rpb_rows, *, batch, seq, ctx_len):
    grid_rows = seq // GRID_W
    n_blocks = grid_rows // Q_ROWS
    n_groups = grid_rows // KV_GROUP
    blk = Q_ROWS * GRID_W
    grp = KV_GROUP * GRID_W

    win_spec = pl.BlockSpec((pl.Element(WIN_GROUPS * grp), pl.Element(NA_WIDTH)),
                            lambda b, i: ((b * n_groups + _window_group(i, n_groups)) * grp, 0))

    q_spec = pl.BlockSpec((blk, NA_WIDTH), lambda b, i: (b * n_blocks + i, 0))
    c_spec = pl.BlockSpec((ctx_len, NA_WIDTH), lambda b, i: (b, 0))
    return pl.pallas_call(
        functools.partial(_na_kernel, grid_rows),
        grid=(batch, n_blocks),
        in_specs=[q_spec, q_spec, win_spec, win_spec]
                 + [c_spec, c_spec, pl.BlockSpec(rpb_rows.shape, lambda b, i: (0, 0, 0))],
        out_specs=q_spec,
        out_shape=jax.ShapeDtypeStruct((batch * seq, NA_WIDTH), BF16),
        scratch_shapes=[pltpu.VMEM((2 * WIN_H - 2, NA_HEADS, GRID_W, 2 * GRID_W), F32)],
        compiler_params=_cparams("parallel", "arbitrary"),
        name="natten",
    )(q, qr, kr, v, kc, vc, rpb_rows)


def _block_diag(x_cat, bd_mask):
    xb = x_cat.astype(BF16)
    tiled = jnp.concatenate([xb] * (QUAD // CHUNK), axis=0)
    return jnp.where(bd_mask, tiled, jnp.zeros_like(tiled))


def _diag_blocks(x, head_of_lane):
    n = QUAD // HEAD_DIM
    out = x[(n - 1) * HEAD_DIM:n * HEAD_DIM]
    for h in range(n - 2, -1, -1):
        out = jnp.where(head_of_lane == h, x[h * HEAD_DIM:(h + 1) * HEAD_DIM], out)
    return out


def _rwkv_kernel(reverse, with_gate, n_ctx, n_blocks,
                 lat_ref, lat_prev_ref, lat_next_ref, ctx_ref, ctx_prev_ref, ctx_next_ref,
                 mup_ref, mun_ref, w0_ref, w2_ref, a0_ref, a2_ref, g2_ref,
                 kk_ref, ka_ref, rk_ref, ones_ref, *refs):
    if with_gate:
        acc_refs = (None, None)
        y_ref, bonus_ref, gate_ref = refs[0:3]
        refs = refs[3:]
    else:
        acc_refs = refs[0:2]
        y_ref, bonus_ref = refs[2:4]
        gate_ref = None
        refs = refs[4:]
    h_ref = refs[0]
    sets = (refs[1:7], refs[7:13])
    psets = (refs[13:17], refs[17:21])
    s = pl.program_id(1)

    last = n_blocks + 1

    def run(parity, stages):
        _rwkv_step(reverse, n_ctx, n_blocks, s, stages, sets[parity], sets[1 - parity], psets[parity],
                   psets[1 - parity], lat_ref, lat_prev_ref, lat_next_ref,
                   ctx_ref, ctx_prev_ref, ctx_next_ref, mup_ref, mun_ref, w0_ref, w2_ref, a0_ref, a2_ref,
                   g2_ref, kk_ref, ka_ref, rk_ref, ones_ref, acc_refs, y_ref, bonus_ref, gate_ref, h_ref)

    @pl.when(s == 0)
    def _():
        h_ref[...] = jnp.zeros_like(h_ref)
        for ref in psets[0]:
            ref[...] = jnp.zeros_like(ref)
        run(0, ("prep",))

    @pl.when(s == last)
    def _():
        run(last % 2, ("state",))

    inner = jnp.logical_and(s > 0, s < last)

    @pl.when(jnp.logical_and(inner, s % 2 == 0))
    def _():
        run(0, ("prep", "chains", "state"))

    @pl.when(jnp.logical_and(inner, s % 2 == 1))
    def _():
        run(1, ("prep", "chains", "state"))


def _rwkv_step(reverse, n_ctx, n_blocks, s, stages, wset, rset, pwset, prset, lat_ref, lat_prev_ref, lat_next_ref,
               ctx_ref, ctx_prev_ref, ctx_next_ref, mup_ref, mun_ref, w0_ref, w2_ref, a0_ref, a2_ref,
               g2_ref, kk_ref, ka_ref, rk_ref, ones_ref, acc_refs, y_ref, bonus_ref, gate_ref, h_ref):
    sp = jnp.minimum(s, n_blocks - 1)
    if reverse:
        n = jnp.where(sp < n_ctx, n_ctx - 1 - sp, n_blocks + n_ctx - 1 - sp)
    else:
        n = sp

    rows = lat_ref.shape[0]
    n_ch = rows // CHUNK
    w = RW_WIDTH
    t = {}

    is_ctx = n < n_ctx
    has_prev = jnp.logical_and(n != 0, n != n_ctx)
    has_next = jnp.logical_and(n != n_ctx - 1, n != n_blocks - 1)
    row = lax.broadcasted_iota(jnp.int32, (rows, 1), 0)

    def shifted(c0, c1):
        p = jnp.where(is_ctx, ctx_ref[:, c0:c1], lat_ref[:, c0:c1])
        prow = jnp.where(is_ctx, ctx_prev_ref[HALO - 1:HALO, c0:c1], lat_prev_ref[HALO - 1:HALO, c0:c1])
        nrow = jnp.where(is_ctx, ctx_next_ref[0:1, c0:c1], lat_next_ref[0:1, c0:c1])
        prow = jnp.where(has_prev, prow, 0.0)
        nrow = jnp.where(has_next, nrow, 0.0)
        prev = jnp.where(row == 0, prow, pltpu.roll(p, 1, 0))
        nxt = jnp.where(row == rows - 1, nrow, pltpu.roll(p, rows - 1, 0))
        return p + mup_ref[:, c0:c1] * (prev - p) + mun_ref[:, c0:c1] * (nxt - p)

    def prep_lora():
        lora = shifted(3 * w, 3 * w + 3 * LORA_W)
        pw = lora[:, 0:LORA_W]
        pa = lora[:, LORA_W:2 * LORA_W]
        t["log_decay"] = -math.exp(-0.5) * jax.nn.sigmoid(
            w0_ref[...] + _dot(jnp.tanh(pw).astype(BF16), w2_ref[...]))
        t["a"] = jax.nn.sigmoid(a0_ref[...] + _dot(pa.astype(BF16), a2_ref[...]))
        if gate_ref is not None:
            gate_ref[...] = _dot(jax.nn.sigmoid(lora[:, 2 * LORA_W:3 * LORA_W]).astype(BF16), g2_ref[...])

    def prep_keys():
        k = shifted(w, 2 * w)
        ones = ones_ref[...]
        kk_raw = k * kk_ref[...]
        kk = kk_raw * lax.rsqrt(jnp.maximum(_head_sums(kk_raw * kk_raw, ones), 1e-24))
        t.update(kk=kk, kd=k * (1.0 + (t["a"] - 1.0) * ka_ref[...]), bb=t["a"] * kk)

    def prep_bonus():
        r = shifted(0, w)
        v = shifted(2 * w, 3 * w)
        bonus = _head_sums(r * t["kd"] * rk_ref[...], ones_ref[...]) * v
        bonus_ref[...] = bonus if acc_refs[1] is None else bonus + acc_refs[1][...]
        t.update(r=r, v=v)

    def finish_prep():
        r, v, kk, kd, bb, log_decay = (t[name] for name in ("r", "v", "kk", "kd", "bb", "log_decay"))
        ti = lax.broadcasted_iota(jnp.int32, (rows, rows), 0)
        si = lax.broadcasted_iota(jnp.int32, (rows, rows), 1)
        same_chunk = (ti // CHUNK) == (si // CHUNK)
        tri = jnp.where(jnp.logical_and(same_chunk, (si >= ti) if reverse else (si <= ti)), 1.0, 0.0).astype(BF16)
        lw_hi, lw_lo = _split(log_decay, 2)
        cum = _dot(tri, lw_hi) + _dot(tri, lw_lo)
        e_neg = jnp.exp(-cum)
        last = 0 if reverse else CHUNK - 1
        w_v, w_kkt, w_rt, w_kh, w_bh, w_et = wset
        w_v[...] = v
        w_kkt[...] = kk * jnp.exp(cum - log_decay)
        w_rt[...] = r * jnp.exp(cum)
        w_kh[...] = kd * e_neg
        w_bh[...] = bb * e_neg
        for c in range(n_ch):
            w_et[c:c + 1, :] = jnp.exp(cum[c * CHUNK + last:c * CHUNK + last + 1, :])

    prep_at = {"start": prep_lora, "gram": prep_keys, "akv": prep_bonus, "dbl1": finish_prep}

    def issue_prep(point):
        if point in prep_at and "prep" in stages:
            prep_at[point]()

    issue_prep("start")
    r_v, r_kkt, r_rt, r_kh, r_bh, r_et = rset
    trow = lax.broadcasted_iota(jnp.int32, (CHUNK, QUAD), 0)
    tcol = lax.broadcasted_iota(jnp.int32, (CHUNK, QUAD), 1) % CHUNK
    strict = (trow < tcol) if reverse else (trow > tcol)
    incl = (trow <= tcol) if reverse else (trow >= tcol)
    on_diag = trow == tcol
    eye_cat = jnp.where(on_diag, 1.0, 0.0)
    brow = lax.broadcasted_iota(jnp.int32, (QUAD, QUAD), 0)
    bcol = lax.broadcasted_iota(jnp.int32, (QUAD, QUAD), 1)
    bd_mask = (brow // HEAD_DIM) == (bcol // HEAD_DIM)
    head_of_lane = lax.broadcasted_iota(jnp.int32, (CHUNK, QUAD), 1) // HEAD_DIM
    n_doublings = int(math.log2(CHUNK)) - 1
    heads = range(QUAD // HEAD_DIM)
    quads = range(RW_WIDTH // QUAD)
    order = list(range(n_ch - 1, -1, -1) if reverse else range(n_ch))

    chains = [(c, qd) for c in order for qd in quads]

    def rsl(c):
        return slice(c * CHUNK, (c + 1) * CHUNK)

    def lsl(qd):
        return slice(qd * QUAD, (qd + 1) * QUAD)

    r_y0, r_qt, r_g, r_h0 = prset
    h = {qd: h_ref[qd] for qd in quads}
    issued = [0]

    def issue_state(upto):
        while "state" in stages and issued[0] < min(upto, len(order)):
            c = order[issued[0]]
            issued[0] += 1
            for qd in quads:
                lhs = jnp.concatenate([r_g[rsl(c), lsl(qd)], r_qt[rsl(c), lsl(qd)]], axis=0).astype(BF16)
                o2 = _dot(lhs, _block_diag(h[qd], bd_mask))
                h[qd] = o2[0:CHUNK] + r_h0[rsl(c), lsl(qd)]
                y = r_y0[rsl(c), lsl(qd)] + o2[CHUNK:2 * CHUNK]
                y_ref[rsl(c), lsl(qd)] = y if acc_refs[0] is None else y + acc_refs[0][rsl(c), lsl(qd)]
            if issued[0] == len(order):
                for qd in quads:
                    h_ref[qd] = h[qd]

    issue_state(1)
    if "chains" not in stages:
        for point in ["gram", "akv", "m1"] + ["dbl%d" % step for step in range(n_doublings)]:
            issue_prep(point)
        issue_state(len(order))
        return
    v_q = {ch: r_v[rsl(ch[0]), lsl(ch[1])] for ch in chains}
    kkt_q = {ch: r_kkt[rsl(ch[0]), lsl(ch[1])] for ch in chains}
    rt_q = {ch: r_rt[rsl(ch[0]), lsl(ch[1])] for ch in chains}

    a_k, a_b, b_k, b_b = {}, {}, {}, {}
    for ch in chains:
        kh = r_kh[rsl(ch[0]), lsl(ch[1])].astype(BF16)
        bh = r_bh[rsl(ch[0]), lsl(ch[1])].astype(BF16)
        zero = jnp.zeros_like(kh)
        rhs = jnp.concatenate([jnp.where(head_of_lane == h, kh, zero) for h in heads]
                              + [jnp.where(head_of_lane == h, bh, zero) for h in heads], axis=0)
        lhs = jnp.concatenate([kkt_q[ch], rt_q[ch]], axis=0).astype(BF16)
        gram = _dot_nt(lhs, rhs)
        a_k[ch] = jnp.where(strict, gram[0:CHUNK, 0:QUAD], 0.0)
        a_b[ch] = jnp.where(strict, gram[0:CHUNK, QUAD:2 * QUAD], 0.0)
        b_k[ch] = jnp.where(incl, gram[CHUNK:2 * CHUNK, 0:QUAD], 0.0)
        b_b[ch] = jnp.where(incl, gram[CHUNK:2 * CHUNK, QUAD:2 * QUAD], 0.0).astype(BF16)
    issue_prep("gram")
    issue_state(2)

    akv, bkv = {}, {}
    for ch in chains:
        both = _dot(jnp.concatenate([a_k[ch], b_k[ch]], axis=0).astype(BF16), _block_diag(v_q[ch], bd_mask))
        akv[ch] = both[0:CHUNK]
        bkv[ch] = both[CHUNK:2 * CHUNK]
    issue_prep("akv")
    issue_state(3)

    m = {ch: -a_b[ch] for ch in chains}
    t_inv = {ch: eye_cat + m[ch] for ch in chains}
    for ch in chains:
        m[ch] = _dot(m[ch].astype(BF16), _block_diag(m[ch], bd_mask))
    issue_prep("m1")
    issue_state(4)
    for step in range(n_doublings):
        final = step == n_doublings - 1
        issue_prep("dbl%d" % step)
        for ch in chains:
            m_bd = _block_diag(m[ch], bd_mask)
            if final:
                t_inv[ch] = t_inv[ch] + _dot(t_inv[ch].astype(BF16), m_bd)
            else:
                both = _dot(jnp.concatenate([m[ch], t_inv[ch]], axis=0).astype(BF16), m_bd)
                m[ch] = both[0:CHUNK]
                t_inv[ch] = t_inv[ch] + both[CHUNK:2 * CHUNK]

    p1, tkk = {}, {}
    for ch in chains:
        t_b = t_inv[ch].astype(BF16)
        p1[ch] = _dot(t_b, _block_diag(akv[ch], bd_mask))
        tkk[ch] = _dot(t_b, _block_diag(kkt_q[ch], bd_mask))

    w_y0, w_qt, w_g, w_h0 = pwset
    for ch in chains:
        c, qd = ch
        w_y0[rsl(c), lsl(qd)] = bkv[ch] - _dot(b_b[ch], _block_diag(p1[ch], bd_mask))
        w_qt[rsl(c), lsl(qd)] = rt_q[ch] - _dot(b_b[ch], _block_diag(tkk[ch], bd_mask))
        e_q = r_et[c:c + 1, lsl(qd)]
        lhs_t = jnp.concatenate([r_kh[rsl(c), lsl(qd)] * e_q, r_bh[rsl(c), lsl(qd)] * e_q], axis=0).astype(BF16)
        rhs_t = jnp.concatenate([jnp.concatenate([v_q[ch], jnp.zeros_like(v_q[ch])], axis=1),
                                 jnp.concatenate([-p1[ch], -tkk[ch]], axis=1)], axis=0).astype(BF16)
        hg = _dot_tn(lhs_t, rhs_t)
        w_h0[rsl(c), lsl(qd)] = _diag_blocks(hg[:, 0:QUAD], head_of_lane)
        w_g[rsl(c), lsl(qd)] = _diag_blocks(hg[:, QUAD:2 * QUAD], head_of_lane) + jnp.where(on_diag, e_q, 0.0)
    issue_state(len(order))


def _rwkv_call(rw_lat, rw_ctx, weights, *, reverse, with_gate, batch, seq, ctx_len, acc=None):
    assert CHUNK == HEAD_DIM
    blk_rows = RW_BLOCK_CHUNKS * CHUNK
    assert ctx_len % blk_rows == 0 and seq % blk_rows == 0
    n_ctx = ctx_len // blk_rows
    n_lat = seq // blk_rows
    n_chunks = n_ctx + n_lat
    halo_per_blk = blk_rows // HALO

    def block_of(s):
        if reverse:
            return jnp.where(s < n_ctx, n_ctx - 1 - s, n_chunks + n_ctx - 1 - s)
        return s

    def stream_specs(first, count):
        n_halo = batch * count * halo_per_blk

        def blk(b, s):
            return b * count + jnp.clip(block_of(jnp.minimum(s, n_chunks - 1)) - first, 0, count - 1)

        return [pl.BlockSpec((blk_rows, RW_COLS), lambda b, s: (blk(b, s), 0)),
                pl.BlockSpec((HALO, RW_COLS), lambda b, s: (jnp.maximum(blk(b, s) * halo_per_blk - 1, 0), 0)),
                pl.BlockSpec((HALO, RW_COLS),
                             lambda b, s: (jnp.minimum((blk(b, s) + 1) * halo_per_blk, n_halo - 1), 0))]

    def out_block(b, step):
        lat = jnp.maximum(step, n_ctx) - n_ctx
        return (b * n_lat + (n_lat - 1 - lat if reverse else lat), 0)

    def const_spec(a):
        return pl.BlockSpec(a.shape, lambda b, s: (0,) * a.ndim)

    prep_spec = pl.BlockSpec((blk_rows, RW_WIDTH), lambda b, s: out_block(b, jnp.minimum(s, n_chunks - 1)))
    y_spec = pl.BlockSpec((blk_rows, RW_WIDTH), lambda b, s: out_block(b, jnp.maximum(s - 2, 0)))
    out_shape = jax.ShapeDtypeStruct((batch * seq, RW_WIDTH), F32)
    n_out = 3 if with_gate else 2
    assert with_gate == (acc is None)
    acc_specs = [] if acc is None else [y_spec, prep_spec]
    assert RW_BLOCK_CHUNKS <= SUBLANES
    term_set = [pltpu.VMEM((blk_rows, RW_WIDTH), F32)] * 5 + [pltpu.VMEM((SUBLANES, RW_WIDTH), F32)]
    chain_set = [pltpu.VMEM((blk_rows, RW_WIDTH), F32)] * 4
    return pl.pallas_call(
        functools.partial(_rwkv_kernel, reverse, with_gate, n_ctx, n_chunks),
        grid=(batch, n_chunks + 2),
        in_specs=stream_specs(n_ctx, n_lat) + stream_specs(0, n_ctx) + [const_spec(a) for a in weights] + acc_specs,
        out_specs=[y_spec] + [prep_spec] * (n_out - 1),
        out_shape=[out_shape] * n_out,
        scratch_shapes=[pltpu.VMEM((RW_WIDTH // QUAD, CHUNK, QUAD), F32)] + term_set * 2 + chain_set * 2,
        compiler_params=_cparams("parallel", "arbitrary"),
        name="rwkv_bwd" if reverse else "rwkv_fwd",
    )(rw_lat, rw_lat, rw_lat, rw_ctx, rw_ctx, rw_ctx, *weights, *(acc or ()))


def _tail_kernel(d_model, alpha, ff_chunk, yna_ref, y_ref, bonus_ref, gate_ref, x_ref, moda_ref, modc_ref,
                 gng_ref, gnb_ref, wo_ref, ln1g_ref, ln1b_ref, w1_ref, w2_ref, ln2g_ref, ln2b_ref,
                 o_ref, x1_a, xm_a, acc_a, x1_b, xm_b, acc_b):
    i = pl.program_id(0)
    last = pl.num_programs(0) - 1

    def step(stages, x1_cur, xm_cur, acc_cur, xm_prev, acc_prev):
        inv_n = 1.0 / HEAD_DIM
        n_chunks = w1_ref.shape[1] // ff_chunk
        t = {}

        def lane_head_sums(z):
            lane = lax.broadcasted_iota(jnp.int32, (z.shape[0], 2 * HEAD_DIM), 1)
            first = lane < HEAD_DIM
            slabs = []
            for p in range(z.shape[1] // (2 * HEAD_DIM)):
                zs = z[:, p * 2 * HEAD_DIM:(p + 1) * 2 * HEAD_DIM]
                both = jnp.sum(zs, axis=-1, keepdims=True)
                lo = jnp.sum(jnp.where(first, zs, 0.0), axis=-1, keepdims=True)
                slabs.append(jnp.where(first, lo, both - lo))
            return jnp.concatenate(slabs, axis=1)

        def readout_stats():
            y = y_ref[...]
            mu = lane_head_sums(y) * inv_n
            yc = y - mu
            t.update(yc=yc, var=lane_head_sums(yc * yc) * inv_n)

        def out_projection():
            yn = t["yc"] * lax.rsqrt(t["var"] + GN_EPS) * gng_ref[...] + gnb_ref[...]
            y_rw = ((yn + bonus_ref[...]) * gate_ref[...]).astype(BF16)
            t["proj"] = (_dot(yna_ref[...], wo_ref[0:NA_WIDTH, :])
                         + _dot(y_rw, wo_ref[NA_WIDTH:NA_WIDTH + RW_WIDTH, :]))

        half_rows = x_ref.shape[0] // 2

        def norm_modulate(part):
            rows = slice(part * half_rows, (part + 1) * half_rows)
            mod = moda_ref[...]
            g1 = mod[:, 2 * d_model:3 * d_model]
            shift = mod[:, 3 * d_model:4 * d_model]
            scale = mod[:, 4 * d_model:5 * d_model]
            x1 = (_normalize(alpha * x_ref[rows, :] + g1 * t["proj"][rows, :], LN_EPS) * ln1g_ref[...]
                  + ln1b_ref[...])
            x1_cur[rows, :] = x1
            xm_cur[rows, :] = (_normalize(x1, LN_EPS) * (1.0 + scale) + shift).astype(BF16)

        def final_norm(part):
            rows = slice(part * half_rows, (part + 1) * half_rows)
            g2 = modc_ref[...][:, 5 * d_model:6 * d_model]
            o_ref[rows, :] = (_normalize(alpha * x1_cur[rows, :] + g2 * acc_cur[rows, :], LN_EPS) * ln2g_ref[...]
                              + ln2b_ref[...])

        others = [(min(k, n_chunks - 1), name, stage) for k, name, stage in
                  ((0, "a", readout_stats), (0, "a", out_projection),
                   (1, "c", functools.partial(final_norm, 0)), (1, "a", functools.partial(norm_modulate, 0)),
                   (2, "c", functools.partial(final_norm, 1)), (2, "a", functools.partial(norm_modulate, 1)))]
        if "b" not in stages:
            for _, name, stage in others:
                if name in stages:
                    stage()
            return
        xm = xm_prev[...]
        acc = None
        for c in range(n_chunks):
            c0 = c * ff_chunk
            hid = jnp.maximum(_dot(xm, w1_ref[:, c0:c0 + ff_chunk]), 0.0)
            part = _dot((hid * hid).astype(BF16), w2_ref[c0:c0 + ff_chunk, :])
            acc = part if acc is None else acc + part
            for after, name, stage in others:
                if after == c and name in stages:
                    stage()
        acc_prev[...] = acc

    slots = ((x1_a, xm_a, acc_a, xm_b, acc_b), (x1_b, xm_b, acc_b, xm_a, acc_a))

    @pl.when(i == 0)
    def _():
        x1_b[...] = jnp.zeros_like(x1_b)
        acc_b[...] = jnp.zeros_like(acc_b)
        step("a", *slots[0])

    inner = jnp.logical_and(i > 0, i < last)

    @pl.when(jnp.logical_and(inner, i % 2 == 0))
    def _():
        step("abc", *slots[0])

    @pl.when(jnp.logical_and(inner, i % 2 == 1))
    def _():
        step("abc", *slots[1])

    @pl.when(jnp.logical_and(i == last, i % 2 == 0))
    def _():
        step("c", *slots[0])

    @pl.when(jnp.logical_and(i == last, i % 2 == 1))
    def _():
        step("c", *slots[1])


def _tail_call(y_na, y, bonus, gate, x2d, mod3, consts, *, tm, seq, alpha):
    rows, d = x2d.shape
    tiles_per_batch = seq // tm
    n_tiles = rows // tm

    def tile_a(i):
        return jnp.minimum(i, n_tiles - 1)

    def tile_c(i):
        return jnp.maximum(i - 2, 0)

    half = pl.BlockSpec((tm, RW_WIDTH), lambda i: (tile_a(i), 0))

    def resident(a):
        return pl.BlockSpec(a.shape, lambda i: (0,) * a.ndim, pipeline_mode=pl.Buffered(1))

    return pl.pallas_call(
        functools.partial(_tail_kernel, d, alpha, FF_CHUNK),
        grid=(n_tiles + 2,),
        in_specs=[half] * 4 + [pl.BlockSpec((tm, d), lambda i: (tile_a(i), 0)),
                               pl.BlockSpec((None, 1, mod3.shape[2]), lambda i: (tile_a(i) // tiles_per_batch, 0, 0)),
                               pl.BlockSpec((None, 1, mod3.shape[2]), lambda i: (tile_c(i) // tiles_per_batch, 0, 0))]
                 + [resident(a) for a in consts],
        out_specs=pl.BlockSpec((tm, d), lambda i: (tile_c(i), 0)),
        out_shape=jax.ShapeDtypeStruct((rows, d), F32),
        scratch_shapes=[pltpu.VMEM((tm, d), F32), pltpu.VMEM((tm, d), BF16), pltpu.VMEM((tm, d), F32)] * 2,
        compiler_params=_cparams("arbitrary"),
        name="tail",
    )(y_na, y, bonus, gate, x2d, mod3, mod3, *consts)


def _rope_tables(seq):
    f = HEAD_DIM // 4
    t = np.arange(seq)
    row = (t // GRID_W).astype(np.float32)
    col = (t % GRID_W).astype(np.float32)
    inv = (ROPE_BASE ** (-np.arange(f, dtype=np.float32) / f)).astype(np.float32)
    ang_r = row[:, None] * inv[None, :]
    ang_c = col[:, None] * inv[None, :]
    zero = np.zeros_like(ang_r)
    cos = np.concatenate([np.cos(ang_r), np.cos(ang_r), np.cos(ang_c), np.cos(ang_c)], axis=1)
    sa = np.concatenate([-np.sin(ang_r), zero, -np.sin(ang_c), zero], axis=1)
    sb = np.concatenate([zero, np.sin(ang_r), zero, np.sin(ang_c)], axis=1)
    return tuple(jnp.asarray(np.concatenate([z, z], axis=1), F32) for z in (cos, sa, sb))


def _bias_rows(rpb):
    pad = GRID_W - WIN_W
    padded = jnp.pad(rpb.astype(F32) * LOG2_E, ((0, 0), (0, 0), (pad, pad + 1)))
    return padded.transpose(1, 0, 2)


def _pad_lora(w, d):
    z = jnp.zeros_like(w[d])
    return jnp.concatenate([w[0] if d == 0 else z, w[1] if d == 1 else z], axis=0)


def kernel(x, c, ctx, c_ctx, w_mod, b_mod, w_in, w_out, ln1_g, ln1_b, mlp_w1, mlp_w2, ln2_g, ln2_b, na_rpb,
           rw_mu_prev, rw_mu_next, rw_w0, rw_w2, rw_a0, rw_a2, rw_g2, rw_k_k, rw_k_a, rw_r_k, rw_gn_g, rw_gn_b):
    depth = w_mod.shape[0]
    assert depth == 1, "single-layer trunk only (the context stream is never updated)"
    batch, seq, d = x.shape
    ctx_len = ctx.shape[1]
    alpha = (2 * depth) ** 0.25
    tm = ROW_TILE
    tm_ctx = math.gcd(ctx_len, ROW_TILE)
    assert seq % (Q_ROWS * GRID_W) == 0 and seq % tm == 0 and tm_ctx % 8 == 0
    assert batch + 1 <= SUBLANES

    cc = jnp.zeros((SUBLANES, d), F32).at[:batch].set(c).at[batch].set(c_ctx)
    mod3 = _mod_call(cc, w_mod[0], b_mod[0]).reshape(SUBLANES, 1, N_MOD * d)

    w_in_b = w_in[0].astype(BF16)
    tiles_lat = seq // tm
    x2d = x.reshape(batch * seq, d)
    q, qr, kr, v, rw_lat = _inproj_call(
        x2d, mod3, _rope_tables(seq), w_in_b, tm=tm,
        mod_index=lambda i: i // tiles_lat, table_index=lambda i: i % tiles_lat, name="inproj")
    ident = tuple(jnp.full((tm_ctx, 2 * HEAD_DIM), val, F32) for val in (1.0, 0.0, 0.0))
    _, _, kc, vc, rw_ctx = _inproj_call(
        ctx.reshape(batch * ctx_len, d), mod3, ident, w_in_b, tm=tm_ctx,
        mod_index=lambda i: batch, table_index=lambda i: 0, name="inproj_ctx")

    y_na = _na_call(q, qr, kr, v, kc, vc, _bias_rows(na_rpb[0]), batch=batch, seq=seq, ctx_len=ctx_len)

    lane_head = jnp.arange(RW_WIDTH) // HEAD_DIM
    ones = (lane_head[:, None] == lane_head[None, :]).astype(BF16)
    row = lambda a: a.reshape(1, -1).astype(F32)
    def dir_weights(dirn):
        return (row(rw_mu_prev[0]), row(rw_mu_next[0]), row(rw_w0[0, dirn]),
                _pad_lora(rw_w2[0], dirn).astype(BF16), row(rw_a0[0, dirn]),
                _pad_lora(rw_a2[0], dirn).astype(BF16), rw_g2[0].astype(BF16),
                row(rw_k_k[0]), row(rw_k_a[0]), row(rw_r_k[0]), ones)

    y_f, bonus_f, gate = _rwkv_call(rw_lat, rw_ctx, dir_weights(0), reverse=False, with_gate=True,
                                    batch=batch, seq=seq, ctx_len=ctx_len)
    y_sum, bonus_sum = _rwkv_call(rw_lat, rw_ctx, dir_weights(1), reverse=True, with_gate=False,
                                  batch=batch, seq=seq, ctx_len=ctx_len, acc=(y_f, bonus_f))

    consts = (row(rw_gn_g[0]), row(rw_gn_b[0]), w_out[0].astype(BF16), row(ln1_g[0]), row(ln1_b[0]),
              mlp_w1[0].astype(BF16), mlp_w2[0].astype(BF16), row(ln2_g[0]), row(ln2_b[0]))
    out = _tail_call(y_na, y_sum, bonus_sum, gate, x2d, mod3, consts, tm=tm, seq=seq, alpha=alpha)
    return out.reshape(batch, seq, d)
```

```python
import functools
import math

import jax
import jax.numpy as jnp
import numpy as np
from jax import lax
from jax.experimental import pallas as pl
from jax.experimental.pallas import tpu as pltpu

HEAD_DIM = 64
NA_HEADS = 8
RW_HEADS = 8
NA_WIDTH = NA_HEADS * HEAD_DIM
RW_WIDTH = RW_HEADS * HEAD_DIM
GRID_W = 64
WIN_H = 8
WIN_W = 16
ROPE_BASE = 10000.0
N_DIR = 2
DECAY_LORA = 64
AAA_LORA = 64
GATE_LORA = 128
LORA_W = N_DIR * DECAY_LORA
NA_COLS = 3 * NA_WIDTH
RW_COLS = 3 * RW_WIDTH + 3 * LORA_W
N_MOD = 6
LN_EPS = 1e-6
GN_EPS = 64e-5
NEG_BIAS = -1e30

CHUNK = 64
RW_BLOCK_CHUNKS = 4
QUAD = 2 * HEAD_DIM
HALO = 8
LOG2_E = math.log2(math.e)
Q_ROWS = 8
NA_UNROLL = 4
NA_LEAD = 4
KV_GROUP = 4
WIN_GROUPS = (Q_ROWS + WIN_H) // KV_GROUP
ROW_TILE = 512
FF_CHUNK = 1024
MOD_COL_TILE = 1536
SUBLANES = 8
VMEM_LIMIT = 56 * 1024 * 1024

F32 = jnp.float32
BF16 = jnp.bfloat16


def _cparams(*sem):
    return pltpu.CompilerParams(dimension_semantics=sem, vmem_limit_bytes=VMEM_LIMIT)


def _dot(a, b):
    return jnp.dot(a, b, preferred_element_type=F32)


def _dot_nt(a, b):
    return lax.dot_general(a, b, (((1,), (1,)), ((), ())), preferred_element_type=F32)


def _dot_tn(a, b):
    return lax.dot_general(a, b, (((0,), (0,)), ((), ())), preferred_element_type=F32)


def _split(x, pieces):
    out = []
    for _ in range(pieces):
        p = x.astype(BF16)
        out.append(p)
        x = x - p.astype(F32)
    return out


def _head_sums(x, ones_bf16):
    return _dot(x.astype(BF16), ones_bf16)


def _normalize(x, eps):
    mu = jnp.mean(x, axis=-1, keepdims=True)
    xc = x - mu
    var = jnp.mean(xc * xc, axis=-1, keepdims=True)
    return xc * lax.rsqrt(var + eps)


def _mod_kernel(c_ref, w_ref, b_ref, o_ref):
    c = c_ref[...]
    s = c * jax.nn.sigmoid(c)
    o_ref[...] = _dot(s.astype(BF16), w_ref[...].astype(BF16)) + b_ref[...]


def _mod_call(cc, w_mod, b_mod):
    rows, d = cc.shape
    n = w_mod.shape[1]
    tn = MOD_COL_TILE
    assert n % tn == 0
    return pl.pallas_call(
        _mod_kernel,
        grid=(n // tn,),
        in_specs=[pl.BlockSpec((rows, d), lambda j: (0, 0)),
                  pl.BlockSpec((d, tn), lambda j: (0, j)),
                  pl.BlockSpec((1, tn), lambda j: (0, j))],
        out_specs=pl.BlockSpec((rows, tn), lambda j: (0, j)),
        out_shape=jax.ShapeDtypeStruct((rows, n), F32),
        compiler_params=_cparams("arbitrary"),
        name="mod",
    )(cc, w_mod, b_mod.reshape(1, n))


def _inproj_kernel(d_model, x_ref, mod_ref, cos_ref, sa_ref, sb_ref, w_ref, q_ref, qr_ref, kr_ref, v_ref, rw_ref):
    x = x_ref[...]
    mod = mod_ref[...]
    shift = mod[:, 0:d_model]
    scale = mod[:, d_model:2 * d_model]
    xm = (_normalize(x, LN_EPS) * (1.0 + scale) + shift).astype(BF16)

    reps = NA_WIDTH // cos_ref.shape[1]
    cos = jnp.concatenate([cos_ref[...]] * reps, axis=1)
    sa = jnp.concatenate([sa_ref[...]] * reps, axis=1)
    sb = jnp.concatenate([sb_ref[...]] * reps, axis=1)

    def rope(z):
        up = pltpu.roll(z, NA_WIDTH - HEAD_DIM // 4, 1)
        down = pltpu.roll(z, HEAD_DIM // 4, 1)
        return z * cos + up * sa + down * sb

    qk_scale = HEAD_DIM ** -0.5 * LOG2_E
    q = _dot(xm, w_ref[:, 0:NA_WIDTH]) * qk_scale
    q_ref[...] = q.astype(BF16)
    qr_ref[...] = rope(q).astype(BF16)
    k = _dot(xm, w_ref[:, NA_WIDTH:2 * NA_WIDTH])
    kr_ref[...] = rope(k).astype(BF16)
    v_ref[...] = _dot(xm, w_ref[:, 2 * NA_WIDTH:NA_COLS]).astype(BF16)
    rw_ref[...] = _dot(xm, w_ref[:, NA_COLS:NA_COLS + RW_COLS])


def _inproj_call(x2d, mod3, tables, w_in_bf16, *, tm, mod_index, table_index, name):
    rows, d = x2d.shape
    cos, sa, sb = tables
    tw = cos.shape[1]
    ncols = w_in_bf16.shape[1]
    na_spec = pl.BlockSpec((tm, NA_WIDTH), lambda i: (i, 0))
    table_spec = pl.BlockSpec((tm, tw), lambda i: (table_index(i), 0))
    na_shape = jax.ShapeDtypeStruct((rows, NA_WIDTH), BF16)
    return pl.pallas_call(
        functools.partial(_inproj_kernel, d),
        grid=(rows // tm,),
        in_specs=[pl.BlockSpec((tm, d), lambda i: (i, 0)),
                  pl.BlockSpec((None, 1, mod3.shape[2]), lambda i: (mod_index(i), 0, 0)),
                  table_spec, table_spec, table_spec,
                  pl.BlockSpec((d, ncols), lambda i: (0, 0))],
        out_specs=[na_spec, na_spec, na_spec, na_spec, pl.BlockSpec((tm, RW_COLS), lambda i: (i, 0))],
        out_shape=[na_shape, na_shape, na_shape, na_shape, jax.ShapeDtypeStruct((rows, RW_COLS), F32)],
        compiler_params=_cparams("parallel"),
        name=name,
    )(x2d, mod3, cos, sa, sb, w_in_bf16)


def _window_group(i, n_groups):
    return jnp.clip((Q_ROWS * i - WIN_H // 2) // KV_GROUP, 0, n_groups - WIN_GROUPS)


def _build_bias_table(rpb_ref, bias_ref):
    n_rows = 2 * WIN_H - 1
    lane = lax.broadcasted_iota(jnp.int32, (GRID_W, 2 * GRID_W), 1)
    qcol = lax.broadcasted_iota(jnp.int32, (GRID_W, 2 * GRID_W), 0)
    kcol = lane % GRID_W
    c0 = jnp.clip(qcol - WIN_W // 2, 0, GRID_W - WIN_W)
    inside = jnp.logical_and(kcol >= c0, kcol < c0 + WIN_W)
    first_half = lane < GRID_W
    for h in range(NA_HEADS):
        def toeplitz(ro, shift):
            row = jnp.broadcast_to(rpb_ref[ro, h:h + 1, :], (GRID_W, 2 * GRID_W))
            return pltpu.roll(row, shift, 1, stride=1, stride_axis=0)
        for ro in range(n_rows - 1):
            pair = jnp.where(first_half, toeplitz(ro, GRID_W + 1), toeplitz(ro + 1, 1))
            bias_ref[ro, h] = jnp.where(inside, pair, NEG_BIAS)


def _na_kernel(grid_rows, q_ref, qr_ref, *refs):
    kwin, vwin, kc_ref, vc_ref, rpb_ref, o_ref, bias_ref = refs
    i = pl.program_id(1)

    @pl.when(i == 0)
    def _():
        _build_bias_table(rpb_ref, bias_ref)

    n_groups = grid_rows // KV_GROUP
    win_row0 = KV_GROUP * _window_group(i, n_groups)
    win_keys = WIN_H * GRID_W
    lane = lax.broadcasted_iota(jnp.int32, (GRID_W, 2 * HEAD_DIM), 1)
    even = lane < HEAD_DIM

    pairs = range(NA_HEADS // 2)
    lanes = [slice(pair * 2 * HEAD_DIM, (pair + 1) * 2 * HEAD_DIM) for pair in pairs]

    def rows_body(jj, carry):
        units = []
        for u in range(NA_UNROLL):
            j = NA_UNROLL * jj + u
            irow = Q_ROWS * i + j
            r0 = jnp.clip(irow - WIN_H // 2, 0, grid_rows - WIN_H)
            koff = pl.multiple_of((r0 - win_row0) * GRID_W, GRID_W)
            brow = (WIN_H - 1) - (irow - r0)
            qoff = pl.multiple_of(j * GRID_W, GRID_W)
            units += [(koff, brow, qoff, pair) for pair in pairs]

        def split_heads(z):
            zero = jnp.zeros_like(z)
            return jnp.concatenate([jnp.where(even, z, zero), jnp.where(even, zero, z)], axis=0)

        def scores(unit):
            koff, brow, qoff, pair = unit
            bias = jnp.concatenate(
                [jnp.concatenate([bias_ref[brow + 2 * m, 2 * pair], bias_ref[brow + 2 * m, 2 * pair + 1]], axis=0)
                 for m in range(WIN_H // 2)], axis=1)
            s_l = _dot_nt(split_heads(qr_ref[pl.ds(qoff, GRID_W), lanes[pair]]),
                          kwin[pl.ds(koff, win_keys), lanes[pair]]) + bias
            s_c = _dot_nt(split_heads(q_ref[pl.ds(qoff, GRID_W), lanes[pair]]), kc_ref[:, lanes[pair]])
            return s_l, s_c

        def attend(unit, s_l, s_c):
            koff, brow, qoff, pair = unit
            mx = jnp.maximum(jnp.max(s_l, axis=-1, keepdims=True), jnp.max(s_c, axis=-1, keepdims=True))
            p_l = jnp.exp2(s_l - mx)
            p_c = jnp.exp2(s_c - mx)
            denom = jnp.sum(p_l, axis=-1, keepdims=True) + jnp.sum(p_c, axis=-1, keepdims=True)
            o = (_dot(p_l.astype(BF16), vwin[pl.ds(koff, win_keys), lanes[pair]])
                 + _dot(p_c.astype(BF16), vc_ref[:, lanes[pair]])) / denom
            o_ref[pl.ds(qoff, GRID_W), lanes[pair]] = jnp.where(
                even, o[0:GRID_W], o[GRID_W:2 * GRID_W]).astype(BF16)

        pending = [scores(unit) for unit in units[:NA_LEAD]]
        for n, unit in enumerate(units):
            if n + NA_LEAD < len(units):
                pending.append(scores(units[n + NA_LEAD]))
            attend(unit, *pending[n])
        return carry

    lax.fori_loop(0, Q_ROWS // NA_UNROLL, rows_body, 0)


def _na_call(q, qr, kr, v, kc, vc, rpb_rows, *, batch, seq, ctx_len):
    grid_rows = seq // GRID_W
    n_blocks = grid_rows // Q_ROWS
    n_groups = grid_rows // KV_GROUP
    blk = Q_ROWS * GRID_W
    grp = KV_GROUP * GRID_W

    win_spec = pl.BlockSpec((pl.Element(WIN_GROUPS * grp), pl.Element(NA_WIDTH)),
                            lambda b, i: ((b * n_groups + _window_group(i, n_groups)) * grp, 0))

    q_spec = pl.BlockSpec((blk, NA_WIDTH), lambda b, i: (b * n_blocks + i, 0))
    c_spec = pl.BlockSpec((ctx_len, NA_WIDTH), lambda b, i: (b, 0))
    return pl.pallas_call(
        functools.partial(_na_kernel, grid_rows),
        grid=(batch, n_blocks),
        in_specs=[q_spec, q_spec, win_spec, win_spec]
                 + [c_spec, c_spec, pl.BlockSpec(rpb_rows.shape, lambda b, i: (0, 0, 0))],
        out_specs=q_spec,
        out_shape=jax.ShapeDtypeStruct((batch * seq, NA_WIDTH), BF16),
        scratch_shapes=[pltpu.VMEM((2 * WIN_H - 2, NA_HEADS, GRID_W, 2 * GRID_W), F32)],
        compiler_params=_cparams("parallel", "arbitrary"),
        name="natten",
    )(q, qr, kr, v, kc, vc, rpb_rows)


def _block_diag(x_cat, bd_mask):
    xb = x_cat.astype(BF16)
    tiled = jnp.concatenate([xb] * (QUAD // CHUNK), axis=0)
    return jnp.where(bd_mask, tiled, jnp.zeros_like(tiled))


def _diag_blocks(x, head_of_lane):
    n = QUAD // HEAD_DIM
    out = x[(n - 1) * HEAD_DIM:n * HEAD_DIM]
    for h in range(n - 2, -1, -1):
        out = jnp.where(head_of_lane == h, x[h * HEAD_DIM:(h + 1) * HEAD_DIM], out)
    return out


def _rwkv_kernel(reverse, with_gate, n_ctx, n_blocks,
                 lat_ref, lat_prev_ref, lat_next_ref, ctx_ref, ctx_prev_ref, ctx_next_ref,
                 mup_ref, mun_ref, w0_ref, w2_ref, a0_ref, a2_ref, g2_ref,
                 kk_ref, ka_ref, rk_ref, ones_ref, *refs):
    if with_gate:
        acc_refs = (None, None)
        y_ref, bonus_ref, gate_ref = refs[0:3]
        refs = refs[3:]
    else:
        acc_refs = refs[0:2]
        y_ref, bonus_ref = refs[2:4]
        gate_ref = None
        refs = refs[4:]
    h_ref = refs[0]
    sets = (refs[1:7], refs[7:13])
    psets = (refs[13:17], refs[17:21])
    s = pl.program_id(1)

    last = n_blocks + 1

    def run(parity, stages):
        _rwkv_step(reverse, n_ctx, n_blocks, s, stages, sets[parity], sets[1 - parity], psets[parity],
                   psets[1 - parity], lat_ref, lat_prev_ref, lat_next_ref,
                   ctx_ref, ctx_prev_ref, ctx_next_ref, mup_ref, mun_ref, w0_ref, w2_ref, a0_ref, a2_ref,
                   g2_ref, kk_ref, ka_ref, rk_ref, ones_ref, acc_refs, y_ref, bonus_ref, gate_ref, h_ref)

    @pl.when(s == 0)
    def _():
        h_ref[...] = jnp.zeros_like(h_ref)
        for ref in psets[0]:
            ref[...] = jnp.zeros_like(ref)
        run(0, ("prep",))

    @pl.when(s == last)
    def _():
        run(last % 2, ("state",))

    inner = jnp.logical_and(s > 0, s < last)

    @pl.when(jnp.logical_and(inner, s % 2 == 0))
    def _():
        run(0, ("prep", "chains", "state"))

    @pl.when(jnp.logical_and(inner, s % 2 == 1))
    def _():
        run(1, ("prep", "chains", "state"))


def _rwkv_step(reverse, n_ctx, n_blocks, s, stages, wset, rset, pwset, prset, lat_ref, lat_prev_ref, lat_next_ref,
               ctx_ref, ctx_prev_ref, ctx_next_ref, mup_ref, mun_ref, w0_ref, w2_ref, a0_ref, a2_ref,
               g2_ref, kk_ref, ka_ref, rk_ref, ones_ref, acc_refs, y_ref, bonus_ref, gate_ref, h_ref):
    sp = jnp.minimum(s, n_blocks - 1)
    if reverse:
        n = jnp.where(sp < n_ctx, n_ctx - 1 - sp, n_blocks + n_ctx - 1 - sp)
    else:
        n = sp

    rows = lat_ref.shape[0]
    n_ch = rows // CHUNK
    w = RW_WIDTH
    t = {}

    is_ctx = n < n_ctx
    has_prev = jnp.logical_and(n != 0, n != n_ctx)
    has_next = jnp.logical_and(n != n_ctx - 1, n != n_blocks - 1)
    row = lax.broadcasted_iota(jnp.int32, (rows, 1), 0)

    def shifted(c0, c1):
        p = jnp.where(is_ctx, ctx_ref[:, c0:c1], lat_ref[:, c0:c1])
        prow = jnp.where(is_ctx, ctx_prev_ref[HALO - 1:HALO, c0:c1], lat_prev_ref[HALO - 1:HALO, c0:c1])
        nrow = jnp.where(is_ctx, ctx_next_ref[0:1, c0:c1], lat_next_ref[0:1, c0:c1])
        prow = jnp.where(has_prev, prow, 0.0)
        nrow = jnp.where(has_next, nrow, 0.0)
        prev = jnp.where(row == 0, prow, pltpu.roll(p, 1, 0))
        nxt = jnp.where(row == rows - 1, nrow, pltpu.roll(p, rows - 1, 0))
        return p + mup_ref[:, c0:c1] * (prev - p) + mun_ref[:, c0:c1] * (nxt - p)

    def prep_lora():
        lora = shifted(3 * w, 3 * w + 3 * LORA_W)
        pw = lora[:, 0:LORA_W]
        pa = lora[:, LORA_W:2 * LORA_W]
        t["log_decay"] = -math.exp(-0.5) * jax.nn.sigmoid(
            w0_ref[...] + _dot(jnp.tanh(pw).astype(BF16), w2_ref[...]))
        t["a"] = jax.nn.sigmoid(a0_ref[...] + _dot(pa.astype(BF16), a2_ref[...]))
        if gate_ref is not None:
            gate_ref[...] = _dot(jax.nn.sigmoid(lora[:, 2 * LORA_W:3 * LORA_W]).astype(BF16), g2_ref[...])

    def prep_keys():
        k = shifted(w, 2 * w)
        ones = ones_ref[0:RW_WIDTH, :]
        kk_raw = k * kk_ref[...]
        kk = kk_raw * lax.rsqrt(jnp.maximum(_head_sums(kk_raw * kk_raw, ones), 1e-24))
        t.update(kk=kk, kd=k * (1.0 + (t["a"] - 1.0) * ka_ref[...]), bb=t["a"] * kk)

    def prep_bonus():
        r = shifted(0, w)
        v = shifted(2 * w, 3 * w)
        bonus = _head_sums(r * t["kd"] * rk_ref[...], ones_ref[0:RW_WIDTH, :]) * v
        bonus_ref[...] = bonus if acc_refs[1] is None else bonus + acc_refs[1][...]
        t.update(r=r, v=v)

    def finish_prep():
        r, v, kk, kd, bb, log_decay = (t[name] for name in ("r", "v", "kk", "kd", "bb", "log_decay"))
        tri = ones_ref[RW_WIDTH:RW_WIDTH + rows, 0:rows]
        lw_hi, lw_lo = _split(log_decay, 2)
        cum = _dot(tri, lw_hi) + _dot(tri, lw_lo)
        e_neg = jnp.exp(-cum)
        last = 0 if reverse else CHUNK - 1
        w_v, w_kkt, w_rt, w_kh, w_bh, w_et = wset
        w_v[...] = v
        w_kkt[...] = kk * jnp.exp(cum - log_decay)
        w_rt[...] = r * jnp.exp(cum)
        w_kh[...] = kd * e_neg
        w_bh[...] = bb * e_neg
        for c in range(n_ch):
            w_et[c:c + 1, :] = jnp.exp(cum[c * CHUNK + last:c * CHUNK + last + 1, :])

    prep_at = {"start": prep_lora, "gram": prep_keys, "akv": prep_bonus, "dbl1": finish_prep}

    def issue_prep(point):
        if point in prep_at and "prep" in stages:
            prep_at[point]()

    issue_prep("start")
    r_v, r_kkt, r_rt, r_kh, r_bh, r_et = rset
    trow = lax.broadcasted_iota(jnp.int32, (CHUNK, QUAD), 0)
    tcol = lax.broadcasted_iota(jnp.int32, (CHUNK, QUAD), 1) % CHUNK
    strict = (trow < tcol) if reverse else (trow > tcol)
    incl = (trow <= tcol) if reverse else (trow >= tcol)
    on_diag = trow == tcol
    eye_cat = jnp.where(on_diag, 1.0, 0.0)
    brow = lax.broadcasted_iota(jnp.int32, (QUAD, QUAD), 0)
    bcol = lax.broadcasted_iota(jnp.int32, (QUAD, QUAD), 1)
    bd_mask = (brow // HEAD_DIM) == (bcol // HEAD_DIM)
    head_of_lane = lax.broadcasted_iota(jnp.int32, (CHUNK, QUAD), 1) // HEAD_DIM
    n_doublings = int(math.log2(CHUNK)) - 1
    heads = range(QUAD // HEAD_DIM)
    quads = range(RW_WIDTH // QUAD)
    order = list(range(n_ch - 1, -1, -1) if reverse else range(n_ch))

    chains = [(c, qd) for c in order for qd in quads]

    def rsl(c):
        return slice(c * CHUNK, (c + 1) * CHUNK)

    def lsl(qd):
        return slice(qd * QUAD, (qd + 1) * QUAD)

    r_y0, r_qt, r_g, r_h0 = prset
    h = {qd: h_ref[qd] for qd in quads}
    issued = [0]

    def issue_state(upto):
        while "state" in stages and issued[0] < min(upto, len(order)):
            c = order[issued[0]]
            issued[0] += 1
            for qd in quads:
                lhs = jnp.concatenate([r_g[rsl(c), lsl(qd)], r_qt[rsl(c), lsl(qd)]], axis=0).astype(BF16)
                o2 = _dot(lhs, _block_diag(h[qd], bd_mask))
                h[qd] = o2[0:CHUNK] + r_h0[rsl(c), lsl(qd)]
                y = r_y0[rsl(c), lsl(qd)] + o2[CHUNK:2 * CHUNK]
                y_ref[rsl(c), lsl(qd)] = y if acc_refs[0] is None else y + acc_refs[0][rsl(c), lsl(qd)]
            if issued[0] == len(order):
                for qd in quads:
                    h_ref[qd] = h[qd]

    issue_state(1)
    if "chains" not in stages:
        for point in ["gram", "akv", "m1"] + ["dbl%d" % step for step in range(n_doublings)]:
            issue_prep(point)
        issue_state(len(order))
        return
    v_q = {ch: r_v[rsl(ch[0]), lsl(ch[1])] for ch in chains}
    kkt_q = {ch: r_kkt[rsl(ch[0]), lsl(ch[1])] for ch in chains}
    rt_q = {ch: r_rt[rsl(ch[0]), lsl(ch[1])] for ch in chains}

    a_k, a_b, b_k, b_b = {}, {}, {}, {}
    for ch in chains:
        kh = r_kh[rsl(ch[0]), lsl(ch[1])].astype(BF16)
        bh = r_bh[rsl(ch[0]), lsl(ch[1])].astype(BF16)
        zero = jnp.zeros_like(kh)
        rhs = jnp.concatenate([jnp.where(head_of_lane == h, kh, zero) for h in heads]
                              + [jnp.where(head_of_lane == h, bh, zero) for h in heads], axis=0)
        lhs = jnp.concatenate([kkt_q[ch], rt_q[ch]], axis=0).astype(BF16)
        gram = _dot_nt(lhs, rhs)
        a_k[ch] = jnp.where(strict, gram[0:CHUNK, 0:QUAD], 0.0)
        a_b[ch] = jnp.where(strict, gram[0:CHUNK, QUAD:2 * QUAD], 0.0)
        b_k[ch] = jnp.where(incl, gram[CHUNK:2 * CHUNK, 0:QUAD], 0.0)
        b_b[ch] = jnp.where(incl, gram[CHUNK:2 * CHUNK, QUAD:2 * QUAD], 0.0).astype(BF16)
    issue_prep("gram")
    issue_state(2)

    akv, bkv = {}, {}
    for ch in chains:
        both = _dot(jnp.concatenate([a_k[ch], b_k[ch]], axis=0).astype(BF16), _block_diag(v_q[ch], bd_mask))
        akv[ch] = both[0:CHUNK]
        bkv[ch] = both[CHUNK:2 * CHUNK]
    issue_prep("akv")
    issue_state(3)

    m = {ch: -a_b[ch] for ch in chains}
    t_inv = {ch: eye_cat + m[ch] for ch in chains}
    for ch in chains:
        m[ch] = _dot(m[ch].astype(BF16), _block_diag(m[ch], bd_mask))
    issue_prep("m1")
    issue_state(4)
    for step in range(n_doublings):
        final = step == n_doublings - 1
        issue_prep("dbl%d" % step)
        for ch in chains:
            m_bd = _block_diag(m[ch], bd_mask)
            if final:
                t_inv[ch] = t_inv[ch] + _dot(t_inv[ch].astype(BF16), m_bd)
            else:
                both = _dot(jnp.concatenate([m[ch], t_inv[ch]], axis=0).astype(BF16), m_bd)
                m[ch] = both[0:CHUNK]
                t_inv[ch] = t_inv[ch] + both[CHUNK:2 * CHUNK]

    p1, tkk = {}, {}
    for ch in chains:
        t_b = t_inv[ch].astype(BF16)
        p1[ch] = _dot(t_b, _block_diag(akv[ch], bd_mask))
        tkk[ch] = _dot(t_b, _block_diag(kkt_q[ch], bd_mask))

    w_y0, w_qt, w_g, w_h0 = pwset
    for ch in chains:
        c, qd = ch
        w_y0[rsl(c), lsl(qd)] = bkv[ch] - _dot(b_b[ch], _block_diag(p1[ch], bd_mask))
        w_qt[rsl(c), lsl(qd)] = rt_q[ch] - _dot(b_b[ch], _block_diag(tkk[ch], bd_mask))
        e_q = r_et[c:c + 1, lsl(qd)]
        lhs_t = jnp.concatenate([r_kh[rsl(c), lsl(qd)] * e_q, r_bh[rsl(c), lsl(qd)] * e_q], axis=0).astype(BF16)
        rhs_t = jnp.concatenate([jnp.concatenate([v_q[ch], jnp.zeros_like(v_q[ch])], axis=1),
                                 jnp.concatenate([-p1[ch], -tkk[ch]], axis=1)], axis=0).astype(BF16)
        hg = _dot_tn(lhs_t, rhs_t)
        w_h0[rsl(c), lsl(qd)] = _diag_blocks(hg[:, 0:QUAD], head_of_lane)
        w_g[rsl(c), lsl(qd)] = _diag_blocks(hg[:, QUAD:2 * QUAD], head_of_lane) + jnp.where(on_diag, e_q, 0.0)
    issue_state(len(order))


def _rwkv_call(rw_lat, rw_ctx, weights, *, reverse, with_gate, batch, seq, ctx_len, acc=None):
    assert CHUNK == HEAD_DIM
    blk_rows = RW_BLOCK_CHUNKS * CHUNK
    assert ctx_len % blk_rows == 0 and seq % blk_rows == 0
    n_ctx = ctx_len // blk_rows
    n_lat = seq // blk_rows
    n_chunks = n_ctx + n_lat
    halo_per_blk = blk_rows // HALO

    def block_of(s):
        if reverse:
            return jnp.where(s < n_ctx, n_ctx - 1 - s, n_chunks + n_ctx - 1 - s)
        return s

    def stream_specs(first, count):
        n_halo = batch * count * halo_per_blk

        def blk(b, s):
            return b * count + jnp.clip(block_of(jnp.minimum(s, n_chunks - 1)) - first, 0, count - 1)

        return [pl.BlockSpec((blk_rows, RW_COLS), lambda b, s: (blk(b, s), 0)),
                pl.BlockSpec((HALO, RW_COLS), lambda b, s: (jnp.maximum(blk(b, s) * halo_per_blk - 1, 0), 0)),
                pl.BlockSpec((HALO, RW_COLS),
                             lambda b, s: (jnp.minimum((blk(b, s) + 1) * halo_per_blk, n_halo - 1), 0))]

    def out_block(b, step):
        lat = jnp.maximum(step, n_ctx) - n_ctx
        return (b * n_lat + (n_lat - 1 - lat if reverse else lat), 0)

    def const_spec(a):
        return pl.BlockSpec(a.shape, lambda b, s: (0,) * a.ndim)

    prep_spec = pl.BlockSpec((blk_rows, RW_WIDTH), lambda b, s: out_block(b, jnp.minimum(s, n_chunks - 1)))
    y_spec = pl.BlockSpec((blk_rows, RW_WIDTH), lambda b, s: out_block(b, jnp.maximum(s - 2, 0)))
    out_shape = jax.ShapeDtypeStruct((batch * seq, RW_WIDTH), F32)
    n_out = 3 if with_gate else 2
    assert with_gate == (acc is None)
    acc_specs = [] if acc is None else [y_spec, prep_spec]
    assert RW_BLOCK_CHUNKS <= SUBLANES
    term_set = [pltpu.VMEM((blk_rows, RW_WIDTH), F32)] * 5 + [pltpu.VMEM((SUBLANES, RW_WIDTH), F32)]
    chain_set = [pltpu.VMEM((blk_rows, RW_WIDTH), F32)] * 4
    return pl.pallas_call(
        functools.partial(_rwkv_kernel, reverse, with_gate, n_ctx, n_chunks),
        grid=(batch, n_chunks + 2),
        in_specs=stream_specs(n_ctx, n_lat) + stream_specs(0, n_ctx) + [const_spec(a) for a in weights] + acc_specs,
        out_specs=[y_spec] + [prep_spec] * (n_out - 1),
        out_shape=[out_shape] * n_out,
        scratch_shapes=[pltpu.VMEM((RW_WIDTH // QUAD, CHUNK, QUAD), F32)] + term_set * 2 + chain_set * 2,
        compiler_params=_cparams("parallel", "arbitrary"),
        name="rwkv_bwd" if reverse else "rwkv_fwd",
    )(rw_lat, rw_lat, rw_lat, rw_ctx, rw_ctx, rw_ctx, *weights, *(acc or ()))


def _tail_kernel(d_model, alpha, ff_chunk, yna_ref, y_ref, bonus_ref, gate_ref, x_ref, moda_ref, modc_ref,
                 gng_ref, gnb_ref, wo_ref, ln1g_ref, ln1b_ref, w1_ref, w2_ref, ln2g_ref, ln2b_ref,
                 o_ref, x1_a, xm_a, acc_a, x1_b, xm_b, acc_b):
    i = pl.program_id(0)
    last = pl.num_programs(0) - 1

    def step(stages, x1_cur, xm_cur, acc_cur, xm_prev, acc_prev):
        inv_n = 1.0 / HEAD_DIM
        n_chunks = w1_ref.shape[1] // ff_chunk
        t = {}

        def lane_head_sums(z):
            lane = lax.broadcasted_iota(jnp.int32, (z.shape[0], 2 * HEAD_DIM), 1)
            first = lane < HEAD_DIM
            slabs = []
            for p in range(z.shape[1] // (2 * HEAD_DIM)):
                zs = z[:, p * 2 * HEAD_DIM:(p + 1) * 2 * HEAD_DIM]
                both = jnp.sum(zs, axis=-1, keepdims=True)
                lo = jnp.sum(jnp.where(first, zs, 0.0), axis=-1, keepdims=True)
                slabs.append(jnp.where(first, lo, both - lo))
            return jnp.concatenate(slabs, axis=1)

        def readout_stats():
            y = y_ref[...]
            mu = lane_head_sums(y) * inv_n
            yc = y - mu
            t.update(yc=yc, var=lane_head_sums(yc * yc) * inv_n)

        def out_projection():
            yn = t["yc"] * lax.rsqrt(t["var"] + GN_EPS) * gng_ref[...] + gnb_ref[...]
            y_rw = ((yn + bonus_ref[...]) * gate_ref[...]).astype(BF16)
            t["proj"] = (_dot(yna_ref[...], wo_ref[0:NA_WIDTH, :])
                         + _dot(y_rw, wo_ref[NA_WIDTH:NA_WIDTH + RW_WIDTH, :]))

        half_rows = x_ref.shape[0] // 2

        def norm_modulate(part):
            rows = slice(part * half_rows, (part + 1) * half_rows)
            mod = moda_ref[...]
            g1 = mod[:, 2 * d_model:3 * d_model]
            shift = mod[:, 3 * d_model:4 * d_model]
            scale = mod[:, 4 * d_model:5 * d_model]
            x1 = (_normalize(alpha * x_ref[rows, :] + g1 * t["proj"][rows, :], LN_EPS) * ln1g_ref[...]
                  + ln1b_ref[...])
            x1_cur[rows, :] = x1
            xm_cur[rows, :] = (_normalize(x1, LN_EPS) * (1.0 + scale) + shift).astype(BF16)

        def final_norm(part):
            rows = slice(part * half_rows, (part + 1) * half_rows)
            g2 = modc_ref[...][:, 5 * d_model:6 * d_model]
            o_ref[rows, :] = (_normalize(alpha * x1_cur[rows, :] + g2 * acc_cur[rows, :], LN_EPS) * ln2g_ref[...]
                              + ln2b_ref[...])

        others = [(min(k, n_chunks - 1), name, stage) for k, name, stage in
                  ((0, "a", readout_stats), (0, "a", out_projection),
                   (1, "c", functools.partial(final_norm, 0)), (1, "a", functools.partial(norm_modulate, 0)),
                   (2, "c", functools.partial(final_norm, 1)), (2, "a", functools.partial(norm_modulate, 1)))]
        if "b" not in stages:
            for _, name, stage in others:
                if name in stages:
                    stage()
            return
        xm = xm_prev[...]
        acc = None
        for c in range(n_chunks):
            c0 = c * ff_chunk
            hid = jnp.maximum(_dot(xm, w1_ref[:, c0:c0 + ff_chunk]), 0.0)
            part = _dot((hid * hid).astype(BF16), w2_ref[c0:c0 + ff_chunk, :])
            acc = part if acc is None else acc + part
            for after, name, stage in others:
                if after == c and name in stages:
                    stage()
        acc_prev[...] = acc

    slots = ((x1_a, xm_a, acc_a, xm_b, acc_b), (x1_b, xm_b, acc_b, xm_a, acc_a))

    @pl.when(i == 0)
    def _():
        x1_b[...] = jnp.zeros_like(x1_b)
        acc_b[...] = jnp.zeros_like(acc_b)
        step("a", *slots[0])

    inner = jnp.logical_and(i > 0, i < last)

    @pl.when(jnp.logical_and(inner, i % 2 == 0))
    def _():
        step("abc", *slots[0])

    @pl.when(jnp.logical_and(inner, i % 2 == 1))
    def _():
        step("abc", *slots[1])

    @pl.when(jnp.logical_and(i == last, i % 2 == 0))
    def _():
        step("c", *slots[0])

    @pl.when(jnp.logical_and(i == last, i % 2 == 1))
    def _():
        step("c", *slots[1])


def _tail_call(y_na, y, bonus, gate, x2d, mod3, consts, *, tm, seq, alpha):
    rows, d = x2d.shape
    tiles_per_batch = seq // tm
    n_tiles = rows // tm

    def tile_a(i):
        return jnp.minimum(i, n_tiles - 1)

    def tile_c(i):
        return jnp.maximum(i - 2, 0)

    half = pl.BlockSpec((tm, RW_WIDTH), lambda i: (tile_a(i), 0))

    def resident(a):
        return pl.BlockSpec(a.shape, lambda i: (0,) * a.ndim, pipeline_mode=pl.Buffered(1))

    return pl.pallas_call(
        functools.partial(_tail_kernel, d, alpha, FF_CHUNK),
        grid=(n_tiles + 2,),
        in_specs=[half] * 4 + [pl.BlockSpec((tm, d), lambda i: (tile_a(i), 0)),
                               pl.BlockSpec((None, 1, mod3.shape[2]), lambda i: (tile_a(i) // tiles_per_batch, 0, 0)),
                               pl.BlockSpec((None, 1, mod3.shape[2]), lambda i: (tile_c(i) // tiles_per_batch, 0, 0))]
                 + [resident(a) for a in consts],
        out_specs=pl.BlockSpec((tm, d), lambda i: (tile_c(i), 0)),
        out_shape=jax.ShapeDtypeStruct((rows, d), F32),
        scratch_shapes=[pltpu.VMEM((tm, d), F32), pltpu.VMEM((tm, d), BF16), pltpu.VMEM((tm, d), F32)] * 2,
        compiler_params=_cparams("arbitrary"),
        name="tail",
    )(y_na, y, bonus, gate, x2d, mod3, mod3, *consts)


def _rope_tables(seq):
    f = HEAD_DIM // 4
    t = np.arange(seq)
    row = (t // GRID_W).astype(np.float32)
    col = (t % GRID_W).astype(np.float32)
    inv = (ROPE_BASE ** (-np.arange(f, dtype=np.float32) / f)).astype(np.float32)
    ang_r = row[:, None] * inv[None, :]
    ang_c = col[:, None] * inv[None, :]
    zero = np.zeros_like(ang_r)
    cos = np.concatenate([np.cos(ang_r), np.cos(ang_r), np.cos(ang_c), np.cos(ang_c)], axis=1)
    sa = np.concatenate([-np.sin(ang_r), zero, -np.sin(ang_c), zero], axis=1)
    sb = np.concatenate([zero, np.sin(ang_r), zero, np.sin(ang_c)], axis=1)
    return tuple(jnp.asarray(np.concatenate([z, z], axis=1), F32) for z in (cos, sa, sb))


def _bias_rows(rpb):
    pad = GRID_W - WIN_W
    padded = jnp.pad(rpb.astype(F32) * LOG2_E, ((0, 0), (0, 0), (pad, pad + 1)))
    return padded.transpose(1, 0, 2)


def _scan_consts(ones, reverse):
    rows = RW_BLOCK_CHUNKS * CHUNK
    t = np.arange(rows)
    same_chunk = (t[:, None] // CHUNK) == (t[None, :] // CHUNK)
    tri = same_chunk & ((t[None, :] >= t[:, None]) if reverse else (t[None, :] <= t[:, None]))
    tri = np.pad(tri.astype(np.float32), ((0, 0), (0, RW_WIDTH - rows)))
    return jnp.concatenate([ones, jnp.asarray(tri, BF16)], axis=0)


def _pad_lora(w, d):
    z = jnp.zeros_like(w[d])
    return jnp.concatenate([w[0] if d == 0 else z, w[1] if d == 1 else z], axis=0)


def kernel(x, c, ctx, c_ctx, w_mod, b_mod, w_in, w_out, ln1_g, ln1_b, mlp_w1, mlp_w2, ln2_g, ln2_b, na_rpb,
           rw_mu_prev, rw_mu_next, rw_w0, rw_w2, rw_a0, rw_a2, rw_g2, rw_k_k, rw_k_a, rw_r_k, rw_gn_g, rw_gn_b):
    depth = w_mod.shape[0]
    assert depth == 1, "single-layer trunk only (the context stream is never updated)"
    batch, seq, d = x.shape
    ctx_len = ctx.shape[1]
    alpha = (2 * depth) ** 0.25
    tm = ROW_TILE
    tm_ctx = math.gcd(ctx_len, ROW_TILE)
    assert seq % (Q_ROWS * GRID_W) == 0 and seq % tm == 0 and tm_ctx % 8 == 0
    assert batch + 1 <= SUBLANES

    cc = jnp.zeros((SUBLANES, d), F32).at[:batch].set(c).at[batch].set(c_ctx)
    mod3 = _mod_call(cc, w_mod[0], b_mod[0]).reshape(SUBLANES, 1, N_MOD * d)

    w_in_b = w_in[0].astype(BF16)
    tiles_lat = seq // tm
    x2d = x.reshape(batch * seq, d)
    q, qr, kr, v, rw_lat = _inproj_call(
        x2d, mod3, _rope_tables(seq), w_in_b, tm=tm,
        mod_index=lambda i: i // tiles_lat, table_index=lambda i: i % tiles_lat, name="inproj")
    ident = tuple(jnp.full((tm_ctx, 2 * HEAD_DIM), val, F32) for val in (1.0, 0.0, 0.0))
    _, _, kc, vc, rw_ctx = _inproj_call(
        ctx.reshape(batch * ctx_len, d), mod3, ident, w_in_b, tm=tm_ctx,
        mod_index=lambda i: batch, table_index=lambda i: 0, name="inproj_ctx")

    y_na = _na_call(q, qr, kr, v, kc, vc, _bias_rows(na_rpb[0]), batch=batch, seq=seq, ctx_len=ctx_len)

    lane_head = jnp.arange(RW_WIDTH) // HEAD_DIM
    ones = (lane_head[:, None] == lane_head[None, :]).astype(BF16)
    row = lambda a: a.reshape(1, -1).astype(F32)
    def dir_weights(dirn):
        return (row(rw_mu_prev[0]), row(rw_mu_next[0]), row(rw_w0[0, dirn]),
                _pad_lora(rw_w2[0], dirn).astype(BF16), row(rw_a0[0, dirn]),
                _pad_lora(rw_a2[0], dirn).astype(BF16), rw_g2[0].astype(BF16),
                row(rw_k_k[0]), row(rw_k_a[0]), row(rw_r_k[0]), _scan_consts(ones, dirn == 1))

    y_f, bonus_f, gate = _rwkv_call(rw_lat, rw_ctx, dir_weights(0), reverse=False, with_gate=True,
                                    batch=batch, seq=seq, ctx_len=ctx_len)
    y_sum, bonus_sum = _rwkv_call(rw_lat, rw_ctx, dir_weights(1), reverse=True, with_gate=False,
                                  batch=batch, seq=seq, ctx_len=ctx_len, acc=(y_f, bonus_f))

    consts = (row(rw_gn_g[0]), row(rw_gn_b[0]), w_out[0].astype(BF16), row(ln1_g[0]), row(ln1_b[0]),
              mlp_w1[0].astype(BF16), mlp_w2[0].astype(BF16), row(ln2_g[0]), row(ln2_b[0]))
    out = _tail_call(y_na, y_sum, bonus_sum, gate, x2d, mod3, consts, tm=tm, seq=seq, alpha=alpha)
    return out.reshape(batch, seq, d)
```

```python
import functools
import math

import jax
import jax.numpy as jnp
import numpy as np
from jax import lax
from jax.experimental import pallas as pl
from jax.experimental.pallas import tpu as pltpu

HEAD_DIM = 64
NA_HEADS = 8
RW_HEADS = 8
NA_WIDTH = NA_HEADS * HEAD_DIM
RW_WIDTH = RW_HEADS * HEAD_DIM
GRID_W = 64
WIN_H = 8
WIN_W = 16
ROPE_BASE = 10000.0
N_DIR = 2
DECAY_LORA = 64
AAA_LORA = 64
GATE_LORA = 128
LORA_W = N_DIR * DECAY_LORA
NA_COLS = 3 * NA_WIDTH
RW_COLS = 3 * RW_WIDTH + 3 * LORA_W
N_MOD = 6
LN_EPS = 1e-6
GN_EPS = 64e-5
NEG_BIAS = -1e30

CHUNK = 64
RW_BLOCK_CHUNKS = 4
QUAD = 2 * HEAD_DIM
HALO = 8
LOG2_E = math.log2(math.e)
Q_ROWS = 8
NA_UNROLL = 4
NA_LEAD = 4
KV_GROUP = 4
WIN_GROUPS = (Q_ROWS + WIN_H) // KV_GROUP
ROW_TILE = 512
FF_CHUNK = 1024
MOD_COL_TILE = 1536
SUBLANES = 8
VMEM_LIMIT = 56 * 1024 * 1024

F32 = jnp.float32
BF16 = jnp.bfloat16


def _cparams(*sem):
    return pltpu.CompilerParams(dimension_semantics=sem, vmem_limit_bytes=VMEM_LIMIT)


def _dot(a, b):
    return jnp.dot(a, b, preferred_element_type=F32)


def _dot_nt(a, b):
    return lax.dot_general(a, b, (((1,), (1,)), ((), ())), preferred_element_type=F32)


def _dot_tn(a, b):
    return lax.dot_general(a, b, (((0,), (0,)), ((), ())), preferred_element_type=F32)


def _split(x, pieces):
    out = []
    for _ in range(pieces):
        p = x.astype(BF16)
        out.append(p)
        x = x - p.astype(F32)
    return out


def _head_sums(x, ones_bf16):
    return _dot(x.astype(BF16), ones_bf16)


def _normalize(x, eps):
    mu = jnp.mean(x, axis=-1, keepdims=True)
    xc = x - mu
    var = jnp.mean(xc * xc, axis=-1, keepdims=True)
    return xc * lax.rsqrt(var + eps)


def _mod_kernel(c_ref, w_ref, b_ref, o_ref):
    c = c_ref[...]
    s = c * jax.nn.sigmoid(c)
    o_ref[...] = _dot(s.astype(BF16), w_ref[...].astype(BF16)) + b_ref[...]


def _mod_call(cc, w_mod, b_mod):
    rows, d = cc.shape
    n = w_mod.shape[1]
    tn = MOD_COL_TILE
    assert n % tn == 0
    return pl.pallas_call(
        _mod_kernel,
        grid=(n // tn,),
        in_specs=[pl.BlockSpec((rows, d), lambda j: (0, 0)),
                  pl.BlockSpec((d, tn), lambda j: (0, j)),
                  pl.BlockSpec((1, tn), lambda j: (0, j))],
        out_specs=pl.BlockSpec((rows, tn), lambda j: (0, j)),
        out_shape=jax.ShapeDtypeStruct((rows, n), F32),
        compiler_params=_cparams("arbitrary"),
        name="mod",
    )(cc, w_mod, b_mod.reshape(1, n))


def _inproj_kernel(d_model, x_ref, mod_ref, cos_ref, sa_ref, sb_ref, w_ref, q_ref, qr_ref, kr_ref, v_ref, rw_ref):
    x = x_ref[...]
    mod = mod_ref[...]
    shift = mod[:, 0:d_model]
    scale = mod[:, d_model:2 * d_model]
    xm = (_normalize(x, LN_EPS) * (1.0 + scale) + shift).astype(BF16)

    reps = NA_WIDTH // cos_ref.shape[1]
    cos = jnp.concatenate([cos_ref[...]] * reps, axis=1)
    sa = jnp.concatenate([sa_ref[...]] * reps, axis=1)
    sb = jnp.concatenate([sb_ref[...]] * reps, axis=1)

    def rope(z):
        up = pltpu.roll(z, NA_WIDTH - HEAD_DIM // 4, 1)
        down = pltpu.roll(z, HEAD_DIM // 4, 1)
        return z * cos + up * sa + down * sb

    qk_scale = HEAD_DIM ** -0.5 * LOG2_E
    q = _dot(xm, w_ref[:, 0:NA_WIDTH]) * qk_scale
    q_ref[...] = q.astype(BF16)
    qr_ref[...] = rope(q).astype(BF16)
    k = _dot(xm, w_ref[:, NA_WIDTH:2 * NA_WIDTH])
    kr_ref[...] = rope(k).astype(BF16)
    v_ref[...] = _dot(xm, w_ref[:, 2 * NA_WIDTH:NA_COLS]).astype(BF16)
    rw_ref[...] = _dot(xm, w_ref[:, NA_COLS:NA_COLS + RW_COLS])


def _inproj_call(x2d, mod3, tables, w_in_bf16, *, tm, mod_index, table_index, name):
    rows, d = x2d.shape
    cos, sa, sb = tables
    tw = cos.shape[1]
    ncols = w_in_bf16.shape[1]
    na_spec = pl.BlockSpec((tm, NA_WIDTH), lambda i: (i, 0))
    table_spec = pl.BlockSpec((tm, tw), lambda i: (table_index(i), 0))
    na_shape = jax.ShapeDtypeStruct((rows, NA_WIDTH), BF16)
    return pl.pallas_call(
        functools.partial(_inproj_kernel, d),
        grid=(rows // tm,),
        in_specs=[pl.BlockSpec((tm, d), lambda i: (i, 0)),
                  pl.BlockSpec((None, 1, mod3.shape[2]), lambda i: (mod_index(i), 0, 0)),
                  table_spec, table_spec, table_spec,
                  pl.BlockSpec((d, ncols), lambda i: (0, 0), pipeline_mode=pl.Buffered(1))],
        out_specs=[na_spec, na_spec, na_spec, na_spec, pl.BlockSpec((tm, RW_COLS), lambda i: (i, 0))],
        out_shape=[na_shape, na_shape, na_shape, na_shape, jax.ShapeDtypeStruct((rows, RW_COLS), F32)],
        compiler_params=_cparams("parallel"),
        name=name,
    )(x2d, mod3, cos, sa, sb, w_in_bf16)


def _window_group(i, n_groups):
    return jnp.clip((Q_ROWS * i - WIN_H // 2) // KV_GROUP, 0, n_groups - WIN_GROUPS)


def _build_bias_table(rpb_ref, bias_ref):
    n_rows = 2 * WIN_H - 1
    lane = lax.broadcasted_iota(jnp.int32, (GRID_W, 2 * GRID_W), 1)
    qcol = lax.broadcasted_iota(jnp.int32, (GRID_W, 2 * GRID_W), 0)
    kcol = lane % GRID_W
    c0 = jnp.clip(qcol - WIN_W // 2, 0, GRID_W - WIN_W)
    inside = jnp.logical_and(kcol >= c0, kcol < c0 + WIN_W)
    first_half = lane < GRID_W
    for h in range(NA_HEADS):
        def toeplitz(ro, shift):
            row = jnp.broadcast_to(rpb_ref[ro, h:h + 1, :], (GRID_W, 2 * GRID_W))
            return pltpu.roll(row, shift, 1, stride=1, stride_axis=0)
        for ro in range(n_rows - 1):
            pair = jnp.where(first_half, toeplitz(ro, GRID_W + 1), toeplitz(ro + 1, 1))
            bias_ref[ro, h] = jnp.where(inside, pair, NEG_BIAS)


def _na_kernel(grid_rows, q_ref, qr_ref, *refs):
    kwin, vwin, kc_ref, vc_ref, rpb_ref, o_ref, bias_ref = refs
    i = pl.program_id(1)

    @pl.when(i == 0)
    def _():
        _build_bias_table(rpb_ref, bias_ref)

    n_groups = grid_rows // KV_GROUP
    win_row0 = KV_GROUP * _window_group(i, n_groups)
    win_keys = WIN_H * GRID_W
    lane = lax.broadcasted_iota(jnp.int32, (GRID_W, 2 * HEAD_DIM), 1)
    even = lane < HEAD_DIM

    pairs = range(NA_HEADS // 2)
    lanes = [slice(pair * 2 * HEAD_DIM, (pair + 1) * 2 * HEAD_DIM) for pair in pairs]

    def rows_body(jj, carry):
        units = []
        for u in range(NA_UNROLL):
            j = NA_UNROLL * jj + u
            irow = Q_ROWS * i + j
            r0 = jnp.clip(irow - WIN_H // 2, 0, grid_rows - WIN_H)
            koff = pl.multiple_of((r0 - win_row0) * GRID_W, GRID_W)
            brow = (WIN_H - 1) - (irow - r0)
            qoff = pl.multiple_of(j * GRID_W, GRID_W)
            units += [(koff, brow, qoff, pair) for pair in pairs]

        def split_heads(z):
            zero = jnp.zeros_like(z)
            return jnp.concatenate([jnp.where(even, z, zero), jnp.where(even, zero, z)], axis=0)

        def scores(unit):
            koff, brow, qoff, pair = unit
            bias = jnp.concatenate(
                [jnp.concatenate([bias_ref[brow + 2 * m, 2 * pair], bias_ref[brow + 2 * m, 2 * pair + 1]], axis=0)
                 for m in range(WIN_H // 2)], axis=1)
            s_l = _dot_nt(split_heads(qr_ref[pl.ds(qoff, GRID_W), lanes[pair]]),
                          kwin[pl.ds(koff, win_keys), lanes[pair]]) + bias
            s_c = _dot_nt(split_heads(q_ref[pl.ds(qoff, GRID_W), lanes[pair]]), kc_ref[:, lanes[pair]])
            return s_l, s_c

        def attend(unit, s_l, s_c):
            koff, brow, qoff, pair = unit
            mx = jnp.maximum(jnp.max(s_l, axis=-1, keepdims=True), jnp.max(s_c, axis=-1, keepdims=True))
            p_l = jnp.exp2(s_l - mx)
            p_c = jnp.exp2(s_c - mx)
            denom = jnp.sum(p_l, axis=-1, keepdims=True) + jnp.sum(p_c, axis=-1, keepdims=True)
            o = (_dot(p_l.astype(BF16), vwin[pl.ds(koff, win_keys), lanes[pair]])
                 + _dot(p_c.astype(BF16), vc_ref[:, lanes[pair]])) / denom
            o_ref[pl.ds(qoff, GRID_W), lanes[pair]] = jnp.where(
                even, o[0:GRID_W], o[GRID_W:2 * GRID_W]).astype(BF16)

        pending = [scores(unit) for unit in units[:NA_LEAD]]
        for n, unit in enumerate(units):
            if n + NA_LEAD < len(units):
                pending.append(scores(units[n + NA_LEAD]))
            attend(unit, *pending[n])
        return carry

    lax.fori_loop(0, Q_ROWS // NA_UNROLL, rows_body, 0)


def _na_call(q, qr, kr, v, kc, vc, rpb_rows, *, batch, seq, ctx_len):
    grid_rows = seq // GRID_W
    n_blocks = grid_rows // Q_ROWS
    n_groups = grid_rows // KV_GROUP
    blk = Q_ROWS * GRID_W
    grp = KV_GROUP * GRID_W

    win_spec = pl.BlockSpec((pl.Element(WIN_GROUPS * grp), pl.Element(NA_WIDTH)),
                            lambda b, i: ((b * n_groups + _window_group(i, n_groups)) * grp, 0))

    q_spec = pl.BlockSpec((blk, NA_WIDTH), lambda b, i: (b * n_blocks + i, 0))
    c_spec = pl.BlockSpec((ctx_len, NA_WIDTH), lambda b, i: (b, 0))
    return pl.pallas_call(
        functools.partial(_na_kernel, grid_rows),
        grid=(batch, n_blocks),
        in_specs=[q_spec, q_spec, win_spec, win_spec]
                 + [c_spec, c_spec, pl.BlockSpec(rpb_rows.shape, lambda b, i: (0, 0, 0))],
        out_specs=q_spec,
        out_shape=jax.ShapeDtypeStruct((batch * seq, NA_WIDTH), BF16),
        scratch_shapes=[pltpu.VMEM((2 * WIN_H - 2, NA_HEADS, GRID_W, 2 * GRID_W), F32)],
        compiler_params=_cparams("parallel", "arbitrary"),
        name="natten",
    )(q, qr, kr, v, kc, vc, rpb_rows)


def _block_diag(x_cat, bd_mask):
    xb = x_cat.astype(BF16)
    tiled = jnp.concatenate([xb] * (QUAD // CHUNK), axis=0)
    return jnp.where(bd_mask, tiled, jnp.zeros_like(tiled))


def _diag_blocks(x, head_of_lane):
    n = QUAD // HEAD_DIM
    out = x[(n - 1) * HEAD_DIM:n * HEAD_DIM]
    for h in range(n - 2, -1, -1):
        out = jnp.where(head_of_lane == h, x[h * HEAD_DIM:(h + 1) * HEAD_DIM], out)
    return out


def _rwkv_kernel(reverse, with_gate, n_ctx, n_blocks,
                 lat_ref, lat_prev_ref, lat_next_ref, ctx_ref, ctx_prev_ref, ctx_next_ref,
                 mup_ref, mun_ref, w0_ref, w2_ref, a0_ref, a2_ref, g2_ref,
                 kk_ref, ka_ref, rk_ref, ones_ref, *refs):
    if with_gate:
        acc_refs = (None, None)
        y_ref, bonus_ref, gate_ref = refs[0:3]
        refs = refs[3:]
    else:
        acc_refs = refs[0:2]
        y_ref, bonus_ref = refs[2:4]
        gate_ref = None
        refs = refs[4:]
    h_ref = refs[0]
    sets = (refs[1:7], refs[7:13])
    psets = (refs[13:17], refs[17:21])
    s = pl.program_id(1)

    last = n_blocks + 1

    def run(parity, stages):
        _rwkv_step(reverse, n_ctx, n_blocks, s, stages, sets[parity], sets[1 - parity], psets[parity],
                   psets[1 - parity], lat_ref, lat_prev_ref, lat_next_ref,
                   ctx_ref, ctx_prev_ref, ctx_next_ref, mup_ref, mun_ref, w0_ref, w2_ref, a0_ref, a2_ref,
                   g2_ref, kk_ref, ka_ref, rk_ref, ones_ref, acc_refs, y_ref, bonus_ref, gate_ref, h_ref)

    @pl.when(s == 0)
    def _():
        h_ref[...] = jnp.zeros_like(h_ref)
        for ref in psets[0]:
            ref[...] = jnp.zeros_like(ref)
        run(0, ("prep",))

    @pl.when(s == last)
    def _():
        run(last % 2, ("state",))

    inner = jnp.logical_and(s > 0, s < last)

    @pl.when(jnp.logical_and(inner, s % 2 == 0))
    def _():
        run(0, ("prep", "chains", "state"))

    @pl.when(jnp.logical_and(inner, s % 2 == 1))
    def _():
        run(1, ("prep", "chains", "state"))


def _rwkv_step(reverse, n_ctx, n_blocks, s, stages, wset, rset, pwset, prset, lat_ref, lat_prev_ref, lat_next_ref,
               ctx_ref, ctx_prev_ref, ctx_next_ref, mup_ref, mun_ref, w0_ref, w2_ref, a0_ref, a2_ref,
               g2_ref, kk_ref, ka_ref, rk_ref, ones_ref, acc_refs, y_ref, bonus_ref, gate_ref, h_ref):
    sp = jnp.minimum(s, n_blocks - 1)
    if reverse:
        n = jnp.where(sp < n_ctx, n_ctx - 1 - sp, n_blocks + n_ctx - 1 - sp)
    else:
        n = sp

    rows = lat_ref.shape[0]
    n_ch = rows // CHUNK
    w = RW_WIDTH
    t = {}

    is_ctx = n < n_ctx
    has_prev = jnp.logical_and(n != 0, n != n_ctx)
    has_next = jnp.logical_and(n != n_ctx - 1, n != n_blocks - 1)
    row = lax.broadcasted_iota(jnp.int32, (rows, 1), 0)

    def shifted(c0, c1):
        p = jnp.where(is_ctx, ctx_ref[:, c0:c1], lat_ref[:, c0:c1])
        prow = jnp.where(is_ctx, ctx_prev_ref[HALO - 1:HALO, c0:c1], lat_prev_ref[HALO - 1:HALO, c0:c1])
        nrow = jnp.where(is_ctx, ctx_next_ref[0:1, c0:c1], lat_next_ref[0:1, c0:c1])
        prow = jnp.where(has_prev, prow, 0.0)
        nrow = jnp.where(has_next, nrow, 0.0)
        prev = jnp.where(row == 0, prow, pltpu.roll(p, 1, 0))
        nxt = jnp.where(row == rows - 1, nrow, pltpu.roll(p, rows - 1, 0))
        return p + mup_ref[:, c0:c1] * (prev - p) + mun_ref[:, c0:c1] * (nxt - p)

    def prep_lora():
        lora = shifted(3 * w, 3 * w + 3 * LORA_W)
        pw = lora[:, 0:LORA_W]
        pa = lora[:, LORA_W:2 * LORA_W]
        t["log_decay"] = -math.exp(-0.5) * jax.nn.sigmoid(
            w0_ref[...] + _dot(jnp.tanh(pw).astype(BF16), w2_ref[...]))
        t["a"] = jax.nn.sigmoid(a0_ref[...] + _dot(pa.astype(BF16), a2_ref[...]))
        if gate_ref is not None:
            gate_ref[...] = _dot(jax.nn.sigmoid(lora[:, 2 * LORA_W:3 * LORA_W]).astype(BF16), g2_ref[...])

    def prep_keys():
        k = shifted(w, 2 * w)
        ones = ones_ref[0:RW_WIDTH, :]
        kk_raw = k * kk_ref[...]
        kk = kk_raw * lax.rsqrt(jnp.maximum(_head_sums(kk_raw * kk_raw, ones), 1e-24))
        t.update(kk=kk, kd=k * (1.0 + (t["a"] - 1.0) * ka_ref[...]), bb=t["a"] * kk)

    def prep_bonus():
        r = shifted(0, w)
        v = shifted(2 * w, 3 * w)
        bonus = _head_sums(r * t["kd"] * rk_ref[...], ones_ref[0:RW_WIDTH, :]) * v
        bonus_ref[...] = bonus if acc_refs[1] is None else bonus + acc_refs[1][...]
        t.update(r=r, v=v)

    def finish_prep():
        r, v, kk, kd, bb, log_decay = (t[name] for name in ("r", "v", "kk", "kd", "bb", "log_decay"))
        tri = ones_ref[RW_WIDTH:RW_WIDTH + rows, 0:rows]
        lw_hi, lw_lo = _split(log_decay, 2)
        cum = _dot(tri, lw_hi) + _dot(tri, lw_lo)
        e_neg = jnp.exp(-cum)
        last = 0 if reverse else CHUNK - 1
        w_v, w_kkt, w_rt, w_kh, w_bh, w_et = wset
        w_v[...] = v
        w_kkt[...] = kk * jnp.exp(cum - log_decay)
        w_rt[...] = r * jnp.exp(cum)
        w_kh[...] = kd * e_neg
        w_bh[...] = bb * e_neg
        for c in range(n_ch):
            w_et[c:c + 1, :] = jnp.exp(cum[c * CHUNK + last:c * CHUNK + last + 1, :])

    prep_at = {"start": prep_lora, "gram": prep_keys, "akv": prep_bonus, "dbl1": finish_prep}

    def issue_prep(point):
        if point in prep_at and "prep" in stages:
            prep_at[point]()

    issue_prep("start")
    r_v, r_kkt, r_rt, r_kh, r_bh, r_et = rset
    trow = lax.broadcasted_iota(jnp.int32, (CHUNK, QUAD), 0)
    tcol = lax.broadcasted_iota(jnp.int32, (CHUNK, QUAD), 1) % CHUNK
    strict = (trow < tcol) if reverse else (trow > tcol)
    incl = (trow <= tcol) if reverse else (trow >= tcol)
    on_diag = trow == tcol
    eye_cat = jnp.where(on_diag, 1.0, 0.0)
    brow = lax.broadcasted_iota(jnp.int32, (QUAD, QUAD), 0)
    bcol = lax.broadcasted_iota(jnp.int32, (QUAD, QUAD), 1)
    bd_mask = (brow // HEAD_DIM) == (bcol // HEAD_DIM)
    head_of_lane = lax.broadcasted_iota(jnp.int32, (CHUNK, QUAD), 1) // HEAD_DIM
    n_doublings = int(math.log2(CHUNK)) - 1
    heads = range(QUAD // HEAD_DIM)
    quads = range(RW_WIDTH // QUAD)
    order = list(range(n_ch - 1, -1, -1) if reverse else range(n_ch))

    chains = [(c, qd) for c in order for qd in quads]

    def rsl(c):
        return slice(c * CHUNK, (c + 1) * CHUNK)

    def lsl(qd):
        return slice(qd * QUAD, (qd + 1) * QUAD)

    r_y0, r_qt, r_g, r_h0 = prset
    h = {qd: h_ref[qd] for qd in quads}
    issued = [0]

    def issue_state(upto):
        while "state" in stages and issued[0] < min(upto, len(order)):
            c = order[issued[0]]
            issued[0] += 1
            for qd in quads:
                lhs = jnp.concatenate([r_g[rsl(c), lsl(qd)], r_qt[rsl(c), lsl(qd)]], axis=0).astype(BF16)
                o2 = _dot(lhs, _block_diag(h[qd], bd_mask))
                h[qd] = o2[0:CHUNK] + r_h0[rsl(c), lsl(qd)]
                y = r_y0[rsl(c), lsl(qd)] + o2[CHUNK:2 * CHUNK]
                y_ref[rsl(c), lsl(qd)] = y if acc_refs[0] is None else y + acc_refs[0][rsl(c), lsl(qd)]
            if issued[0] == len(order):
                for qd in quads:
                    h_ref[qd] = h[qd]

    issue_state(1)
    if "chains" not in stages:
        for point in ["gram", "akv", "m1"] + ["dbl%d" % step for step in range(n_doublings)]:
            issue_prep(point)
        issue_state(len(order))
        return
    v_q = {ch: r_v[rsl(ch[0]), lsl(ch[1])] for ch in chains}
    kkt_q = {ch: r_kkt[rsl(ch[0]), lsl(ch[1])] for ch in chains}
    rt_q = {ch: r_rt[rsl(ch[0]), lsl(ch[1])] for ch in chains}

    a_k, a_b, b_k, b_b = {}, {}, {}, {}
    for ch in chains:
        kh = r_kh[rsl(ch[0]), lsl(ch[1])].astype(BF16)
        bh = r_bh[rsl(ch[0]), lsl(ch[1])].astype(BF16)
        zero = jnp.zeros_like(kh)
        rhs = jnp.concatenate([jnp.where(head_of_lane == h, kh, zero) for h in heads]
                              + [jnp.where(head_of_lane == h, bh, zero) for h in heads], axis=0)
        lhs = jnp.concatenate([kkt_q[ch], rt_q[ch]], axis=0).astype(BF16)
        gram = _dot_nt(lhs, rhs)
        a_k[ch] = jnp.where(strict, gram[0:CHUNK, 0:QUAD], 0.0)
        a_b[ch] = jnp.where(strict, gram[0:CHUNK, QUAD:2 * QUAD], 0.0)
        b_k[ch] = jnp.where(incl, gram[CHUNK:2 * CHUNK, 0:QUAD], 0.0)
        b_b[ch] = jnp.where(incl, gram[CHUNK:2 * CHUNK, QUAD:2 * QUAD], 0.0).astype(BF16)
    issue_prep("gram")
    issue_state(2)

    akv, bkv = {}, {}
    for ch in chains:
        both = _dot(jnp.concatenate([a_k[ch], b_k[ch]], axis=0).astype(BF16), _block_diag(v_q[ch], bd_mask))
        akv[ch] = both[0:CHUNK]
        bkv[ch] = both[CHUNK:2 * CHUNK]
    issue_prep("akv")
    issue_state(3)

    m = {ch: -a_b[ch] for ch in chains}
    t_inv = {ch: eye_cat + m[ch] for ch in chains}
    for ch in chains:
        m[ch] = _dot(m[ch].astype(BF16), _block_diag(m[ch], bd_mask))
    issue_prep("m1")
    issue_state(4)
    for step in range(n_doublings):
        final = step == n_doublings - 1
        issue_prep("dbl%d" % step)
        for ch in chains:
            m_bd = _block_diag(m[ch], bd_mask)
            if final:
                t_inv[ch] = t_inv[ch] + _dot(t_inv[ch].astype(BF16), m_bd)
            else:
                both = _dot(jnp.concatenate([m[ch], t_inv[ch]], axis=0).astype(BF16), m_bd)
                m[ch] = both[0:CHUNK]
                t_inv[ch] = t_inv[ch] + both[CHUNK:2 * CHUNK]

    p1, tkk = {}, {}
    for ch in chains:
        t_b = t_inv[ch].astype(BF16)
        p1[ch] = _dot(t_b, _block_diag(akv[ch], bd_mask))
        tkk[ch] = _dot(t_b, _block_diag(kkt_q[ch], bd_mask))

    w_y0, w_qt, w_g, w_h0 = pwset
    for ch in chains:
        c, qd = ch
        w_y0[rsl(c), lsl(qd)] = bkv[ch] - _dot(b_b[ch], _block_diag(p1[ch], bd_mask))
        w_qt[rsl(c), lsl(qd)] = rt_q[ch] - _dot(b_b[ch], _block_diag(tkk[ch], bd_mask))
        e_q = r_et[c:c + 1, lsl(qd)]
        lhs_t = jnp.concatenate([r_kh[rsl(c), lsl(qd)] * e_q, r_bh[rsl(c), lsl(qd)] * e_q], axis=0).astype(BF16)
        rhs_t = jnp.concatenate([jnp.concatenate([v_q[ch], jnp.zeros_like(v_q[ch])], axis=1),
                                 jnp.concatenate([-p1[ch], -tkk[ch]], axis=1)], axis=0).astype(BF16)
        hg = _dot_tn(lhs_t, rhs_t)
        w_h0[rsl(c), lsl(qd)] = _diag_blocks(hg[:, 0:QUAD], head_of_lane)
        w_g[rsl(c), lsl(qd)] = _diag_blocks(hg[:, QUAD:2 * QUAD], head_of_lane) + jnp.where(on_diag, e_q, 0.0)
    issue_state(len(order))


def _rwkv_call(rw_lat, rw_ctx, weights, *, reverse, with_gate, batch, seq, ctx_len, acc=None):
    assert CHUNK == HEAD_DIM
    blk_rows = RW_BLOCK_CHUNKS * CHUNK
    assert ctx_len % blk_rows == 0 and seq % blk_rows == 0
    n_ctx = ctx_len // blk_rows
    n_lat = seq // blk_rows
    n_chunks = n_ctx + n_lat
    halo_per_blk = blk_rows // HALO

    def block_of(s):
        if reverse:
            return jnp.where(s < n_ctx, n_ctx - 1 - s, n_chunks + n_ctx - 1 - s)
        return s

    def stream_specs(first, count):
        n_halo = batch * count * halo_per_blk

        def blk(b, s):
            return b * count + jnp.clip(block_of(jnp.minimum(s, n_chunks - 1)) - first, 0, count - 1)

        return [pl.BlockSpec((blk_rows, RW_COLS), lambda b, s: (blk(b, s), 0)),
                pl.BlockSpec((HALO, RW_COLS), lambda b, s: (jnp.maximum(blk(b, s) * halo_per_blk - 1, 0), 0)),
                pl.BlockSpec((HALO, RW_COLS),
                             lambda b, s: (jnp.minimum((blk(b, s) + 1) * halo_per_blk, n_halo - 1), 0))]

    def out_block(b, step):
        lat = jnp.maximum(step, n_ctx) - n_ctx
        return (b * n_lat + (n_lat - 1 - lat if reverse else lat), 0)

    def const_spec(a):
        return pl.BlockSpec(a.shape, lambda b, s: (0,) * a.ndim, pipeline_mode=pl.Buffered(1))

    prep_spec = pl.BlockSpec((blk_rows, RW_WIDTH), lambda b, s: out_block(b, jnp.minimum(s, n_chunks - 1)))
    y_spec = pl.BlockSpec((blk_rows, RW_WIDTH), lambda b, s: out_block(b, jnp.maximum(s - 2, 0)))
    out_shape = jax.ShapeDtypeStruct((batch * seq, RW_WIDTH), F32)
    n_out = 3 if with_gate else 2
    assert with_gate == (acc is None)
    acc_specs = [] if acc is None else [y_spec, prep_spec]
    assert RW_BLOCK_CHUNKS <= SUBLANES
    term_set = [pltpu.VMEM((blk_rows, RW_WIDTH), F32)] * 5 + [pltpu.VMEM((SUBLANES, RW_WIDTH), F32)]
    chain_set = [pltpu.VMEM((blk_rows, RW_WIDTH), F32)] * 4
    return pl.pallas_call(
        functools.partial(_rwkv_kernel, reverse, with_gate, n_ctx, n_chunks),
        grid=(batch, n_chunks + 2),
        in_specs=stream_specs(n_ctx, n_lat) + stream_specs(0, n_ctx) + [const_spec(a) for a in weights] + acc_specs,
        out_specs=[y_spec] + [prep_spec] * (n_out - 1),
        out_shape=[out_shape] * n_out,
        scratch_shapes=[pltpu.VMEM((RW_WIDTH // QUAD, CHUNK, QUAD), F32)] + term_set * 2 + chain_set * 2,
        compiler_params=_cparams("parallel", "arbitrary"),
        name="rwkv_bwd" if reverse else "rwkv_fwd",
    )(rw_lat, rw_lat, rw_lat, rw_ctx, rw_ctx, rw_ctx, *weights, *(acc or ()))


def _tail_kernel(d_model, alpha, ff_chunk, yna_ref, y_ref, bonus_ref, gate_ref, x_ref, moda_ref, modc_ref,
                 gng_ref, gnb_ref, wo_ref, ln1g_ref, ln1b_ref, w1_ref, w2_ref, ln2g_ref, ln2b_ref,
                 o_ref, x1_a, xm_a, acc_a, x1_b, xm_b, acc_b):
    i = pl.program_id(0)
    last = pl.num_programs(0) - 1

    def step(stages, x1_cur, xm_cur, acc_cur, xm_prev, acc_prev):
        inv_n = 1.0 / HEAD_DIM
        n_chunks = w1_ref.shape[1] // ff_chunk
        t = {}

        def lane_head_sums(z):
            lane = lax.broadcasted_iota(jnp.int32, (z.shape[0], 2 * HEAD_DIM), 1)
            first = lane < HEAD_DIM
            slabs = []
            for p in range(z.shape[1] // (2 * HEAD_DIM)):
                zs = z[:, p * 2 * HEAD_DIM:(p + 1) * 2 * HEAD_DIM]
                both = jnp.sum(zs, axis=-1, keepdims=True)
                lo = jnp.sum(jnp.where(first, zs, 0.0), axis=-1, keepdims=True)
                slabs.append(jnp.where(first, lo, both - lo))
            return jnp.concatenate(slabs, axis=1)

        def readout_stats():
            y = y_ref[...]
            mu = lane_head_sums(y) * inv_n
            yc = y - mu
            t.update(yc=yc, var=lane_head_sums(yc * yc) * inv_n)

        def out_projection():
            yn = t["yc"] * lax.rsqrt(t["var"] + GN_EPS) * gng_ref[...] + gnb_ref[...]
            y_rw = ((yn + bonus_ref[...]) * gate_ref[...]).astype(BF16)
            t["proj"] = (_dot(yna_ref[...], wo_ref[0:NA_WIDTH, :])
                         + _dot(y_rw, wo_ref[NA_WIDTH:NA_WIDTH + RW_WIDTH, :]))

        half_rows = x_ref.shape[0] // 2

        def norm_modulate(part):
            rows = slice(part * half_rows, (part + 1) * half_rows)
            mod = moda_ref[...]
            g1 = mod[:, 2 * d_model:3 * d_model]
            shift = mod[:, 3 * d_model:4 * d_model]
            scale = mod[:, 4 * d_model:5 * d_model]
            x1 = (_normalize(alpha * x_ref[rows, :] + g1 * t["proj"][rows, :], LN_EPS) * ln1g_ref[...]
                  + ln1b_ref[...])
            x1_cur[rows, :] = x1
            xm_cur[rows, :] = (_normalize(x1, LN_EPS) * (1.0 + scale) + shift).astype(BF16)

        def final_norm(part):
            rows = slice(part * half_rows, (part + 1) * half_rows)
            g2 = modc_ref[...][:, 5 * d_model:6 * d_model]
            o_ref[rows, :] = (_normalize(alpha * x1_cur[rows, :] + g2 * acc_cur[rows, :], LN_EPS) * ln2g_ref[...]
                              + ln2b_ref[...])

        others = [(min(k, n_chunks - 1), name, stage) for k, name, stage in
                  ((0, "a", readout_stats), (0, "a", out_projection),
                   (1, "c", functools.partial(final_norm, 0)), (1, "a", functools.partial(norm_modulate, 0)),
                   (2, "c", functools.partial(final_norm, 1)), (2, "a", functools.partial(norm_modulate, 1)))]
        if "b" not in stages:
            for _, name, stage in others:
                if name in stages:
                    stage()
            return
        xm = xm_prev[...]
        acc = None
        for c in range(n_chunks):
            c0 = c * ff_chunk
            hid = jnp.maximum(_dot(xm, w1_ref[:, c0:c0 + ff_chunk]), 0.0)
            part = _dot((hid * hid).astype(BF16), w2_ref[c0:c0 + ff_chunk, :])
            acc = part if acc is None else acc + part
            for after, name, stage in others:
                if after == c and name in stages:
                    stage()
        acc_prev[...] = acc

    slots = ((x1_a, xm_a, acc_a, xm_b, acc_b), (x1_b, xm_b, acc_b, xm_a, acc_a))

    @pl.when(i == 0)
    def _():
        x1_b[...] = jnp.zeros_like(x1_b)
        acc_b[...] = jnp.zeros_like(acc_b)
        step("a", *slots[0])

    inner = jnp.logical_and(i > 0, i < last)

    @pl.when(jnp.logical_and(inner, i % 2 == 0))
    def _():
        step("abc", *slots[0])

    @pl.when(jnp.logical_and(inner, i % 2 == 1))
    def _():
        step("abc", *slots[1])

    @pl.when(jnp.logical_and(i == last, i % 2 == 0))
    def _():
        step("c", *slots[0])

    @pl.when(jnp.logical_and(i == last, i % 2 == 1))
    def _():
        step("c", *slots[1])


def _tail_call(y_na, y, bonus, gate, x2d, mod3, consts, *, tm, seq, alpha):
    rows, d = x2d.shape
    tiles_per_batch = seq // tm
    n_tiles = rows // tm

    def tile_a(i):
        return jnp.minimum(i, n_tiles - 1)

    def tile_c(i):
        return jnp.maximum(i - 2, 0)

    half = pl.BlockSpec((tm, RW_WIDTH), lambda i: (tile_a(i), 0))

    def resident(a):
        return pl.BlockSpec(a.shape, lambda i: (0,) * a.ndim, pipeline_mode=pl.Buffered(1))

    return pl.pallas_call(
        functools.partial(_tail_kernel, d, alpha, FF_CHUNK),
        grid=(n_tiles + 2,),
        in_specs=[half] * 4 + [pl.BlockSpec((tm, d), lambda i: (tile_a(i), 0)),
                               pl.BlockSpec((None, 1, mod3.shape[2]), lambda i: (tile_a(i) // tiles_per_batch, 0, 0)),
                               pl.BlockSpec((None, 1, mod3.shape[2]), lambda i: (tile_c(i) // tiles_per_batch, 0, 0))]
                 + [resident(a) for a in consts],
        out_specs=pl.BlockSpec((tm, d), lambda i: (tile_c(i), 0)),
        out_shape=jax.ShapeDtypeStruct((rows, d), F32),
        scratch_shapes=[pltpu.VMEM((tm, d), F32), pltpu.VMEM((tm, d), BF16), pltpu.VMEM((tm, d), F32)] * 2,
        compiler_params=_cparams("arbitrary"),
        name="tail",
    )(y_na, y, bonus, gate, x2d, mod3, mod3, *consts)


def _rope_tables(seq):
    f = HEAD_DIM // 4
    t = np.arange(seq)
    row = (t // GRID_W).astype(np.float32)
    col = (t % GRID_W).astype(np.float32)
    inv = (ROPE_BASE ** (-np.arange(f, dtype=np.float32) / f)).astype(np.float32)
    ang_r = row[:, None] * inv[None, :]
    ang_c = col[:, None] * inv[None, :]
    zero = np.zeros_like(ang_r)
    cos = np.concatenate([np.cos(ang_r), np.cos(ang_r), np.cos(ang_c), np.cos(ang_c)], axis=1)
    sa = np.concatenate([-np.sin(ang_r), zero, -np.sin(ang_c), zero], axis=1)
    sb = np.concatenate([zero, np.sin(ang_r), zero, np.sin(ang_c)], axis=1)
    return tuple(jnp.asarray(np.concatenate([z, z], axis=1), F32) for z in (cos, sa, sb))


def _bias_rows(rpb):
    pad = GRID_W - WIN_W
    padded = jnp.pad(rpb.astype(F32) * LOG2_E, ((0, 0), (0, 0), (pad, pad + 1)))
    return padded.transpose(1, 0, 2)


def _scan_consts(ones, reverse):
    rows = RW_BLOCK_CHUNKS * CHUNK
    t = np.arange(rows)
    same_chunk = (t[:, None] // CHUNK) == (t[None, :] // CHUNK)
    tri = same_chunk & ((t[None, :] >= t[:, None]) if reverse else (t[None, :] <= t[:, None]))
    tri = np.pad(tri.astype(np.float32), ((0, 0), (0, RW_WIDTH - rows)))
    return jnp.concatenate([ones, jnp.asarray(tri, BF16)], axis=0)


def _pad_lora(w, d):
    z = jnp.zeros_like(w[d])
    return jnp.concatenate([w[0] if d == 0 else z, w[1] if d == 1 else z], axis=0)


def kernel(x, c, ctx, c_ctx, w_mod, b_mod, w_in, w_out, ln1_g, ln1_b, mlp_w1, mlp_w2, ln2_g, ln2_b, na_rpb,
           rw_mu_prev, rw_mu_next, rw_w0, rw_w2, rw_a0, rw_a2, rw_g2, rw_k_k, rw_k_a, rw_r_k, rw_gn_g, rw_gn_b):
    depth = w_mod.shape[0]
    assert depth == 1, "single-layer trunk only (the context stream is never updated)"
    batch, seq, d = x.shape
    ctx_len = ctx.shape[1]
    alpha = (2 * depth) ** 0.25
    tm = ROW_TILE
    tm_ctx = math.gcd(ctx_len, ROW_TILE)
    assert seq % (Q_ROWS * GRID_W) == 0 and seq % tm == 0 and tm_ctx % 8 == 0
    assert batch + 1 <= SUBLANES

    cc = jnp.zeros((SUBLANES, d), F32).at[:batch].set(c).at[batch].set(c_ctx)
    mod3 = _mod_call(cc, w_mod[0], b_mod[0]).reshape(SUBLANES, 1, N_MOD * d)

    w_in_b = w_in[0].astype(BF16)
    tiles_lat = seq // tm
    x2d = x.reshape(batch * seq, d)
    q, qr, kr, v, rw_lat = _inproj_call(
        x2d, mod3, _rope_tables(seq), w_in_b, tm=tm,
        mod_index=lambda i: i // tiles_lat, table_index=lambda i: i % tiles_lat, name="inproj")
    ident = tuple(jnp.full((tm_ctx, 2 * HEAD_DIM), val, F32) for val in (1.0, 0.0, 0.0))
    _, _, kc, vc, rw_ctx = _inproj_call(
        ctx.reshape(batch * ctx_len, d), mod3, ident, w_in_b, tm=tm_ctx,
        mod_index=lambda i: batch, table_index=lambda i: 0, name="inproj_ctx")

    y_na = _na_call(q, qr, kr, v, kc, vc, _bias_rows(na_rpb[0]), batch=batch, seq=seq, ctx_len=ctx_len)

    lane_head = jnp.arange(RW_WIDTH) // HEAD_DIM
    ones = (lane_head[:, None] == lane_head[None, :]).astype(BF16)
    row = lambda a: a.reshape(1, -1).astype(F32)
    def dir_weights(dirn):
        return (row(rw_mu_prev[0]), row(rw_mu_next[0]), row(rw_w0[0, dirn]),
                _pad_lora(rw_w2[0], dirn).astype(BF16), row(rw_a0[0, dirn]),
                _pad_lora(rw_a2[0], dirn).astype(BF16), rw_g2[0].astype(BF16),
                row(rw_k_k[0]), row(rw_k_a[0]), row(rw_r_k[0]), _scan_consts(ones, dirn == 1))

    y_f, bonus_f, gate = _rwkv_call(rw_lat, rw_ctx, dir_weights(0), reverse=False, with_gate=True,
                                    batch=batch, seq=seq, ctx_len=ctx_len)
    y_sum, bonus_sum = _rwkv_call(rw_lat, rw_ctx, dir_weights(1), reverse=True, with_gate=False,
                                  batch=batch, seq=seq, ctx_len=ctx_len, acc=(y_f, bonus_f))

    consts = (row(rw_gn_g[0]), row(rw_gn_b[0]), w_out[0].astype(BF16), row(ln1_g[0]), row(ln1_b[0]),
              mlp_w1[0].astype(BF16), mlp_w2[0].astype(BF16), row(ln2_g[0]), row(ln2_b[0]))
    out = _tail_call(y_na, y_sum, bonus_sum, gate, x2d, mod3, consts, tm=tm, seq=seq, alpha=alpha)
    return out.reshape(batch, seq, d)
```
